```python
import math, functools
import jax, jax.numpy as jnp
from jax import lax
import numpy as np

D_MODEL = 1024
BATCH = 2
SEQ = 8192
DEPTH = 2
DEC_BATCH = 128
DEC_SEQ = 4
PAST_LEN = 8192
PAGE_SIZE = 128

MIX_WIDTH = D_MODEL
M_HEADS = 4
M_DV = (MIX_WIDTH // 2) // M_HEADS
M_DK = M_DV // 2
M_WIDTH = M_HEADS * M_DV
M_CHUNK = 64
GATE_CAP = 15.0
A_HEADS = 4
A_NOPE = 128
A_ROPE = 64
A_DV = (MIX_WIDTH - M_WIDTH) // A_HEADS
A_WIDTH = A_HEADS * A_DV
Q_RANK = 384
KV_RANK = 256
LAT_DIM = KV_RANK + A_ROPE
ROPE_THETA = 10000.0
Q_BLOCK = 128
D_FF = 2816
CONV_W = 3
PLE_DIM = 256
EPS = 1e-6
IN_SIZES = (M_HEADS * M_DK, M_HEADS * M_DK, M_WIDTH, M_WIDTH, M_HEADS, M_HEADS, Q_RANK, KV_RANK, A_ROPE)

kernel_name = 'hymba_mlstm_mla_convffn_step'


def rmsnorm(x, g):
    xf = x.astype(jnp.float32)
    xf = xf * lax.rsqrt(jnp.mean(xf * xf, axis=-1, keepdims=True) + EPS)
    return (xf * g.astype(jnp.float32)).astype(x.dtype)


def softcap(x):
    return GATE_CAP * jnp.tanh(x / GATE_CAP)


def rope(x, pos):
    half = A_ROPE // 2
    freqs = ROPE_THETA ** (-jnp.arange(half, dtype=jnp.float32) * 2.0 / A_ROPE)
    ang = pos.astype(jnp.float32)[:, None] * freqs[None, :]
    ang = ang.reshape((ang.shape[0],) + (1,) * (x.ndim - 3) + (half,))
    cos, sin = jnp.cos(ang), jnp.sin(ang)
    xf = x.astype(jnp.float32)
    x1, x2 = xf[..., :half], xf[..., half:]
    return jnp.concatenate([x1 * cos - x2 * sin, x2 * cos + x1 * sin], axis=-1).astype(x.dtype)


def mlstm_chunkwise(q, k, v, ig, lf, C0, n0, m0):
    B, T = q.shape[0], q.shape[1]
    L = math.gcd(T, M_CHUNK)
    NC = T // L
    f32 = jnp.float32

    def to_chunks(a):
        return jnp.moveaxis(a.astype(f32).reshape((B, NC, L) + a.shape[2:]), 1, 0)

    causal = jnp.tril(jnp.ones((L, L), dtype=bool))

    def step(carry, xs):
        C, n, m = carry
        qc, kc, vc, ic, fc = xs
        b = jnp.cumsum(fc, axis=1)
        dmat = b[:, :, None, :] - b[:, None, :, :] + ic[:, None, :, :]
        dmat = jnp.where(causal[None, :, :, None], dmat, -jnp.inf)
        inter = b + m[:, None, :]
        m_t = jnp.maximum(inter, jnp.max(dmat, axis=2))
        s = jnp.einsum('bqhd,bkhd->bqkh', qc, kc) * jnp.exp(dmat - m_t[:, :, None, :])
        a = jnp.exp(inter - m_t)
        num = jnp.einsum('bqkh,bkhv->bqhv', s, vc) + a[..., None] * jnp.einsum('bqhd,bhdv->bqhv', qc, C)
        nq = jnp.sum(s, axis=2) + a * jnp.einsum('bqhd,bhd->bqh', qc, n)
        den = jnp.maximum(jnp.abs(nq), jnp.exp(-m_t))
        h = num / den[..., None]
        b_last = b[:, -1]
        m_new = m_t[:, -1]
        wk = jnp.exp(b_last[:, None, :] - b + ic - m_new[:, None, :])
        decay = jnp.exp(b_last + m - m_new)
        C_new = decay[..., None, None] * C + jnp.einsum('blh,blhd,blhv->bhdv', wk, kc, vc)
        n_new = decay[..., None] * n + jnp.einsum('blh,blhd->bhd', wk, kc)
        return (C_new, n_new, m_new), h

    init = (C0.astype(f32), n0.astype(f32), m0.astype(f32))
    (C, n, m), hs = lax.scan(step, init, (to_chunks(q), to_chunks(k), to_chunks(v), to_chunks(ig), to_chunks(lf)))
    h = jnp.moveaxis(hs, 0, 1).reshape(B, T, q.shape[2], v.shape[3])
    return h, C, n, m


def mla_attend_prompt(qcat, lat):
    B, T, H, _ = qcat.shape
    qb = min(Q_BLOCK, T)
    nb = T // qb
    scale = (A_NOPE + A_ROPE) ** -0.5
    kpos = jnp.arange(T)
    vals = lat[..., :KV_RANK]

    def one_block(i):
        q = lax.dynamic_slice_in_dim(qcat, i * qb, qb, axis=1)
        s = jnp.einsum('bqhd,bkd->bhqk', q, lat).astype(jnp.float32) * scale
        qpos = i * qb + jnp.arange(qb)
        s = jnp.where(kpos[None, :] <= qpos[:, None], s, -jnp.inf)
        p = jax.nn.softmax(s, axis=-1).astype(vals.dtype)
        return jnp.einsum('bhqk,bkr->bqhr', p, vals)

    o = lax.map(one_block, jnp.arange(nb))
    return jnp.moveaxis(o, 0, 1).reshape(B, T, H, KV_RANK)


def mla_attend_sample(qcat, lat, past):
    T = qcat.shape[1]
    P = past.shape[1]
    scale = (A_NOPE + A_ROPE) ** -0.5
    s_past = jnp.einsum('bqhd,bkd->bhqk', qcat, past).astype(jnp.float32) * scale
    s_new = jnp.einsum('bqhd,bkd->bhqk', qcat, lat).astype(jnp.float32) * scale
    s_new = jnp.where(jnp.tril(jnp.ones((T, T), dtype=bool)), s_new, -jnp.inf)
    p = jax.nn.softmax(jnp.concatenate([s_past, s_new], axis=-1), axis=-1)
    o = jnp.einsum('bhqk,bkr->bqhr', p[..., :P].astype(past.dtype), past[..., :KV_RANK])
    return o + jnp.einsum('bhqk,bkr->bqhr', p[..., P:].astype(lat.dtype), lat[..., :KV_RANK])


def layer(x, p_in, pos, attend, C0, n0, m0, buf0, lw):
    B, T, _ = x.shape
    h = rmsnorm(x, lw['norm_mix'])
    z = h @ lw['w_in']
    splits = np.cumsum(IN_SIZES)[:-1].tolist()
    qm, km, vm, om, im, fm, cq, ckv, kr = jnp.split(z, splits, axis=-1)
    qm = qm.reshape(B, T, M_HEADS, M_DK)
    km = km.reshape(B, T, M_HEADS, M_DK) * (M_DK ** -0.5)
    vm = vm.reshape(B, T, M_HEADS, M_DV)
    gb = lw['m_gate_bias'].astype(jnp.float32)
    ig = softcap(im.astype(jnp.float32) + gb[0])
    lf = jax.nn.log_sigmoid(softcap(fm.astype(jnp.float32) + gb[1]))
    hm, C, n, m = mlstm_chunkwise(qm, km, vm, ig, lf, C0, n0, m0)
    hm = rmsnorm(hm.astype(x.dtype), lw['m_norm']).reshape(B, T, M_WIDTH) * jax.nn.sigmoid(om)
    lat = jnp.concatenate([rmsnorm(ckv, lw['mla_kv_norm']), rope(kr, pos)], axis=-1)
    qa = (rmsnorm(cq, lw['mla_q_norm']) @ lw['mla_w_uq']).reshape(B, T, A_HEADS, A_NOPE + A_ROPE)
    q_lat = jnp.einsum('bthn,rhn->bthr', qa[..., :A_NOPE], lw['mla_w_uk'])
    qcat = jnp.concatenate([q_lat, rope(qa[..., A_NOPE:], pos)], axis=-1)
    o_lat = attend(qcat, lat)
    ha = jnp.einsum('bthr,rhv->bthv', o_lat, lw['mla_w_uv']).reshape(B, T, A_WIDTH)
    x = x + jnp.concatenate([hm, ha], axis=-1) @ lw['w_out']
    u = rmsnorm(x, lw['norm_ffn']) @ lw['ffn_w_up']
    u_pad = jnp.concatenate([buf0.astype(u.dtype), u], axis=1)
    wc = lw['ffn_conv']
    uc = sum(u_pad[:, j:j + T] * wc[j] for j in range(CONV_W)) + lw['ffn_conv_b']
    gate, up = jnp.split(uc, 2, axis=-1)
    x = x + (jax.nn.silu(gate) * up) @ lw['ffn_w_down']
    new_buf = u_pad[:, -(CONV_W - 1):]
    x = x + jax.nn.sigmoid(rmsnorm(x, lw['ple_norm']) @ lw['ple_w_gate']) * (p_in @ lw['ple_w_proj'])
    return x, lat, C, n, m, new_buf


def setup_inputs(seed: int = 0) -> dict:
    key = jax.random.key(seed)
    ks = iter(jax.random.split(key, 40))

    def nrm(shape, scale=1.0):
        return scale * jax.random.normal(next(ks), shape, jnp.float32)

    def gain(shape):
        return 1.0 + 0.02 * jax.random.normal(next(ks), shape, jnp.float32)

    n_pages = PAST_LEN // PAGE_SIZE
    n_used = DEC_BATCH * n_pages
    n_pool = n_used + n_used // 4
    in_cols = sum(IN_SIZES)
    return {
        'x_prompt': nrm((BATCH, SEQ, D_MODEL)),
        'x_sample': nrm((DEC_BATCH, DEC_SEQ, D_MODEL)),
        'p_prompt': nrm((DEPTH, BATCH, SEQ, PLE_DIM)),
        'p_sample': nrm((DEPTH, DEC_BATCH, DEC_SEQ, PLE_DIM)),
        'cache_mla': nrm((DEPTH, n_pool, PAGE_SIZE, LAT_DIM)),
        'state_mlstm_C': nrm((DEPTH, DEC_BATCH, M_HEADS, M_DK, M_DV), 0.5),
        'state_mlstm_n': nrm((DEPTH, DEC_BATCH, M_HEADS, M_DK), 0.5),
        'state_mlstm_m': nrm((DEPTH, DEC_BATCH, M_HEADS)),
        'state_ffn_conv': nrm((DEPTH, DEC_BATCH, CONV_W - 1, 2 * D_FF)),
        'page_table': jax.random.permutation(next(ks), n_pool)[:n_used].reshape(DEC_BATCH, n_pages).astype(jnp.int32),
        'norm_mix': gain((DEPTH, D_MODEL)),
        'w_in': nrm((DEPTH, D_MODEL, in_cols), D_MODEL ** -0.5),
        'm_gate_bias': jnp.stack([nrm((DEPTH, M_HEADS), 0.1),
                                  jnp.linspace(3.0, 6.0, M_HEADS)[None, :] + nrm((DEPTH, M_HEADS), 0.1)], axis=1),
        'm_norm': gain((DEPTH, M_HEADS, M_DV)),
        'mla_q_norm': gain((DEPTH, Q_RANK)),
        'mla_w_uq': nrm((DEPTH, Q_RANK, A_HEADS * (A_NOPE + A_ROPE)), Q_RANK ** -0.5),
        'mla_kv_norm': gain((DEPTH, KV_RANK)),
        'mla_w_uk': nrm((DEPTH, KV_RANK, A_HEADS, A_NOPE), KV_RANK ** -0.5),
        'mla_w_uv': nrm((DEPTH, KV_RANK, A_HEADS, A_DV), KV_RANK ** -0.5),
        'w_out': nrm((DEPTH, MIX_WIDTH, D_MODEL), MIX_WIDTH ** -0.5),
        'norm_ffn': gain((DEPTH, D_MODEL)),
        'ffn_w_up': nrm((DEPTH, D_MODEL, 2 * D_FF), D_MODEL ** -0.5),
        'ffn_conv': nrm((DEPTH, CONV_W, 2 * D_FF), CONV_W ** -0.5),
        'ffn_conv_b': nrm((DEPTH, 2 * D_FF), 0.02),
        'ffn_w_down': nrm((DEPTH, D_FF, D_MODEL), D_FF ** -0.5),
        'ple_norm': gain((DEPTH, D_MODEL)),
        'ple_w_gate': nrm((DEPTH, D_MODEL, D_MODEL), D_MODEL ** -0.5),
        'ple_w_proj': nrm((DEPTH, PLE_DIM, D_MODEL), PLE_DIM ** -0.5),
        'final_norm': gain((D_MODEL,)),
    }


def reference(x_prompt, x_sample, p_prompt, p_sample, cache_mla, state_mlstm_C, state_mlstm_n, state_mlstm_m,
              state_ffn_conv, page_table, norm_mix, w_in, m_gate_bias, m_norm, mla_q_norm, mla_w_uq, mla_kv_norm,
              mla_w_uk, mla_w_uv, w_out, norm_ffn, ffn_w_up, ffn_conv, ffn_conv_b, ffn_w_down, ple_norm,
              ple_w_gate, ple_w_proj, final_norm):
    Bp, Tp = x_prompt.shape[0], x_prompt.shape[1]
    Bs, Ts = x_sample.shape[0], x_sample.shape[1]
    pos_p = jnp.arange(Tp)
    past_len = page_table.shape[1] * PAGE_SIZE
    pos_s = past_len + jnp.arange(Ts)
    C_p0 = jnp.zeros((Bp, M_HEADS, M_DK, M_DV), jnp.float32)
    n_p0 = jnp.zeros((Bp, M_HEADS, M_DK), jnp.float32)
    m_p0 = jnp.zeros((Bp, M_HEADS), jnp.float32)
    buf_p0 = jnp.zeros((Bp, CONV_W - 1, 2 * D_FF), x_prompt.dtype)
    xp, xs = x_prompt, x_sample
    lat_p, lat_s, Cp, np_, mp, Cs, ns, ms, bp, bs = ([] for _ in range(10))
    for l in range(DEPTH):
        lw = {'norm_mix': norm_mix[l], 'w_in': w_in[l], 'm_gate_bias': m_gate_bias[l], 'm_norm': m_norm[l],
              'mla_q_norm': mla_q_norm[l], 'mla_w_uq': mla_w_uq[l], 'mla_kv_norm': mla_kv_norm[l],
              'mla_w_uk': mla_w_uk[l], 'mla_w_uv': mla_w_uv[l], 'w_out': w_out[l], 'norm_ffn': norm_ffn[l],
              'ffn_w_up': ffn_w_up[l], 'ffn_conv': ffn_conv[l], 'ffn_conv_b': ffn_conv_b[l],
              'ffn_w_down': ffn_w_down[l], 'ple_norm': ple_norm[l], 'ple_w_gate': ple_w_gate[l],
              'ple_w_proj': ple_w_proj[l]}
        xp, lat, C, n, m, buf = layer(xp, p_prompt[l], pos_p, mla_attend_prompt, C_p0, n_p0, m_p0, buf_p0, lw)
        lat_p.append(lat); Cp.append(C); np_.append(n); mp.append(m); bp.append(buf)
        past = cache_mla[l, page_table].reshape(Bs, past_len, LAT_DIM)
        xs, lat, C, n, m, buf = layer(xs, p_sample[l], pos_s, functools.partial(mla_attend_sample, past=past),
                                      state_mlstm_C[l], state_mlstm_n[l], state_mlstm_m[l], state_ffn_conv[l], lw)
        lat_s.append(lat); Cs.append(C); ns.append(n); ms.append(m); bs.append(buf)
    y_prompt = rmsnorm(xp, final_norm)
    y_sample = rmsnorm(xs, final_norm)
    lat_prompt, lat_sample = jnp.stack(lat_p), jnp.stack(lat_s)
    C_prompt, n_prompt, m_prompt = jnp.stack(Cp), jnp.stack(np_), jnp.stack(mp)
    C_sample, n_sample, m_sample = jnp.stack(Cs), jnp.stack(ns), jnp.stack(ms)
    conv_prompt, conv_sample = jnp.stack(bp), jnp.stack(bs)
    return (y_prompt, y_sample, lat_prompt, lat_sample, C_prompt, n_prompt, m_prompt,
            C_sample, n_sample, m_sample, conv_prompt, conv_sample)
```

```python
import functools

import jax
import jax.numpy as jnp
from jax import lax
from jax.experimental import pallas as pl
from jax.experimental.pallas import tpu as pltpu

F32 = jnp.float32
BF16 = jnp.bfloat16

M_HEADS = 4
M_DK = 64
M_DV = 128
A_HEADS = 4
A_NOPE = 128
A_ROPE = 64
KV_RANK = 256
Q_RANK = 384
LAT_DIM = KV_RANK + A_ROPE
GATE_CAP = 15.0
ROPE_THETA = 10000.0
EPS = 1e-6
CONV_W = 3
ATT_SCALE = (A_NOPE + A_ROPE) ** -0.5

LANES = 128
SUBLANES = 8
VMEM_LIMIT_BYTES = 56 * 1024 * 1024

LAT_PAD = 3 * LANES
TOK_TILE = 256
MLSTM_CHUNK = 256
ATT_BLOCK = 512
SAMPLE_GROUP = 32
FF_BLOCK = 256
SAMPLE_FFN_TILE = 128

C_Q, C_K, C_V, C_O = 0, 256, 512, 1024
C_CQ, C_CKV, C_KR, C_G = 1536, 1920, 2176, 2304
IN_COLS = 2432


def _const_spec(shape):
    nd = len(shape)
    return pl.BlockSpec(shape, lambda *_: (0,) * nd, pipeline_mode=pl.Buffered(1))


def _params(n_axes):
    return pltpu.CompilerParams(dimension_semantics=("arbitrary",) * n_axes,
                                vmem_limit_bytes=VMEM_LIMIT_BYTES)


def _rms(x, g):
    return x * lax.rsqrt(jnp.mean(x * x, axis=-1, keepdims=True) + EPS) * g


def _sigmoid(x):
    return 1.0 / (1.0 + jnp.exp(-x))


def _dot(a, b):
    return jnp.dot(a, b, preferred_element_type=F32)


def _dot_nt(a, b):
    return lax.dot_general(a, b, (((1,), (1,)), ((), ())), preferred_element_type=F32)


def _fold_halves(p):
    return p + pltpu.roll(p, 64, 1)


def _proj_kernel(x_ref, tab_ref, gmix_ref, win_ref, gb_ref, gq_ref, wuq_ref, gkv_ref, wuk_ref,
                 qk_ref, v_ref, o_ref, gates_ref, lat_ref, katt_ref, qatt_ref):
    h = _rms(x_ref[...], gmix_ref[...]).astype(BF16)
    z = _dot(h, win_ref[...])
    qk_ref[:, 0:256] = z[:, C_Q:C_Q + 256].astype(BF16)
    qk_ref[:, 256:512] = (z[:, C_K:C_K + 256] * (M_DK ** -0.5)).astype(BF16)
    v_ref[...] = z[:, C_V:C_V + 512].astype(BF16)
    o_ref[...] = z[:, C_O:C_O + 512]

    gt = z[:, C_G:C_G + LANES].T[0:SUBLANES, :] + gb_ref[...]
    capped = GATE_CAP * jnp.tanh(gt / GATE_CAP)
    log_sig = jnp.minimum(capped, 0.0) - jnp.log1p(jnp.exp(-jnp.abs(capped)))
    row = lax.broadcasted_iota(jnp.int32, gt.shape, 0)
    gates_ref[...] = jnp.where(row < M_HEADS, log_sig, capped)

    tab = tab_ref[...]
    lane = lax.broadcasted_iota(jnp.int32, tab.shape, 1)
    ckv_n = _rms(z[:, C_CKV:C_CKV + KV_RANK], gkv_ref[...])
    rk = _fold_halves(z[:, C_KR:C_KR + LANES] * tab)
    lat_ref[:, 0:KV_RANK] = ckv_n
    lat_ref[:, KV_RANK:LAT_DIM] = rk[:, 0:A_ROPE]
    katt_ref[...] = jnp.concatenate([ckv_n, jnp.where(lane < A_ROPE, rk, 0.0)], axis=1).astype(BF16)

    qn = _rms(z[:, C_CQ:C_CQ + Q_RANK], gq_ref[...]).astype(BF16)
    qa = _dot(qn, wuq_ref[...])
    for hd in range(A_HEADS):
        q_lat = _dot(qa[:, hd * A_NOPE:(hd + 1) * A_NOPE].astype(BF16), wuk_ref[hd])
        rq = _fold_halves(qa[:, 512 + hd * LANES:512 + (hd + 1) * LANES] * tab)
        q_cat = jnp.concatenate([q_lat, jnp.where(lane < A_ROPE, rq, 0.0)], axis=1)
        qatt_ref[hd] = (q_cat * ATT_SCALE).astype(BF16)


def _proj(x, tab, lw):
    n = x.shape[0]
    tm = TOK_TILE
    row = lambda i: (i, 0)
    out_shape = (
        jax.ShapeDtypeStruct((n, 512), BF16),
        jax.ShapeDtypeStruct((n, 512), BF16),
        jax.ShapeDtypeStruct((n, 512), F32),
        jax.ShapeDtypeStruct((SUBLANES, n), F32),
        jax.ShapeDtypeStruct((n, LAT_DIM), F32),
        jax.ShapeDtypeStruct((n, LAT_PAD), BF16),
        jax.ShapeDtypeStruct((A_HEADS, n, LAT_PAD), BF16),
    )
    return pl.pallas_call(
        _proj_kernel,
        grid=(n // tm,),
        in_specs=[
            pl.BlockSpec((tm, x.shape[1]), row),
            pl.BlockSpec((tm, LANES), row),
            _const_spec(lw['norm_mix'].shape),
            _const_spec(lw['w_in'].shape),
            _const_spec(lw['gate_bias'].shape),
            _const_spec(lw['mla_q_norm'].shape),
            _const_spec(lw['w_uq'].shape),
            _const_spec(lw['mla_kv_norm'].shape),
            _const_spec(lw['w_uk'].shape),
        ],
        out_specs=(
            pl.BlockSpec((tm, 512), row),
            pl.BlockSpec((tm, 512), row),
            pl.BlockSpec((tm, 512), row),
            pl.BlockSpec((SUBLANES, tm), lambda i: (0, i)),
            pl.BlockSpec((tm, LAT_DIM), row),
            pl.BlockSpec((tm, LAT_PAD), row),
            pl.BlockSpec((A_HEADS, tm, LAT_PAD), lambda i: (0, i, 0)),
        ),
        out_shape=out_shape,
        compiler_params=_params(1),
        name="proj",
    )(x, tab, lw['norm_mix'], lw['w_in'], lw['gate_bias'], lw['mla_q_norm'], lw['w_uq'],
      lw['mla_kv_norm'], lw['w_uk'])


def _seg_scan(x, op, fill, seg):
    n = x.shape[1]
    lane = lax.broadcasted_iota(jnp.int32, x.shape, 1)
    pos = lane if seg is None else lane % seg
    span = n if seg is None else seg
    s = 1
    while s < span:
        x = op(x, jnp.where(pos >= s, pltpu.roll(x, s, 1), fill))
        s *= 2
    return x


def _gate_rows(gates, m_prev, seg):
    ig = pltpu.roll(gates, M_HEADS, 0)
    b = _seg_scan(gates, jnp.add, 0.0, seg)
    d = ig - b
    e = _seg_scan(d, jnp.maximum, -jnp.inf, seg)
    m = b + jnp.maximum(m_prev, e)
    return b, d, ig, m


def _to_cols(row_blocks, n_tok):
    pad = jnp.zeros((LANES - SUBLANES * len(row_blocks), n_tok), F32)
    return jnp.concatenate(list(row_blocks) + [pad], axis=0).T


def _head_out(hh, og, gain):
    hn = hh * lax.rsqrt(jnp.mean(hh * hh, axis=-1, keepdims=True) + EPS) * gain
    return (hn * _sigmoid(og)).astype(BF16)


def _mlstm_prompt_kernel(qk_ref, v_ref, o_ref, g_ref, gain_ref, hm_ref, c_ref, n_ref, m_ref):
    L = qk_ref.shape[0]

    @pl.when(pl.program_id(1) == 0)
    def _():
        c_ref[...] = jnp.zeros_like(c_ref)
        n_ref[...] = jnp.zeros_like(n_ref)
        m_ref[...] = jnp.zeros_like(m_ref)

    m_prev = m_ref[:, 0:1]
    b, d, ig, m = _gate_rows(g_ref[...], m_prev, None)
    b_last = b[:, L - 1:L]
    m_new = m[:, L - 1:L]
    cols = _to_cols((b, m, ig), L)
    qi = lax.broadcasted_iota(jnp.int32, (L, L), 0)
    ki = lax.broadcasted_iota(jnp.int32, (L, L), 1)
    causal = ki <= qi

    for hd in range(M_HEADS):
        q = qk_ref[:, hd * M_DK:(hd + 1) * M_DK]
        k = qk_ref[:, 256 + hd * M_DK:256 + (hd + 1) * M_DK]
        v = v_ref[:, hd * M_DV:(hd + 1) * M_DV]
        b_c = cols[:, hd:hd + 1]
        m_c = cols[:, 8 + hd:9 + hd]
        i_c = cols[:, 16 + hd:17 + hd]
        mp = m_prev[hd:hd + 1, :]
        c_old = c_ref[hd]
        n_old = n_ref[hd:hd + 1, :]

        p = jnp.exp(jnp.where(causal, b_c + d[hd:hd + 1, :] - m_c, -jnp.inf))
        s = _dot_nt(q, k) * p
        a_c = jnp.exp(b_c + mp - m_c)
        num = _dot(s.astype(BF16), v) + a_c * _dot(q, c_old.astype(BF16))
        nq = jnp.sum(s, axis=1, keepdims=True) + a_c * jnp.sum(q.astype(F32) * n_old, axis=1, keepdims=True)
        den = jnp.maximum(jnp.abs(nq), jnp.exp(-m_c))
        sl = slice(hd * M_DV, (hd + 1) * M_DV)
        hm_ref[:, sl] = _head_out(num / den, o_ref[:, sl], gain_ref[hd:hd + 1, :])

        w_c = jnp.exp(b_last[hd:hd + 1, :] - b_c + i_c - m_new[hd:hd + 1, :])
        decay = jnp.exp(b_last[hd:hd + 1, :] + mp - m_new[hd:hd + 1, :])
        kw = k.astype(F32) * w_c
        upd = lax.dot_general(kw.astype(BF16), v, (((0,), (0,)), ((), ())), preferred_element_type=F32)
        c_ref[hd] = decay * c_old + upd
        n_ref[hd:hd + 1, :] = decay * n_old + jnp.sum(kw, axis=0, keepdims=True)

    row = lax.broadcasted_iota(jnp.int32, m_ref.shape, 0)
    m_ref[...] = jnp.where(row < M_HEADS, jnp.broadcast_to(m_new, m_ref.shape), 0.0)


def _mlstm_prompt(qk, v, o, gates, gain, n_seq):
    n = qk.shape[0]
    L = MLSTM_CHUNK
    nc = n // n_seq // L
    tok = lambda b, c: (b * nc + c, 0)
    return pl.pallas_call(
        _mlstm_prompt_kernel,
        grid=(n_seq, nc),
        in_specs=[
            pl.BlockSpec((L, 512), tok),
            pl.BlockSpec((L, 512), tok),
            pl.BlockSpec((L, 512), tok),
            pl.BlockSpec((SUBLANES, L), lambda b, c: (0, b * nc + c)),
            _const_spec(gain.shape),
        ],
        out_specs=(
            pl.BlockSpec((L, 512), tok),
            pl.BlockSpec((None, M_HEADS, M_DK, M_DV), lambda b, c: (b, 0, 0, 0)),
            pl.BlockSpec((None, M_HEADS, M_DK), lambda b, c: (b, 0, 0)),
            pl.BlockSpec((None, SUBLANES, LANES), lambda b, c: (b, 0, 0)),
        ),
        out_shape=(
            jax.ShapeDtypeStruct((n, 512), BF16),
            jax.ShapeDtypeStruct((n_seq, M_HEADS, M_DK, M_DV), F32),
            jax.ShapeDtypeStruct((n_seq, M_HEADS, M_DK), F32),
            jax.ShapeDtypeStruct((n_seq, SUBLANES, LANES), F32),
        ),
        compiler_params=_params(2),
        name="mlstm_prompt",
    )(qk, v, o, gates, gain)


def _exact_onehot_dot(onehot, x):
    hi = x.astype(BF16)
    r1 = x - hi.astype(F32)
    mid = r1.astype(BF16)
    lo = (r1 - mid.astype(F32)).astype(BF16)
    return (_dot(onehot, hi) + _dot(onehot, mid)) + _dot(onehot, lo)


def _mlstm_sample_kernel(qk_ref, v_ref, o_ref, g_ref, mp_ref, gain_ref, c0_ref, n0_ref,
                         hm_ref, c_ref, n_ref, m_ref, cols_ref, *, t_seq):
    L = qk_ref.shape[0]
    G = L // t_seq
    mp_rows = jnp.concatenate([mp_ref[...], jnp.zeros((SUBLANES - M_HEADS, L), F32)], axis=0)
    b, d, ig, m = _gate_rows(g_ref[...], mp_rows, t_seq)
    m_ref[...] = m[0:M_HEADS, :]

    lane = lax.broadcasted_iota(jnp.int32, b.shape, 1)
    pos = lane % t_seq

    def seg_last(x):
        s = 1
        while s < t_seq:
            x = jnp.where(pos >= t_seq - s, x, pltpu.roll(x, L - s, 1))
            s *= 2
        return x

    b_last = seg_last(b)
    m_new = seg_last(m)
    w_rows = jnp.exp(b_last - b + ig - m_new)
    decay_rows = jnp.exp(b_last + mp_rows - m_new)
    cols_ref[...] = _to_cols((b, m, mp_rows, w_rows, decay_rows), L)
    cols = cols_ref[...]
    seq_cols = cols_ref[pl.ds(0, G, stride=t_seq), :]

    qi = lax.broadcasted_iota(jnp.int32, (L, L), 0)
    ki = lax.broadcasted_iota(jnp.int32, (L, L), 1)
    mask = (ki <= qi) & (ki // t_seq == qi // t_seq)
    er = lax.broadcasted_iota(jnp.int32, (L, G * M_DK), 0)
    ec = lax.broadcasted_iota(jnp.int32, (L, G * M_DK), 1)
    own = (ec // M_DK) == (er // t_seq)
    tok_of_seq = (lax.broadcasted_iota(jnp.int32, (L, G), 0) // t_seq
                  == lax.broadcasted_iota(jnp.int32, (L, G), 1)).astype(BF16)
    seq_of_tok = (lax.broadcasted_iota(jnp.int32, (G, L), 1) // t_seq
                  == lax.broadcasted_iota(jnp.int32, (G, L), 0)).astype(BF16)

    def expand(x):
        x2 = jnp.concatenate([x, x], axis=1)
        return jnp.where(own, jnp.tile(x2, (1, G // 2)), 0.0)

    for hd in range(M_HEADS):
        q = qk_ref[:, hd * M_DK:(hd + 1) * M_DK]
        k = qk_ref[:, 256 + hd * M_DK:256 + (hd + 1) * M_DK]
        v = v_ref[:, hd * M_DV:(hd + 1) * M_DV]
        b_c = cols[:, hd:hd + 1]
        m_c = cols[:, 8 + hd:9 + hd]
        mp_c = cols[:, 16 + hd:17 + hd]
        w_c = cols[:, 24 + hd:25 + hd]
        decay_seq = seq_cols[:, 32 + hd:33 + hd]
        n_old = n0_ref[hd]
        c_old = c0_ref[:, hd].reshape(G * M_DK, M_DV)

        p = jnp.exp(jnp.where(mask, b_c + d[hd:hd + 1, :] - m_c, -jnp.inf))
        s = _dot_nt(q, k) * p
        a_c = jnp.exp(b_c + mp_c - m_c)
        qf = q.astype(F32)
        num = _dot(s.astype(BF16), v) + a_c * _dot(expand(qf).astype(BF16), c_old.astype(BF16))
        n_tok = _exact_onehot_dot(tok_of_seq, n_old)
        nq = jnp.sum(s, axis=1, keepdims=True) + a_c * jnp.sum(qf * n_tok, axis=1, keepdims=True)
        den = jnp.maximum(jnp.abs(nq), jnp.exp(-m_c))
        sl = slice(hd * M_DV, (hd + 1) * M_DV)
        hm_ref[:, sl] = _head_out(num / den, o_ref[:, sl], gain_ref[hd:hd + 1, :])

        kw = k.astype(F32) * w_c
        upd = _dot(expand(kw).T.astype(BF16), v)
        for g in range(G):
            c_ref[g, hd] = decay_seq[g:g + 1, :] * c0_ref[g, hd] + upd[g * M_DK:(g + 1) * M_DK, :]
        n_ref[hd] = decay_seq * n_old + _dot(seq_of_tok, kw.astype(BF16))


def _mlstm_sample(qk, v, o, gates, m_prev_rows, gain, c0, n0_t, layer, t_seq):
    n = qk.shape[0]
    G = SAMPLE_GROUP
    L = G * t_seq
    n_seq = n // t_seq
    tok = lambda i: (i, 0)
    return pl.pallas_call(
        functools.partial(_mlstm_sample_kernel, t_seq=t_seq),
        grid=(n // L,),
        in_specs=[
            pl.BlockSpec((L, 512), tok),
            pl.BlockSpec((L, 512), tok),
            pl.BlockSpec((L, 512), tok),
            pl.BlockSpec((SUBLANES, L), lambda i: (0, i)),
            pl.BlockSpec((M_HEADS, L), lambda i: (0, i)),
            _const_spec(gain.shape),
            pl.BlockSpec((None, G, M_HEADS, M_DK, M_DV), lambda i: (layer, i, 0, 0, 0)),
            pl.BlockSpec((M_HEADS, G, M_DK), lambda i: (0, i, 0)),
        ],
        out_specs=(
            pl.BlockSpec((L, 512), tok),
            pl.BlockSpec((G, M_HEADS, M_DK, M_DV), lambda i: (i, 0, 0, 0)),
            pl.BlockSpec((M_HEADS, G, M_DK), lambda i: (0, i, 0)),
            pl.BlockSpec((M_HEADS, L), lambda i: (0, i)),
        ),
        out_shape=(
            jax.ShapeDtypeStruct((n, 512), BF16),
            jax.ShapeDtypeStruct((n_seq, M_HEADS, M_DK, M_DV), F32),
            jax.ShapeDtypeStruct((M_HEADS, n_seq, M_DK), F32),
            jax.ShapeDtypeStruct((M_HEADS, n), F32),
        ),
        scratch_shapes=[pltpu.VMEM((L, LANES), F32)],
        compiler_params=_params(1),
        name="mlstm_sample",
    )(qk, v, o, gates, m_prev_rows, gain, c0, n0_t)


def _attn_prompt_kernel(q_ref, k_ref, o_ref, m_scr, l_scr, acc_scr):
    tq = q_ref.shape[1]
    rows = A_HEADS * tq
    qb = pl.program_id(1)
    q = q_ref[...].reshape(rows, LAT_PAD)
    m_scr[...] = jnp.full_like(m_scr, -jnp.inf)
    l_scr[...] = jnp.zeros_like(l_scr)
    acc_scr[...] = jnp.zeros_like(acc_scr)

    def block(kb, masked):
        kblk = k_ref[pl.ds(pl.multiple_of(kb * tq, tq), tq), :]
        s = _dot_nt(q, kblk)
        if masked:
            qpos = lax.broadcasted_iota(jnp.int32, (rows, tq), 0) % tq
            kpos = lax.broadcasted_iota(jnp.int32, (rows, tq), 1)
            s = jnp.where(kpos <= qpos, s, -jnp.inf)
        m_old = m_scr[...]
        m_new = jnp.maximum(m_old, jnp.max(s, axis=1, keepdims=True))
        alpha = jnp.exp(m_old - m_new)
        p = jnp.exp(s - m_new)
        l_scr[...] = alpha * l_scr[...] + jnp.sum(p, axis=1, keepdims=True)
        acc_scr[...] = alpha * acc_scr[...] + _dot(p.astype(BF16), kblk[:, 0:KV_RANK])
        m_scr[...] = m_new

    def body(kb, carry):
        block(kb, False)
        return carry

    lax.fori_loop(0, qb, body, 0)
    block(qb, True)
    o_ref[...] = (acc_scr[...] / l_scr[...]).reshape(A_HEADS, tq, KV_RANK).astype(BF16)


def _attn_prompt(qatt, katt, n_seq):
    n = katt.shape[0]
    t = n // n_seq
    tq = ATT_BLOCK
    nq = t // tq
    rows = A_HEADS * tq
    return pl.pallas_call(
        _attn_prompt_kernel,
        grid=(n_seq, nq),
        in_specs=[
            pl.BlockSpec((A_HEADS, tq, LAT_PAD), lambda b, i: (0, b * nq + i, 0)),
            pl.BlockSpec((t, LAT_PAD), lambda b, i: (b, 0)),
        ],
        out_specs=pl.BlockSpec((A_HEADS, tq, KV_RANK), lambda b, i: (0, b * nq + i, 0)),
        out_shape=jax.ShapeDtypeStruct((A_HEADS, n, KV_RANK), BF16),
        scratch_shapes=[pltpu.VMEM((rows, 1), F32), pltpu.VMEM((rows, 1), F32),
                        pltpu.VMEM((rows, KV_RANK), F32)],
        compiler_params=_params(2),
        name="attn_prompt",
    )(qatt, katt)


def _attn_sample_kernel(pt_ref, q_ref, knew_ref, *rest, n_pages, t_seq):
    page_refs = rest[:n_pages]
    o_ref = rest[n_pages]
    kbuf = rest[n_pages + 1]
    page = page_refs[0].shape[0]
    for j in range(n_pages):
        kbuf[j * page:(j + 1) * page, :] = page_refs[j][...].astype(BF16)

    q = q_ref[...]
    rows = q.shape[0]
    s_past = _dot_nt(q[:, 0:LAT_DIM], kbuf[...])
    qf = q.astype(F32)
    knew = knew_ref[...].astype(F32)
    tq = lax.broadcasted_iota(jnp.int32, (rows, 1), 0) % t_seq
    s_new = []
    for j in range(t_seq):
        sj = jnp.sum(qf * knew[j:j + 1, :], axis=1, keepdims=True)
        s_new.append(jnp.where(tq >= j, sj, -jnp.inf))
    m = jnp.max(s_past, axis=1, keepdims=True)
    for sj in s_new:
        m = jnp.maximum(m, sj)
    p_past = jnp.exp(s_past - m)
    p_new = [jnp.exp(sj - m) for sj in s_new]
    denom = jnp.sum(p_past, axis=1, keepdims=True)
    for pj in p_new:
        denom = denom + pj
    inv = 1.0 / denom
    out = _dot((p_past * inv).astype(BF16), kbuf[:, 0:KV_RANK])
    for j in range(t_seq):
        pj = (p_new[j] * inv).astype(BF16).astype(F32)
        out = out + pj * knew[j:j + 1, 0:KV_RANK]
    o_ref[...] = out.astype(BF16)


def _attn_sample(q_seq, knew_seq, cache, page_table, layer):
    n_seq, rows, _ = q_seq.shape
    t_seq = knew_seq.shape[1]
    n_pages = page_table.shape[1]
    page = cache.shape[2]

    def page_spec(j):
        return pl.BlockSpec((None, None, page, LAT_DIM), lambda s, pt: (layer, pt[s, j], 0, 0))

    grid_spec = pltpu.PrefetchScalarGridSpec(
        num_scalar_prefetch=1,
        grid=(n_seq,),
        in_specs=[pl.BlockSpec((None, rows, LAT_PAD), lambda s, pt: (s, 0, 0)),
                  pl.BlockSpec((None, t_seq, LAT_PAD), lambda s, pt: (s, 0, 0))]
                 + [page_spec(j) for j in range(n_pages)],
        out_specs=pl.BlockSpec((None, rows, KV_RANK), lambda s, pt: (s, 0, 0)),
        scratch_shapes=[pltpu.VMEM((n_pages * page, LAT_DIM), BF16)],
    )
    return pl.pallas_call(
        functools.partial(_attn_sample_kernel, n_pages=n_pages, t_seq=t_seq),
        grid_spec=grid_spec,
        out_shape=jax.ShapeDtypeStruct((n_seq, rows, KV_RANK), BF16),
        compiler_params=_params(1),
        name="attn_sample",
    )(page_table, q_seq, knew_seq, *([cache] * n_pages))


def _ffn_kernel(*refs, is_sample, is_last, t_seq, tiles_per_seq):
    (x_ref, hm_ref, ol_ref, p_ref, wuv_ref, wout_ref, gffn_ref, wup_ref, wc_ref, cb_ref,
     wdown_ref, gple_ref, wpg_ref, wpp_ref) = refs[:14]
    refs = refs[14:]
    if is_last:
        gfin_ref, refs = refs[0], refs[1:]
    if is_sample:
        s1_ref, s2_ref, x_out, u_out = refs
    else:
        x_out, tail_ref = refs
    tm = x_ref.shape[0]
    d_ff = wdown_ref.shape[0]

    ha = [_dot(ol_ref[hd], wuv_ref[hd]).astype(BF16) for hd in range(A_HEADS)]
    mix = jnp.concatenate([hm_ref[...]] + ha, axis=1)
    x = x_ref[...] + _dot(mix, wout_ref[...])

    hf = _rms(x, gffn_ref[...]).astype(BF16)
    row = lax.broadcasted_iota(jnp.int32, (tm, FF_BLOCK), 0)
    if not is_sample:
        @pl.when(pl.program_id(0) % tiles_per_seq == 0)
        def _():
            tail_ref[...] = jnp.zeros_like(tail_ref)

    def conv(cols):
        u = _dot(hf, wup_ref[:, cols])
        r1 = pltpu.roll(u, 1, 0)
        r2 = pltpu.roll(u, 2, 0)
        if is_sample:
            u_out[:, cols] = u
            pos = row % t_seq
            u1 = jnp.where(pos >= 1, r1, s1_ref[:, cols])
            u2 = jnp.where(pos >= 2, r2, s2_ref[:, cols])
        else:
            prev = tail_ref[:, cols]
            u1 = jnp.where(row >= 1, r1, prev[1:2, :])
            u2 = jnp.where(row >= 2, r2, jnp.where(row == 1, prev[1:2, :], prev[0:1, :]))
            tail_ref[0:2, cols] = u[tm - 2:tm, :]
        return (u2 * wc_ref[0:1, cols] + u1 * wc_ref[1:2, cols]) + u * wc_ref[2:3, cols] + cb_ref[:, cols]

    acc = jnp.zeros((tm, x.shape[1]), F32)
    for j in range(d_ff // FF_BLOCK):
        gate = conv(slice(j * FF_BLOCK, (j + 1) * FF_BLOCK))
        up = conv(slice(d_ff + j * FF_BLOCK, d_ff + (j + 1) * FF_BLOCK))
        act = (gate * _sigmoid(gate) * up).astype(BF16)
        acc = acc + _dot(act, wdown_ref[j * FF_BLOCK:(j + 1) * FF_BLOCK, :])
    x = x + acc

    hp = _rms(x, gple_ref[...]).astype(BF16)
    x = x + _sigmoid(_dot(hp, wpg_ref[...])) * _dot(p_ref[...].astype(BF16), wpp_ref[...])
    x_out[...] = _rms(x, gfin_ref[...]) if is_last else x


def _ffn(x, hm, olat, p_in, lw, *, final_gain, conv_state, n_seq, t_seq):
    n, d = x.shape
    d_up = lw['w_up'].shape[1]
    is_sample = conv_state is not None
    tm = SAMPLE_FFN_TILE if is_sample else TOK_TILE
    is_last = final_gain is not None
    row = lambda i: (i, 0)
    weights = [lw['w_uv'], lw['w_out'], lw['norm_ffn'], lw['w_up'], lw['ffn_conv'], lw['ffn_conv_b'],
               lw['w_down'], lw['ple_norm'], lw['w_pg'], lw['w_pp']]
    if is_last:
        weights.append(final_gain)
    args = [x, hm, olat, p_in] + weights
    in_specs = [pl.BlockSpec((tm, d), row), pl.BlockSpec((tm, 512), row),
                pl.BlockSpec((A_HEADS, tm, KV_RANK), lambda i: (0, i, 0)),
                pl.BlockSpec((tm, p_in.shape[1]), row)] + [_const_spec(w.shape) for w in weights]
    out_shape = [jax.ShapeDtypeStruct((n, d), F32)]
    out_specs = [pl.BlockSpec((tm, d), row)]
    if is_sample:
        args += list(conv_state)
        in_specs += [pl.BlockSpec((tm, d_up), row)] * 2
        out_shape.append(jax.ShapeDtypeStruct((n, d_up), F32))
        out_specs.append(pl.BlockSpec((tm, d_up), row))
        tiles_per_seq = 1
    else:
        tiles_per_seq = n // n_seq // tm
        out_shape.append(jax.ShapeDtypeStruct((n_seq, SUBLANES, d_up), F32))
        out_specs.append(pl.BlockSpec((None, SUBLANES, d_up), lambda i: (i // tiles_per_seq, 0, 0)))
    return pl.pallas_call(
        functools.partial(_ffn_kernel, is_sample=is_sample, is_last=is_last, t_seq=t_seq,
                          tiles_per_seq=tiles_per_seq),
        grid=(n // tm,),
        in_specs=in_specs,
        out_specs=tuple(out_specs),
        out_shape=tuple(out_shape),
        compiler_params=_params(1),
        name="ffn_sample" if is_sample else "ffn_prompt",
    )(*args)


def _swap_halves(w):
    half = w.shape[-1] // 2
    return jnp.concatenate([w[..., half:], w[..., :half]], axis=-1)


def _layer_weights(l, norm_mix, w_in, m_gate_bias, m_norm, mla_q_norm, mla_w_uq, mla_kv_norm, mla_w_uk,
                   mla_w_uv, w_out, norm_ffn, ffn_w_up, ffn_conv, ffn_conv_b, ffn_w_down, ple_norm,
                   ple_w_gate, ple_w_proj):
    d = w_in.shape[1]
    wi = w_in[l]
    sizes = (256, 256, 512, 512, M_HEADS, M_HEADS, Q_RANK, KV_RANK, A_ROPE)
    offs = [0]
    for s in sizes:
        offs.append(offs[-1] + s)
    qm, km, vm, om, im, fm, cq, ckv, kr = (wi[:, offs[i]:offs[i + 1]] for i in range(len(sizes)))
    gate_pad = jnp.zeros((d, LANES - 2 * M_HEADS), wi.dtype)
    w_in_r = jnp.concatenate([qm, km, vm, om, cq, ckv, kr, _swap_halves(kr), fm, im, gate_pad], axis=1)
    uq = mla_w_uq[l].reshape(Q_RANK, A_HEADS, A_NOPE + A_ROPE)
    uq_rope = uq[:, :, A_NOPE:]
    w_uq_r = jnp.concatenate([uq[:, :, :A_NOPE].reshape(Q_RANK, A_HEADS * A_NOPE),
                              jnp.concatenate([uq_rope, _swap_halves(uq_rope)], axis=-1)
                              .reshape(Q_RANK, A_HEADS * LANES)], axis=1)
    row = lambda v: v.reshape(1, -1)
    return {
        'norm_mix': row(norm_mix[l]),
        'w_in': w_in_r.astype(BF16),
        'gate_bias': jnp.concatenate([m_gate_bias[l, 1], m_gate_bias[l, 0]]).reshape(SUBLANES, 1),
        'm_norm': m_norm[l],
        'mla_q_norm': row(mla_q_norm[l]),
        'w_uq': w_uq_r.astype(BF16),
        'mla_kv_norm': row(mla_kv_norm[l]),
        'w_uk': jnp.transpose(mla_w_uk[l], (1, 2, 0)).astype(BF16),
        'w_uv': jnp.transpose(mla_w_uv[l], (1, 0, 2)).astype(BF16),
        'w_out': w_out[l].astype(BF16),
        'norm_ffn': row(norm_ffn[l]),
        'w_up': ffn_w_up[l].astype(BF16),
        'ffn_conv': ffn_conv[l],
        'ffn_conv_b': row(ffn_conv_b[l]),
        'w_down': ffn_w_down[l].astype(BF16),
        'ple_norm': row(ple_norm[l]),
        'w_pg': ple_w_gate[l].astype(BF16),
        'w_pp': ple_w_proj[l].astype(BF16),
    }


def _rope_table(pos):
    half = A_ROPE // 2
    freqs = ROPE_THETA ** (-jnp.arange(half, dtype=F32) * 2.0 / A_ROPE)
    ang = pos.astype(F32)[:, None] * freqs[None, :]
    cos, sin = jnp.cos(ang), jnp.sin(ang)
    return jnp.concatenate([cos, cos, -sin, sin], axis=1)


def kernel(x_prompt, x_sample, p_prompt, p_sample, cache_mla, state_mlstm_C, state_mlstm_n, state_mlstm_m,
           state_ffn_conv, page_table, norm_mix, w_in, m_gate_bias, m_norm, mla_q_norm, mla_w_uq, mla_kv_norm,
           mla_w_uk, mla_w_uv, w_out, norm_ffn, ffn_w_up, ffn_conv, ffn_conv_b, ffn_w_down, ple_norm,
           ple_w_gate, ple_w_proj, final_norm):
    depth = w_in.shape[0]
    bp, tp, d = x_prompt.shape
    bs, ts, _ = x_sample.shape
    n_p, n_s = bp * tp, bs * ts
    past_len = page_table.shape[1] * cache_mla.shape[2]
    d_up = ffn_w_up.shape[2]

    tab_p = jnp.tile(_rope_table(jnp.arange(tp)), (bp, 1))
    tab_s = jnp.tile(_rope_table(past_len + jnp.arange(ts)), (bs, 1))
    xp = x_prompt.reshape(n_p, d)
    xs = x_sample.reshape(n_s, d)
    final_gain = final_norm.reshape(1, d)

    lat_p, lat_s, c_p, n_pl, m_p, c_s, n_sl, m_s, conv_p, conv_s = ([] for _ in range(10))
    for l in range(depth):
        lw = _layer_weights(l, norm_mix, w_in, m_gate_bias, m_norm, mla_q_norm, mla_w_uq, mla_kv_norm,
                            mla_w_uk, mla_w_uv, w_out, norm_ffn, ffn_w_up, ffn_conv, ffn_conv_b,
                            ffn_w_down, ple_norm, ple_w_gate, ple_w_proj)
        fin = final_gain if l == depth - 1 else None

        qk, v, o, gates, lat, katt, qatt = _proj(xp, tab_p, lw)
        hm, c_new, n_new, m_new = _mlstm_prompt(qk, v, o, gates, lw['m_norm'], bp)
        olat = _attn_prompt(qatt, katt, bp)
        xp, tail = _ffn(xp, hm, olat, p_prompt[l].reshape(n_p, -1), lw, final_gain=fin,
                        conv_state=None, n_seq=bp, t_seq=tp)
        lat_p.append(lat.reshape(bp, tp, LAT_DIM))
        c_p.append(c_new)
        n_pl.append(n_new)
        m_p.append(m_new[:, :M_HEADS, 0])
        conv_p.append(tail[:, :CONV_W - 1])

        qk, v, o, gates, lat, katt, qatt = _proj(xs, tab_s, lw)
        m_prev_rows = jnp.repeat(state_mlstm_m[l].T, ts, axis=1)
        hm, c_new, n_new_t, m_rows = _mlstm_sample(qk, v, o, gates, m_prev_rows, lw['m_norm'],
                                                   state_mlstm_C, jnp.transpose(state_mlstm_n[l], (1, 0, 2)),
                                                   l, ts)
        q_seq = jnp.transpose(qatt.reshape(A_HEADS, bs, ts, LAT_PAD), (1, 0, 2, 3)).reshape(bs, A_HEADS * ts, LAT_PAD)
        o_seq = _attn_sample(q_seq, katt.reshape(bs, ts, LAT_PAD), cache_mla, page_table, l)
        olat = jnp.transpose(o_seq.reshape(bs, A_HEADS, ts, KV_RANK), (1, 0, 2, 3)).reshape(A_HEADS, n_s, KV_RANK)
        buf0 = state_ffn_conv[l]
        zero = jnp.zeros((bs, d_up), F32)
        s1 = jnp.stack([buf0[:, 1]] + [zero] * (ts - 1), axis=1).reshape(n_s, d_up)
        s2 = jnp.stack([buf0[:, 0], buf0[:, 1]] + [zero] * (ts - 2), axis=1).reshape(n_s, d_up)
        xs, u_s = _ffn(xs, hm, olat, p_sample[l].reshape(n_s, -1), lw, final_gain=fin,
                       conv_state=(s1, s2), n_seq=bs, t_seq=ts)
        lat_s.append(lat.reshape(bs, ts, LAT_DIM))
        c_s.append(c_new)
        n_sl.append(jnp.transpose(n_new_t, (1, 0, 2)))
        m_s.append(m_rows[:, ts - 1::ts].T)
        conv_s.append(u_s.reshape(bs, ts, d_up)[:, ts - (CONV_W - 1):])

    return (xp.reshape(bp, tp, d), xs.reshape(bs, ts, d), jnp.stack(lat_p), jnp.stack(lat_s),
            jnp.stack(c_p), jnp.stack(n_pl), jnp.stack(m_p), jnp.stack(c_s), jnp.stack(n_sl), jnp.stack(m_s),
            jnp.stack(conv_p), jnp.stack(conv_s))
```

```python
import functools

import jax
import jax.numpy as jnp
from jax import lax
from jax.experimental import pallas as pl
from jax.experimental.pallas import tpu as pltpu

F32 = jnp.float32
BF16 = jnp.bfloat16

M_HEADS = 4
M_DK = 64
M_DV = 128
A_HEADS = 4
A_NOPE = 128
A_ROPE = 64
KV_RANK = 256
Q_RANK = 384
LAT_DIM = KV_RANK + A_ROPE
GATE_CAP = 15.0
ROPE_THETA = 10000.0
EPS = 1e-6
CONV_W = 3
ATT_SCALE = (A_NOPE + A_ROPE) ** -0.5
LOG2_E = 1.4426950408889634
Q_SCALE = ATT_SCALE * LOG2_E

LANES = 128
SUBLANES = 8
VMEM_LIMIT_BYTES = 56 * 1024 * 1024

LAT_PAD = 3 * LANES
TOK_TILE = 256
MLSTM_CHUNK = 256
ATT_BLOCK = 512
SAMPLE_GROUP = 32
FF_BLOCK = 256
SAMPLE_FFN_TILE = 128

C_Q, C_K, C_V, C_O = 0, 256, 512, 1024
C_CQ, C_CKV, C_KR, C_G = 1536, 1920, 2176, 2304
IN_COLS = 2432


def _const_spec(shape):
    nd = len(shape)
    return pl.BlockSpec(shape, lambda *_: (0,) * nd, pipeline_mode=pl.Buffered(1))


def _params(n_axes):
    return pltpu.CompilerParams(dimension_semantics=("arbitrary",) * n_axes,
                                vmem_limit_bytes=VMEM_LIMIT_BYTES)


def _rms(x, g):
    return x * lax.rsqrt(jnp.mean(x * x, axis=-1, keepdims=True) + EPS) * g


def _sigmoid(x):
    return 1.0 / (1.0 + jnp.exp(-x))


def _dot(a, b):
    return jnp.dot(a, b, preferred_element_type=F32)


def _dot_nt(a, b):
    return lax.dot_general(a, b, (((1,), (1,)), ((), ())), preferred_element_type=F32)


def _fold_halves(p):
    return p + pltpu.roll(p, 64, 1)


def _proj_kernel(x_ref, tab_ref, gmix_ref, win_ref, gb_ref, gq_ref, wuq_ref, gkv_ref, wuk_ref,
                 qk_ref, v_ref, o_ref, gates_ref, lat_ref, katt_ref, *q_refs, feature_major):
    if feature_major:
        katt_t_ref, qatt_ref = q_refs
    else:
        qatt_ref, = q_refs
    h = _rms(x_ref[...], gmix_ref[...]).astype(BF16)
    z = _dot(h, win_ref[...])
    qk_ref[:, 0:256] = z[:, C_Q:C_Q + 256].astype(BF16)
    qk_ref[:, 256:512] = (z[:, C_K:C_K + 256] * (M_DK ** -0.5)).astype(BF16)
    v_ref[...] = z[:, C_V:C_V + 512].astype(BF16)
    o_ref[...] = z[:, C_O:C_O + 512]

    gt = z[:, C_G:C_G + LANES].T[0:SUBLANES, :] + gb_ref[...]
    capped = GATE_CAP * jnp.tanh(gt / GATE_CAP)
    log_sig = jnp.minimum(capped, 0.0) - jnp.log1p(jnp.exp(-jnp.abs(capped)))
    row = lax.broadcasted_iota(jnp.int32, gt.shape, 0)
    gates_ref[...] = jnp.where(row < M_HEADS, log_sig, capped)

    tab = tab_ref[...]
    lane = lax.broadcasted_iota(jnp.int32, tab.shape, 1)
    ckv_n = _rms(z[:, C_CKV:C_CKV + KV_RANK], gkv_ref[...])
    rk = _fold_halves(z[:, C_KR:C_KR + LANES] * tab)
    lat_ref[:, 0:KV_RANK] = ckv_n
    lat_ref[:, KV_RANK:LAT_DIM] = rk[:, 0:A_ROPE]
    k_cat = jnp.concatenate([ckv_n, jnp.where(lane < A_ROPE, rk, 0.0)], axis=1)
    katt_ref[...] = k_cat.astype(BF16)
    if feature_major:
        katt_t_ref[...] = k_cat.T.astype(BF16)

    qn = _rms(z[:, C_CQ:C_CQ + Q_RANK], gq_ref[...]).astype(BF16)
    qa = _dot(qn, wuq_ref[...])
    for hd in range(A_HEADS):
        q_lat = _dot(qa[:, hd * A_NOPE:(hd + 1) * A_NOPE].astype(BF16), wuk_ref[hd])
        rq = _fold_halves(qa[:, 512 + hd * LANES:512 + (hd + 1) * LANES] * tab)
        q_cat = jnp.concatenate([q_lat, jnp.where(lane < A_ROPE, rq, 0.0)], axis=1) * Q_SCALE
        qatt_ref[hd] = (q_cat.T if feature_major else q_cat).astype(BF16)


def _proj(x, tab, lw, feature_major):
    n = x.shape[0]
    tm = TOK_TILE
    row = lambda i: (i, 0)
    out_shape = [
        jax.ShapeDtypeStruct((n, 512), BF16),
        jax.ShapeDtypeStruct((n, 512), BF16),
        jax.ShapeDtypeStruct((n, 512), F32),
        jax.ShapeDtypeStruct((SUBLANES, n), F32),
        jax.ShapeDtypeStruct((n, LAT_DIM), F32),
        jax.ShapeDtypeStruct((n, LAT_PAD), BF16),
    ]
    out_specs = [
        pl.BlockSpec((tm, 512), row),
        pl.BlockSpec((tm, 512), row),
        pl.BlockSpec((tm, 512), row),
        pl.BlockSpec((SUBLANES, tm), lambda i: (0, i)),
        pl.BlockSpec((tm, LAT_DIM), row),
        pl.BlockSpec((tm, LAT_PAD), row),
    ]
    if feature_major:
        out_shape += [jax.ShapeDtypeStruct((LAT_PAD, n), BF16),
                      jax.ShapeDtypeStruct((A_HEADS, LAT_PAD, n), BF16)]
        out_specs += [pl.BlockSpec((LAT_PAD, tm), lambda i: (0, i)),
                      pl.BlockSpec((A_HEADS, LAT_PAD, tm), lambda i: (0, 0, i))]
    else:
        out_shape.append(jax.ShapeDtypeStruct((A_HEADS, n, LAT_PAD), BF16))
        out_specs.append(pl.BlockSpec((A_HEADS, tm, LAT_PAD), lambda i: (0, i, 0)))
    return pl.pallas_call(
        functools.partial(_proj_kernel, feature_major=feature_major),
        grid=(n // tm,),
        in_specs=[
            pl.BlockSpec((tm, x.shape[1]), row),
            pl.BlockSpec((tm, LANES), row),
            _const_spec(lw['norm_mix'].shape),
            _const_spec(lw['w_in'].shape),
            _const_spec(lw['gate_bias'].shape),
            _const_spec(lw['mla_q_norm'].shape),
            _const_spec(lw['w_uq'].shape),
            _const_spec(lw['mla_kv_norm'].shape),
            _const_spec(lw['w_uk'].shape),
        ],
        out_specs=tuple(out_specs),
        out_shape=tuple(out_shape),
        compiler_params=_params(1),
        name="proj_prompt" if feature_major else "proj_sample",
    )(x, tab, lw['norm_mix'], lw['w_in'], lw['gate_bias'], lw['mla_q_norm'], lw['w_uq'],
      lw['mla_kv_norm'], lw['w_uk'])


def _seg_scan(x, op, fill, seg):
    n = x.shape[1]
    lane = lax.broadcasted_iota(jnp.int32, x.shape, 1)
    pos = lane if seg is None else lane % seg
    span = n if seg is None else seg
    s = 1
    while s < span:
        x = op(x, jnp.where(pos >= s, pltpu.roll(x, s, 1), fill))
        s *= 2
    return x


def _gate_rows(gates, m_prev, seg):
    ig = pltpu.roll(gates, M_HEADS, 0)
    b = _seg_scan(gates, jnp.add, 0.0, seg)
    d = ig - b
    e = _seg_scan(d, jnp.maximum, -jnp.inf, seg)
    m = b + jnp.maximum(m_prev, e)
    return b, d, ig, m


def _to_cols(row_blocks, n_tok):
    pad = jnp.zeros((LANES - SUBLANES * len(row_blocks), n_tok), F32)
    return jnp.concatenate(list(row_blocks) + [pad], axis=0).T


def _head_out(hh, og, gain):
    hn = hh * lax.rsqrt(jnp.mean(hh * hh, axis=-1, keepdims=True) + EPS) * gain
    return (hn * _sigmoid(og)).astype(BF16)


def _mlstm_prompt_kernel(qk_ref, v_ref, o_ref, g_ref, gain_ref, hm_ref, c_ref, n_ref, m_ref):
    L = qk_ref.shape[0]

    @pl.when(pl.program_id(1) == 0)
    def _():
        c_ref[...] = jnp.zeros_like(c_ref)
        n_ref[...] = jnp.zeros_like(n_ref)
        m_ref[...] = jnp.zeros_like(m_ref)

    m_prev = m_ref[:, 0:1]
    b, d, ig, m = _gate_rows(g_ref[...], m_prev, None)
    b_last = b[:, L - 1:L]
    m_new = m[:, L - 1:L]
    cols = _to_cols((b, m, ig), L)
    qi = lax.broadcasted_iota(jnp.int32, (L, L), 0)
    ki = lax.broadcasted_iota(jnp.int32, (L, L), 1)
    causal = ki <= qi

    for hd in range(M_HEADS):
        q = qk_ref[:, hd * M_DK:(hd + 1) * M_DK]
        k = qk_ref[:, 256 + hd * M_DK:256 + (hd + 1) * M_DK]
        v = v_ref[:, hd * M_DV:(hd + 1) * M_DV]
        b_c = cols[:, hd:hd + 1]
        m_c = cols[:, 8 + hd:9 + hd]
        i_c = cols[:, 16 + hd:17 + hd]
        mp = m_prev[hd:hd + 1, :]
        c_old = c_ref[hd]
        n_old = n_ref[hd:hd + 1, :]

        p = jnp.exp(jnp.where(causal, b_c + d[hd:hd + 1, :] - m_c, -jnp.inf))
        s = _dot_nt(q, k) * p
        a_c = jnp.exp(b_c + mp - m_c)
        num = _dot(s.astype(BF16), v) + a_c * _dot(q, c_old.astype(BF16))
        nq = jnp.sum(s, axis=1, keepdims=True) + a_c * jnp.sum(q.astype(F32) * n_old, axis=1, keepdims=True)
        den = jnp.maximum(jnp.abs(nq), jnp.exp(-m_c))
        sl = slice(hd * M_DV, (hd + 1) * M_DV)
        hm_ref[:, sl] = _head_out(num / den, o_ref[:, sl], gain_ref[hd:hd + 1, :])

        w_c = jnp.exp(b_last[hd:hd + 1, :] - b_c + i_c - m_new[hd:hd + 1, :])
        decay = jnp.exp(b_last[hd:hd + 1, :] + mp - m_new[hd:hd + 1, :])
        kw = k.astype(F32) * w_c
        upd = lax.dot_general(kw.astype(BF16), v, (((0,), (0,)), ((), ())), preferred_element_type=F32)
        c_ref[hd] = decay * c_old + upd
        n_ref[hd:hd + 1, :] = decay * n_old + jnp.sum(kw, axis=0, keepdims=True)

    row = lax.broadcasted_iota(jnp.int32, m_ref.shape, 0)
    m_ref[...] = jnp.where(row < M_HEADS, jnp.broadcast_to(m_new, m_ref.shape), 0.0)


def _mlstm_prompt(qk, v, o, gates, gain, n_seq):
    n = qk.shape[0]
    L = MLSTM_CHUNK
    nc = n // n_seq // L
    tok = lambda b, c: (b * nc + c, 0)
    return pl.pallas_call(
        _mlstm_prompt_kernel,
        grid=(n_seq, nc),
        in_specs=[
            pl.BlockSpec((L, 512), tok),
            pl.BlockSpec((L, 512), tok),
            pl.BlockSpec((L, 512), tok),
            pl.BlockSpec((SUBLANES, L), lambda b, c: (0, b * nc + c)),
            _const_spec(gain.shape),
        ],
        out_specs=(
            pl.BlockSpec((L, 512), tok),
            pl.BlockSpec((None, M_HEADS, M_DK, M_DV), lambda b, c: (b, 0, 0, 0)),
            pl.BlockSpec((None, M_HEADS, M_DK), lambda b, c: (b, 0, 0)),
            pl.BlockSpec((None, SUBLANES, LANES), lambda b, c: (b, 0, 0)),
        ),
        out_shape=(
            jax.ShapeDtypeStruct((n, 512), BF16),
            jax.ShapeDtypeStruct((n_seq, M_HEADS, M_DK, M_DV), F32),
            jax.ShapeDtypeStruct((n_seq, M_HEADS, M_DK), F32),
            jax.ShapeDtypeStruct((n_seq, SUBLANES, LANES), F32),
        ),
        compiler_params=_params(2),
        name="mlstm_prompt",
    )(qk, v, o, gates, gain)


def _exact_onehot_dot(onehot, x):
    hi = x.astype(BF16)
    r1 = x - hi.astype(F32)
    mid = r1.astype(BF16)
    lo = (r1 - mid.astype(F32)).astype(BF16)
    return (_dot(onehot, hi) + _dot(onehot, mid)) + _dot(onehot, lo)


def _mlstm_sample_kernel(qk_ref, v_ref, o_ref, g_ref, mp_ref, gain_ref, c0_ref, n0_ref,
                         hm_ref, c_ref, n_ref, m_ref, cols_ref, *, t_seq):
    L = qk_ref.shape[0]
    G = L // t_seq
    mp_rows = jnp.concatenate([mp_ref[...], jnp.zeros((SUBLANES - M_HEADS, L), F32)], axis=0)
    b, d, ig, m = _gate_rows(g_ref[...], mp_rows, t_seq)
    m_ref[...] = m[0:M_HEADS, :]

    lane = lax.broadcasted_iota(jnp.int32, b.shape, 1)
    pos = lane % t_seq

    def seg_last(x):
        s = 1
        while s < t_seq:
            x = jnp.where(pos >= t_seq - s, x, pltpu.roll(x, L - s, 1))
            s *= 2
        return x

    b_last = seg_last(b)
    m_new = seg_last(m)
    w_rows = jnp.exp(b_last - b + ig - m_new)
    decay_rows = jnp.exp(b_last + mp_rows - m_new)
    cols_ref[...] = _to_cols((b, m, mp_rows, w_rows, decay_rows), L)
    cols = cols_ref[...]
    seq_cols = cols_ref[pl.ds(0, G, stride=t_seq), :]

    qi = lax.broadcasted_iota(jnp.int32, (L, L), 0)
    ki = lax.broadcasted_iota(jnp.int32, (L, L), 1)
    mask = (ki <= qi) & (ki // t_seq == qi // t_seq)
    er = lax.broadcasted_iota(jnp.int32, (L, G * M_DK), 0)
    ec = lax.broadcasted_iota(jnp.int32, (L, G * M_DK), 1)
    own = (ec // M_DK) == (er // t_seq)
    tok_of_seq = (lax.broadcasted_iota(jnp.int32, (L, G), 0) // t_seq
                  == lax.broadcasted_iota(jnp.int32, (L, G), 1)).astype(BF16)
    seq_of_tok = (lax.broadcasted_iota(jnp.int32, (G, L), 1) // t_seq
                  == lax.broadcasted_iota(jnp.int32, (G, L), 0)).astype(BF16)

    def expand(x):
        x2 = jnp.concatenate([x, x], axis=1)
        return jnp.where(own, jnp.tile(x2, (1, G // 2)), 0.0)

    for hd in range(M_HEADS):
        q = qk_ref[:, hd * M_DK:(hd + 1) * M_DK]
        k = qk_ref[:, 256 + hd * M_DK:256 + (hd + 1) * M_DK]
        v = v_ref[:, hd * M_DV:(hd + 1) * M_DV]
        b_c = cols[:, hd:hd + 1]
        m_c = cols[:, 8 + hd:9 + hd]
        mp_c = cols[:, 16 + hd:17 + hd]
        w_c = cols[:, 24 + hd:25 + hd]
        decay_seq = seq_cols[:, 32 + hd:33 + hd]
        n_old = n0_ref[hd]
        c_old = c0_ref[:, hd].reshape(G * M_DK, M_DV)

        p = jnp.exp(jnp.where(mask, b_c + d[hd:hd + 1, :] - m_c, -jnp.inf))
        s = _dot_nt(q, k) * p
        a_c = jnp.exp(b_c + mp_c - m_c)
        qf = q.astype(F32)
        num = _dot(s.astype(BF16), v) + a_c * _dot(expand(qf).astype(BF16), c_old.astype(BF16))
        n_tok = _exact_onehot_dot(tok_of_seq, n_old)
        nq = jnp.sum(s, axis=1, keepdims=True) + a_c * jnp.sum(qf * n_tok, axis=1, keepdims=True)
        den = jnp.maximum(jnp.abs(nq), jnp.exp(-m_c))
        sl = slice(hd * M_DV, (hd + 1) * M_DV)
        hm_ref[:, sl] = _head_out(num / den, o_ref[:, sl], gain_ref[hd:hd + 1, :])

        kw = k.astype(F32) * w_c
        upd = _dot(expand(kw).T.astype(BF16), v)
        for g in range(G):
            c_ref[g, hd] = decay_seq[g:g + 1, :] * c0_ref[g, hd] + upd[g * M_DK:(g + 1) * M_DK, :]
        n_ref[hd] = decay_seq * n_old + _dot(seq_of_tok, kw.astype(BF16))


def _mlstm_sample(qk, v, o, gates, m_prev_rows, gain, c0, n0_t, layer, t_seq):
    n = qk.shape[0]
    G = SAMPLE_GROUP
    L = G * t_seq
    n_seq = n // t_seq
    tok = lambda i: (i, 0)
    return pl.pallas_call(
        functools.partial(_mlstm_sample_kernel, t_seq=t_seq),
        grid=(n // L,),
        in_specs=[
            pl.BlockSpec((L, 512), tok),
            pl.BlockSpec((L, 512), tok),
            pl.BlockSpec((L, 512), tok),
            pl.BlockSpec((SUBLANES, L), lambda i: (0, i)),
            pl.BlockSpec((M_HEADS, L), lambda i: (0, i)),
            _const_spec(gain.shape),
            pl.BlockSpec((None, G, M_HEADS, M_DK, M_DV), lambda i: (layer, i, 0, 0, 0)),
            pl.BlockSpec((M_HEADS, G, M_DK), lambda i: (0, i, 0)),
        ],
        out_specs=(
            pl.BlockSpec((L, 512), tok),
            pl.BlockSpec((G, M_HEADS, M_DK, M_DV), lambda i: (i, 0, 0, 0)),
            pl.BlockSpec((M_HEADS, G, M_DK), lambda i: (0, i, 0)),
            pl.BlockSpec((M_HEADS, L), lambda i: (0, i)),
        ),
        out_shape=(
            jax.ShapeDtypeStruct((n, 512), BF16),
            jax.ShapeDtypeStruct((n_seq, M_HEADS, M_DK, M_DV), F32),
            jax.ShapeDtypeStruct((M_HEADS, n_seq, M_DK), F32),
            jax.ShapeDtypeStruct((M_HEADS, n), F32),
        ),
        scratch_shapes=[pltpu.VMEM((L, LANES), F32)],
        compiler_params=_params(1),
        name="mlstm_sample",
    )(qk, v, o, gates, m_prev_rows, gain, c0, n0_t)


def _attn_prompt_kernel(qt_ref, k_ref, kt_ref, o_ref, m_scr, l_scr, acc_scr):
    tq = qt_ref.shape[2]
    qb = pl.program_id(1)
    m_scr[...] = jnp.full_like(m_scr, -jnp.inf)
    l_scr[...] = jnp.zeros_like(l_scr)
    acc_scr[...] = jnp.zeros_like(acc_scr)

    def block(kb, masked):
        start = pl.multiple_of(kb * tq, tq)
        kblk = k_ref[pl.ds(start, tq), :]
        vt = kt_ref[0:KV_RANK, pl.ds(start, tq)]
        for hd in range(A_HEADS):
            st = _dot(kblk, qt_ref[hd])
            if masked:
                kpos = lax.broadcasted_iota(jnp.int32, st.shape, 0)
                qpos = lax.broadcasted_iota(jnp.int32, st.shape, 1)
                st = jnp.where(kpos <= qpos, st, -jnp.inf)
            m_old = m_scr[hd:hd + 1, :]
            m_new = jnp.maximum(m_old, jnp.max(st, axis=0, keepdims=True))
            alpha = jnp.exp2(m_old - m_new)
            p = jnp.exp2(st - m_new)
            l_scr[hd:hd + 1, :] = alpha * l_scr[hd:hd + 1, :] + jnp.sum(p, axis=0, keepdims=True)
            acc_scr[hd] = alpha * acc_scr[hd] + _dot(vt, p.astype(BF16))
            m_scr[hd:hd + 1, :] = m_new

    def body(kb, carry):
        block(kb, False)
        return carry

    lax.fori_loop(0, qb, body, 0)
    block(qb, True)
    for hd in range(A_HEADS):
        o_ref[hd] = (acc_scr[hd] / l_scr[hd:hd + 1, :]).T.astype(BF16)


def _attn_prompt(qatt_t, katt, katt_t, n_seq):
    n = katt.shape[0]
    t = n // n_seq
    tq = ATT_BLOCK
    nq = t // tq
    return pl.pallas_call(
        _attn_prompt_kernel,
        grid=(n_seq, nq),
        in_specs=[
            pl.BlockSpec((A_HEADS, LAT_PAD, tq), lambda b, i: (0, 0, b * nq + i)),
            pl.BlockSpec((t, LAT_PAD), lambda b, i: (b, 0)),
            pl.BlockSpec((LAT_PAD, t), lambda b, i: (0, b)),
        ],
        out_specs=pl.BlockSpec((A_HEADS, tq, KV_RANK), lambda b, i: (0, b * nq + i, 0)),
        out_shape=jax.ShapeDtypeStruct((A_HEADS, n, KV_RANK), BF16),
        scratch_shapes=[pltpu.VMEM((SUBLANES, tq), F32), pltpu.VMEM((SUBLANES, tq), F32),
                        pltpu.VMEM((A_HEADS, KV_RANK, tq), F32)],
        compiler_params=_params(2),
        name="attn_prompt",
    )(qatt_t, katt, katt_t)


def _attn_sample_kernel(pt_ref, q_ref, knew_ref, *rest, n_pages, t_seq):
    page_refs = rest[:n_pages]
    o_ref = rest[n_pages]
    kbuf = rest[n_pages + 1]
    page = page_refs[0].shape[1]
    for j in range(n_pages):
        kbuf[:, j * page:(j + 1) * page] = page_refs[j][...].astype(BF16)

    q = q_ref[...]
    rows = q.shape[0]
    s_past = _dot(q[:, 0:LAT_DIM], kbuf[...])
    qf = q.astype(F32)
    knew = knew_ref[...].astype(F32)
    tq = lax.broadcasted_iota(jnp.int32, (rows, 1), 0) % t_seq
    s_new = []
    for j in range(t_seq):
        sj = jnp.sum(qf * knew[j:j + 1, :], axis=1, keepdims=True)
        s_new.append(jnp.where(tq >= j, sj, -jnp.inf))
    m = jnp.max(s_past, axis=1, keepdims=True)
    for sj in s_new:
        m = jnp.maximum(m, sj)
    p_past = jnp.exp2(s_past - m)
    p_new = [jnp.exp2(sj - m) for sj in s_new]
    denom = jnp.sum(p_past, axis=1, keepdims=True)
    for pj in p_new:
        denom = denom + pj
    inv = 1.0 / denom
    out = _dot_nt((p_past * inv).astype(BF16), kbuf[0:KV_RANK, :])
    for j in range(t_seq):
        pj = (p_new[j] * inv).astype(BF16).astype(F32)
        out = out + pj * knew[j:j + 1, 0:KV_RANK]
    o_ref[...] = out.astype(BF16)


def _attn_sample(q_seq, knew_seq, cache_t, page_table, layer):
    n_seq, rows, _ = q_seq.shape
    t_seq = knew_seq.shape[1]
    n_pages = page_table.shape[1]
    page = cache_t.shape[3]

    def page_spec(j):
        return pl.BlockSpec((None, None, LAT_DIM, page), lambda s, pt: (layer, pt[s, j], 0, 0))

    grid_spec = pltpu.PrefetchScalarGridSpec(
        num_scalar_prefetch=1,
        grid=(n_seq,),
        in_specs=[pl.BlockSpec((None, rows, LAT_PAD), lambda s, pt: (s, 0, 0)),
                  pl.BlockSpec((None, t_seq, LAT_PAD), lambda s, pt: (s, 0, 0))]
                 + [page_spec(j) for j in range(n_pages)],
        out_specs=pl.BlockSpec((None, rows, KV_RANK), lambda s, pt: (s, 0, 0)),
        scratch_shapes=[pltpu.VMEM((LAT_DIM, n_pages * page), BF16)],
    )
    return pl.pallas_call(
        functools.partial(_attn_sample_kernel, n_pages=n_pages, t_seq=t_seq),
        grid_spec=grid_spec,
        out_shape=jax.ShapeDtypeStruct((n_seq, rows, KV_RANK), BF16),
        compiler_params=_params(1),
        name="attn_sample",
    )(page_table, q_seq, knew_seq, *([cache_t] * n_pages))


def _ffn_kernel(*refs, is_sample, is_last, t_seq, tiles_per_seq):
    (x_ref, hm_ref, ol_ref, p_ref, wuv_ref, wout_ref, gffn_ref, wup_ref, wc_ref, cb_ref,
     wdown_ref, gple_ref, wpg_ref, wpp_ref) = refs[:14]
    refs = refs[14:]
    if is_last:
        gfin_ref, refs = refs[0], refs[1:]
    if is_sample:
        s1_ref, s2_ref, x_out, u_out, ubuf = refs
    else:
        x_out, tail_ref, ubuf = refs
    tm = x_ref.shape[0]
    d_ff = wdown_ref.shape[0]
    hdr = SUBLANES

    ha = [_dot(ol_ref[hd], wuv_ref[hd]).astype(BF16) for hd in range(A_HEADS)]
    mix = jnp.concatenate([hm_ref[...]] + ha, axis=1)
    x = x_ref[...] + _dot(mix, wout_ref[...])

    hf = _rms(x, gffn_ref[...]).astype(BF16)
    row = lax.broadcasted_iota(jnp.int32, (tm, FF_BLOCK), 0)
    if is_sample:
        ubuf[0:hdr, :] = jnp.zeros((hdr, FF_BLOCK), F32)
    else:
        @pl.when(pl.program_id(0) % tiles_per_seq == 0)
        def _():
            tail_ref[...] = jnp.zeros_like(tail_ref)

    def conv(cols):
        u = _dot(hf, wup_ref[:, cols])
        if not is_sample:
            ubuf[0:hdr, :] = tail_ref[:, cols]
            tail_ref[:, cols] = u[tm - hdr:tm, :]
        ubuf[hdr:hdr + tm, :] = u
        u1 = ubuf[hdr - 1:hdr - 1 + tm, :]
        u2 = ubuf[hdr - 2:hdr - 2 + tm, :]
        if is_sample:
            u_out[:, cols] = u
            pos = row % t_seq
            u1 = jnp.where(pos >= 1, u1, s1_ref[:, cols])
            u2 = jnp.where(pos >= 2, u2, s2_ref[:, cols])
        return (u2 * wc_ref[0:1, cols] + u1 * wc_ref[1:2, cols]) + u * wc_ref[2:3, cols] + cb_ref[:, cols]

    acc = jnp.zeros((tm, x.shape[1]), F32)
    for j in range(d_ff // FF_BLOCK):
        gate = conv(slice(j * FF_BLOCK, (j + 1) * FF_BLOCK))
        up = conv(slice(d_ff + j * FF_BLOCK, d_ff + (j + 1) * FF_BLOCK))
        act = (gate * _sigmoid(gate) * up).astype(BF16)
        acc = acc + _dot(act, wdown_ref[j * FF_BLOCK:(j + 1) * FF_BLOCK, :])
    x = x + acc

    hp = _rms(x, gple_ref[...]).astype(BF16)
    x = x + _sigmoid(_dot(hp, wpg_ref[...])) * _dot(p_ref[...].astype(BF16), wpp_ref[...])
    x_out[...] = _rms(x, gfin_ref[...]) if is_last else x


def _ffn(x, hm, olat, p_in, lw, *, final_gain, conv_state, n_seq, t_seq):
    n, d = x.shape
    d_up = lw['w_up'].shape[1]
    is_sample = conv_state is not None
    tm = SAMPLE_FFN_TILE if is_sample else TOK_TILE
    is_last = final_gain is not None
    row = lambda i: (i, 0)
    weights = [lw['w_uv'], lw['w_out'], lw['norm_ffn'], lw['w_up'], lw['ffn_conv'], lw['ffn_conv_b'],
               lw['w_down'], lw['ple_norm'], lw['w_pg'], lw['w_pp']]
    if is_last:
        weights.append(final_gain)
    args = [x, hm, olat, p_in] + weights
    in_specs = [pl.BlockSpec((tm, d), row), pl.BlockSpec((tm, 512), row),
                pl.BlockSpec((A_HEADS, tm, KV_RANK), lambda i: (0, i, 0)),
                pl.BlockSpec((tm, p_in.shape[1]), row)] + [_const_spec(w.shape) for w in weights]
    out_shape = [jax.ShapeDtypeStruct((n, d), F32)]
    out_specs = [pl.BlockSpec((tm, d), row)]
    if is_sample:
        args += list(conv_state)
        in_specs += [pl.BlockSpec((tm, d_up), row)] * 2
        out_shape.append(jax.ShapeDtypeStruct((n, d_up), F32))
        out_specs.append(pl.BlockSpec((tm, d_up), row))
        tiles_per_seq = 1
    else:
        tiles_per_seq = n // n_seq // tm
        out_shape.append(jax.ShapeDtypeStruct((n_seq, SUBLANES, d_up), F32))
        out_specs.append(pl.BlockSpec((None, SUBLANES, d_up), lambda i: (i // tiles_per_seq, 0, 0)))
    return pl.pallas_call(
        functools.partial(_ffn_kernel, is_sample=is_sample, is_last=is_last, t_seq=t_seq,
                          tiles_per_seq=tiles_per_seq),
        grid=(n // tm,),
        in_specs=in_specs,
        out_specs=tuple(out_specs),
        out_shape=tuple(out_shape),
        scratch_shapes=[pltpu.VMEM((SUBLANES + tm, FF_BLOCK), F32)],
        compiler_params=_params(1),
        name="ffn_sample" if is_sample else "ffn_prompt",
    )(*args)


def _swap_halves(w):
    half = w.shape[-1] // 2
    return jnp.concatenate([w[..., half:], w[..., :half]], axis=-1)


def _layer_weights(l, norm_mix, w_in, m_gate_bias, m_norm, mla_q_norm, mla_w_uq, mla_kv_norm, mla_w_uk,
                   mla_w_uv, w_out, norm_ffn, ffn_w_up, ffn_conv, ffn_conv_b, ffn_w_down, ple_norm,
                   ple_w_gate, ple_w_proj):
    d = w_in.shape[1]
    wi = w_in[l]
    sizes = (256, 256, 512, 512, M_HEADS, M_HEADS, Q_RANK, KV_RANK, A_ROPE)
    offs = [0]
    for s in sizes:
        offs.append(offs[-1] + s)
    qm, km, vm, om, im, fm, cq, ckv, kr = (wi[:, offs[i]:offs[i + 1]] for i in range(len(sizes)))
    gate_pad = jnp.zeros((d, LANES - 2 * M_HEADS), wi.dtype)
    w_in_r = jnp.concatenate([qm, km, vm, om, cq, ckv, kr, _swap_halves(kr), fm, im, gate_pad], axis=1)
    uq = mla_w_uq[l].reshape(Q_RANK, A_HEADS, A_NOPE + A_ROPE)
    uq_rope = uq[:, :, A_NOPE:]
    w_uq_r = jnp.concatenate([uq[:, :, :A_NOPE].reshape(Q_RANK, A_HEADS * A_NOPE),
                              jnp.concatenate([uq_rope, _swap_halves(uq_rope)], axis=-1)
                              .reshape(Q_RANK, A_HEADS * LANES)], axis=1)
    row = lambda v: v.reshape(1, -1)
    return {
        'norm_mix': row(norm_mix[l]),
        'w_in': w_in_r.astype(BF16),
        'gate_bias': jnp.concatenate([m_gate_bias[l, 1], m_gate_bias[l, 0]]).reshape(SUBLANES, 1),
        'm_norm': m_norm[l],
        'mla_q_norm': row(mla_q_norm[l]),
        'w_uq': w_uq_r.astype(BF16),
        'mla_kv_norm': row(mla_kv_norm[l]),
        'w_uk': jnp.transpose(mla_w_uk[l], (1, 2, 0)).astype(BF16),
        'w_uv': jnp.transpose(mla_w_uv[l], (1, 0, 2)).astype(BF16),
        'w_out': w_out[l].astype(BF16),
        'norm_ffn': row(norm_ffn[l]),
        'w_up': ffn_w_up[l].astype(BF16),
        'ffn_conv': ffn_conv[l],
        'ffn_conv_b': row(ffn_conv_b[l]),
        'w_down': ffn_w_down[l].astype(BF16),
        'ple_norm': row(ple_norm[l]),
        'w_pg': ple_w_gate[l].astype(BF16),
        'w_pp': ple_w_proj[l].astype(BF16),
    }


def _rope_table(pos):
    half = A_ROPE // 2
    freqs = ROPE_THETA ** (-jnp.arange(half, dtype=F32) * 2.0 / A_ROPE)
    ang = pos.astype(F32)[:, None] * freqs[None, :]
    cos, sin = jnp.cos(ang), jnp.sin(ang)
    return jnp.concatenate([cos, cos, -sin, sin], axis=1)


def kernel(x_prompt, x_sample, p_prompt, p_sample, cache_mla, state_mlstm_C, state_mlstm_n, state_mlstm_m,
           state_ffn_conv, page_table, norm_mix, w_in, m_gate_bias, m_norm, mla_q_norm, mla_w_uq, mla_kv_norm,
           mla_w_uk, mla_w_uv, w_out, norm_ffn, ffn_w_up, ffn_conv, ffn_conv_b, ffn_w_down, ple_norm,
           ple_w_gate, ple_w_proj, final_norm):
    depth = w_in.shape[0]
    bp, tp, d = x_prompt.shape
    bs, ts, _ = x_sample.shape
    n_p, n_s = bp * tp, bs * ts
    past_len = page_table.shape[1] * cache_mla.shape[2]
    d_up = ffn_w_up.shape[2]

    tab_p = jnp.tile(_rope_table(jnp.arange(tp)), (bp, 1))
    tab_s = jnp.tile(_rope_table(past_len + jnp.arange(ts)), (bs, 1))
    xp = x_prompt.reshape(n_p, d)
    xs = x_sample.reshape(n_s, d)
    final_gain = final_norm.reshape(1, d)
    cache_t = jnp.swapaxes(cache_mla, 2, 3)

    lat_p, lat_s, c_p, n_pl, m_p, c_s, n_sl, m_s, conv_p, conv_s = ([] for _ in range(10))
    for l in range(depth):
        lw = _layer_weights(l, norm_mix, w_in, m_gate_bias, m_norm, mla_q_norm, mla_w_uq, mla_kv_norm,
                            mla_w_uk, mla_w_uv, w_out, norm_ffn, ffn_w_up, ffn_conv, ffn_conv_b,
                            ffn_w_down, ple_norm, ple_w_gate, ple_w_proj)
        fin = final_gain if l == depth - 1 else None

        qk, v, o, gates, lat, katt, katt_t, qatt_t = _proj(xp, tab_p, lw, True)
        hm, c_new, n_new, m_new = _mlstm_prompt(qk, v, o, gates, lw['m_norm'], bp)
        olat = _attn_prompt(qatt_t, katt, katt_t, bp)
        xp, tail = _ffn(xp, hm, olat, p_prompt[l].reshape(n_p, -1), lw, final_gain=fin,
                        conv_state=None, n_seq=bp, t_seq=tp)
        lat_p.append(lat.reshape(bp, tp, LAT_DIM))
        c_p.append(c_new)
        n_pl.append(n_new)
        m_p.append(m_new[:, :M_HEADS, 0])
        conv_p.append(tail[:, SUBLANES - (CONV_W - 1):])

        qk, v, o, gates, lat, katt, qatt = _proj(xs, tab_s, lw, False)
        m_prev_rows = jnp.repeat(state_mlstm_m[l].T, ts, axis=1)
        hm, c_new, n_new_t, m_rows = _mlstm_sample(qk, v, o, gates, m_prev_rows, lw['m_norm'],
                                                   state_mlstm_C, jnp.transpose(state_mlstm_n[l], (1, 0, 2)),
                                                   l, ts)
        q_seq = jnp.transpose(qatt.reshape(A_HEADS, bs, ts, LAT_PAD), (1, 0, 2, 3)).reshape(bs, A_HEADS * ts, LAT_PAD)
        o_seq = _attn_sample(q_seq, katt.reshape(bs, ts, LAT_PAD), cache_t, page_table, l)
        olat = jnp.transpose(o_seq.reshape(bs, A_HEADS, ts, KV_RANK), (1, 0, 2, 3)).reshape(A_HEADS, n_s, KV_RANK)
        buf0 = state_ffn_conv[l]
        zero = jnp.zeros((bs, d_up), F32)
        s1 = jnp.stack([buf0[:, 1]] + [zero] * (ts - 1), axis=1).reshape(n_s, d_up)
        s2 = jnp.stack([buf0[:, 0], buf0[:, 1]] + [zero] * (ts - 2), axis=1).reshape(n_s, d_up)
        xs, u_s = _ffn(xs, hm, olat, p_sample[l].reshape(n_s, -1), lw, final_gain=fin,
                       conv_state=(s1, s2), n_seq=bs, t_seq=ts)
        lat_s.append(lat.reshape(bs, ts, LAT_DIM))
        c_s.append(c_new)
        n_sl.append(jnp.transpose(n_new_t, (1, 0, 2)))
        m_s.append(m_rows[:, ts - 1::ts].T)
        conv_s.append(u_s.reshape(bs, ts, d_up)[:, ts - (CONV_W - 1):])

    return (xp.reshape(bp, tp, d), xs.reshape(bs, ts, d), jnp.stack(lat_p), jnp.stack(lat_s),
            jnp.stack(c_p), jnp.stack(n_pl), jnp.stack(m_p), jnp.stack(c_s), jnp.stack(n_sl), jnp.stack(m_s),
            jnp.stack(conv_p), jnp.stack(conv_s))
```

```python
import functools

import jax
import jax.numpy as jnp
from jax import lax
from jax.experimental import pallas as pl
from jax.experimental.pallas import tpu as pltpu

F32 = jnp.float32
BF16 = jnp.bfloat16

M_HEADS = 4
M_DK = 64
M_DV = 128
A_HEADS = 4
A_NOPE = 128
A_ROPE = 64
KV_RANK = 256
Q_RANK = 384
LAT_DIM = KV_RANK + A_ROPE
GATE_CAP = 15.0
ROPE_THETA = 10000.0
EPS = 1e-6
CONV_W = 3
ATT_SCALE = (A_NOPE + A_ROPE) ** -0.5
LOG2_E = 1.4426950408889634
Q_SCALE = ATT_SCALE * LOG2_E

LANES = 128
SUBLANES = 8
VMEM_LIMIT_BYTES = 56 * 1024 * 1024

LAT_PAD = 3 * LANES
TOK_TILE = 256
MLSTM_CHUNK = 256
ATT_BLOCK = 512
SAMPLE_GROUP = 32
FF_BLOCK = 256
SAMPLE_FFN_TILE = 128

C_Q, C_K, C_V, C_O = 0, 256, 512, 1024
C_CQ, C_CKV, C_KR, C_G = 1536, 1920, 2176, 2304
IN_COLS = 2432


def _const_spec(shape):
    nd = len(shape)
    return pl.BlockSpec(shape, lambda *_: (0,) * nd, pipeline_mode=pl.Buffered(1))


def _params(n_axes):
    return pltpu.CompilerParams(dimension_semantics=("arbitrary",) * n_axes,
                                vmem_limit_bytes=VMEM_LIMIT_BYTES)


def _rms(x, g):
    return x * lax.rsqrt(jnp.mean(x * x, axis=-1, keepdims=True) + EPS) * g


def _sigmoid(x):
    return 1.0 / (1.0 + jnp.exp(-x))


def _dot(a, b):
    return jnp.dot(a, b, preferred_element_type=F32)


def _dot_nt(a, b):
    return lax.dot_general(a, b, (((1,), (1,)), ((), ())), preferred_element_type=F32)


def _fold_halves(p):
    return p + pltpu.roll(p, 64, 1)


def _proj_kernel(x_ref, tab_ref, gmix_ref, win_ref, gb_ref, gq_ref, wuq_ref, gkv_ref, wuk_ref,
                 qk_ref, v_ref, o_ref, gates_ref, lat_ref, katt_ref, *q_refs, feature_major):
    if feature_major:
        katt_t_ref, qatt_ref = q_refs
    else:
        qatt_ref, = q_refs
    h = _rms(x_ref[...], gmix_ref[...]).astype(BF16)
    z = _dot(h, win_ref[...])
    qk_ref[:, 0:256] = z[:, C_Q:C_Q + 256].astype(BF16)
    qk_ref[:, 256:512] = (z[:, C_K:C_K + 256] * (M_DK ** -0.5)).astype(BF16)
    v_ref[...] = z[:, C_V:C_V + 512].astype(BF16)
    o_ref[...] = z[:, C_O:C_O + 512]

    gt = z[:, C_G:C_G + LANES].T[0:SUBLANES, :] + gb_ref[...]
    capped = GATE_CAP * jnp.tanh(gt / GATE_CAP)
    log_sig = jnp.minimum(capped, 0.0) - jnp.log1p(jnp.exp(-jnp.abs(capped)))
    row = lax.broadcasted_iota(jnp.int32, gt.shape, 0)
    gates_ref[...] = jnp.where(row < M_HEADS, log_sig, capped)

    tab = tab_ref[...]
    lane = lax.broadcasted_iota(jnp.int32, tab.shape, 1)
    ckv_n = _rms(z[:, C_CKV:C_CKV + KV_RANK], gkv_ref[...])
    rk = _fold_halves(z[:, C_KR:C_KR + LANES] * tab)
    lat_ref[:, 0:KV_RANK] = ckv_n
    lat_ref[:, KV_RANK:LAT_DIM] = rk[:, 0:A_ROPE]
    k_cat = jnp.concatenate([ckv_n, jnp.where(lane < A_ROPE, rk, 0.0)], axis=1)
    katt_ref[...] = k_cat.astype(BF16)
    if feature_major:
        katt_t_ref[...] = k_cat.T.astype(BF16)

    qn = _rms(z[:, C_CQ:C_CQ + Q_RANK], gq_ref[...]).astype(BF16)
    qa = _dot(qn, wuq_ref[...])
    for hd in range(A_HEADS):
        q_lat = _dot(qa[:, hd * A_NOPE:(hd + 1) * A_NOPE].astype(BF16), wuk_ref[hd])
        rq = _fold_halves(qa[:, 512 + hd * LANES:512 + (hd + 1) * LANES] * tab)
        q_cat = jnp.concatenate([q_lat, jnp.where(lane < A_ROPE, rq, 0.0)], axis=1) * Q_SCALE
        qatt_ref[hd] = (q_cat.T if feature_major else q_cat).astype(BF16)


def _proj(x, tab, lw, feature_major, gate_seqs):
    n = x.shape[0]
    tm = TOK_TILE
    tiles_per_seq = n // gate_seqs // tm
    row = lambda i: (i, 0)
    out_shape = [
        jax.ShapeDtypeStruct((n, 512), BF16),
        jax.ShapeDtypeStruct((n, 512), BF16),
        jax.ShapeDtypeStruct((n, 512), F32),
        jax.ShapeDtypeStruct((gate_seqs * SUBLANES, n // gate_seqs), F32),
        jax.ShapeDtypeStruct((n, LAT_DIM), F32),
        jax.ShapeDtypeStruct((n, LAT_PAD), BF16),
    ]
    out_specs = [
        pl.BlockSpec((tm, 512), row),
        pl.BlockSpec((tm, 512), row),
        pl.BlockSpec((tm, 512), row),
        pl.BlockSpec((SUBLANES, tm), lambda i: (i // tiles_per_seq, i % tiles_per_seq)),
        pl.BlockSpec((tm, LAT_DIM), row),
        pl.BlockSpec((tm, LAT_PAD), row),
    ]
    if feature_major:
        out_shape += [jax.ShapeDtypeStruct((LAT_PAD, n), BF16),
                      jax.ShapeDtypeStruct((A_HEADS, LAT_PAD, n), BF16)]
        out_specs += [pl.BlockSpec((LAT_PAD, tm), lambda i: (0, i)),
                      pl.BlockSpec((A_HEADS, LAT_PAD, tm), lambda i: (0, 0, i))]
    else:
        out_shape.append(jax.ShapeDtypeStruct((A_HEADS, n, LAT_PAD), BF16))
        out_specs.append(pl.BlockSpec((A_HEADS, tm, LAT_PAD), lambda i: (0, i, 0)))
    return pl.pallas_call(
        functools.partial(_proj_kernel, feature_major=feature_major),
        grid=(n // tm,),
        in_specs=[
            pl.BlockSpec((tm, x.shape[1]), row),
            pl.BlockSpec((tm, LANES), row),
            _const_spec(lw['norm_mix'].shape),
            _const_spec(lw['w_in'].shape),
            _const_spec(lw['gate_bias'].shape),
            _const_spec(lw['mla_q_norm'].shape),
            _const_spec(lw['w_uq'].shape),
            _const_spec(lw['mla_kv_norm'].shape),
            _const_spec(lw['w_uk'].shape),
        ],
        out_specs=tuple(out_specs),
        out_shape=tuple(out_shape),
        compiler_params=_params(1),
        name="proj_prompt" if feature_major else "proj_sample",
    )(x, tab, lw['norm_mix'], lw['w_in'], lw['gate_bias'], lw['mla_q_norm'], lw['w_uq'],
      lw['mla_kv_norm'], lw['w_uk'])


def _seg_scan(x, op, fill, seg):
    n = x.shape[1]
    lane = lax.broadcasted_iota(jnp.int32, x.shape, 1)
    pos = lane if seg is None else lane % seg
    span = n if seg is None else seg
    s = 1
    while s < span:
        x = op(x, jnp.where(pos >= s, pltpu.roll(x, s, 1), fill))
        s *= 2
    return x


def _gate_rows(gates, m_prev, seg):
    ig = pltpu.roll(gates, M_HEADS, 0)
    b = _seg_scan(gates, jnp.add, 0.0, seg)
    d = ig - b
    e = _seg_scan(d, jnp.maximum, -jnp.inf, seg)
    m = b + jnp.maximum(m_prev, e)
    return b, d, ig, m


def _to_cols(row_blocks, n_tok):
    pad = jnp.zeros((LANES - SUBLANES * len(row_blocks), n_tok), F32)
    return jnp.concatenate(list(row_blocks) + [pad], axis=0).T


def _head_out(hh, og, gain):
    hn = hh * lax.rsqrt(jnp.mean(hh * hh, axis=-1, keepdims=True) + EPS) * gain
    return (hn * _sigmoid(og)).astype(BF16)


def _mlstm_prompt_kernel(qk_ref, v_ref, o_ref, g_ref, gain_ref, hm_ref, c_ref, n_ref, m_ref):
    n_seq, L = qk_ref.shape[0], qk_ref.shape[1]

    @pl.when(pl.program_id(0) == 0)
    def _():
        c_ref[...] = jnp.zeros_like(c_ref)
        n_ref[...] = jnp.zeros_like(n_ref)
        m_ref[...] = jnp.zeros_like(m_ref)

    qi = lax.broadcasted_iota(jnp.int32, (L, L), 0)
    ki = lax.broadcasted_iota(jnp.int32, (L, L), 1)
    causal = ki <= qi
    row = lax.broadcasted_iota(jnp.int32, (SUBLANES, LANES), 0)

    for sq in range(n_seq):
        m_prev = m_ref[sq, :, 0:1]
        b, d, ig, m = _gate_rows(g_ref[sq * SUBLANES:(sq + 1) * SUBLANES, :], m_prev, None)
        b_last = b[:, L - 1:L]
        m_new = m[:, L - 1:L]
        cols = _to_cols((b, m, ig), L)

        qs = [qk_ref[sq, :, hd * M_DK:(hd + 1) * M_DK] for hd in range(M_HEADS)]
        ks = [qk_ref[sq, :, 256 + hd * M_DK:256 + (hd + 1) * M_DK] for hd in range(M_HEADS)]
        c_olds = [c_ref[sq, hd] for hd in range(M_HEADS)]
        qk_scores = [_dot_nt(qs[hd], ks[hd]) for hd in range(M_HEADS)]
        q_state = [_dot(qs[hd], c_olds[hd].astype(BF16)) for hd in range(M_HEADS)]

        for hd in range(M_HEADS):
            q, k, c_old = qs[hd], ks[hd], c_olds[hd]
            v = v_ref[sq, :, hd * M_DV:(hd + 1) * M_DV]
            b_c = cols[:, hd:hd + 1]
            m_c = cols[:, 8 + hd:9 + hd]
            i_c = cols[:, 16 + hd:17 + hd]
            mp = m_prev[hd:hd + 1, :]
            n_old = n_ref[sq, hd:hd + 1, :]

            p = jnp.exp(jnp.where(causal, b_c + d[hd:hd + 1, :] - m_c, -jnp.inf))
            s = qk_scores[hd] * p
            a_c = jnp.exp(b_c + mp - m_c)
            num = _dot(s.astype(BF16), v) + a_c * q_state[hd]
            nq = (jnp.sum(s, axis=1, keepdims=True)
                  + a_c * jnp.sum(q.astype(F32) * n_old, axis=1, keepdims=True))
            den = jnp.maximum(jnp.abs(nq), jnp.exp(-m_c))
            sl = slice(hd * M_DV, (hd + 1) * M_DV)
            hm_ref[sq, :, sl] = _head_out(num / den, o_ref[sq, :, sl], gain_ref[hd:hd + 1, :])

            w_c = jnp.exp(b_last[hd:hd + 1, :] - b_c + i_c - m_new[hd:hd + 1, :])
            decay = jnp.exp(b_last[hd:hd + 1, :] + mp - m_new[hd:hd + 1, :])
            kw = k.astype(F32) * w_c
            upd = lax.dot_general(kw.astype(BF16), v, (((0,), (0,)), ((), ())), preferred_element_type=F32)
            c_ref[sq, hd] = decay * c_old + upd
            n_ref[sq, hd:hd + 1, :] = decay * n_old + jnp.sum(kw, axis=0, keepdims=True)

        m_ref[sq] = jnp.where(row < M_HEADS, jnp.broadcast_to(m_new, row.shape), 0.0)


def _mlstm_prompt(qk, v, o, gates, gain, n_seq):
    t = qk.shape[1]
    L = MLSTM_CHUNK
    tok = lambda c: (0, c, 0)
    whole4 = lambda c: (0, 0, 0, 0)
    whole3 = lambda c: (0, 0, 0)
    return pl.pallas_call(
        _mlstm_prompt_kernel,
        grid=(t // L,),
        in_specs=[
            pl.BlockSpec((n_seq, L, 512), tok),
            pl.BlockSpec((n_seq, L, 512), tok),
            pl.BlockSpec((n_seq, L, 512), tok),
            pl.BlockSpec((n_seq * SUBLANES, L), lambda c: (0, c)),
            _const_spec(gain.shape),
        ],
        out_specs=(
            pl.BlockSpec((n_seq, L, 512), tok),
            pl.BlockSpec((n_seq, M_HEADS, M_DK, M_DV), whole4),
            pl.BlockSpec((n_seq, M_HEADS, M_DK), whole3),
            pl.BlockSpec((n_seq, SUBLANES, LANES), whole3),
        ),
        out_shape=(
            jax.ShapeDtypeStruct((n_seq, t, 512), BF16),
            jax.ShapeDtypeStruct((n_seq, M_HEADS, M_DK, M_DV), F32),
            jax.ShapeDtypeStruct((n_seq, M_HEADS, M_DK), F32),
            jax.ShapeDtypeStruct((n_seq, SUBLANES, LANES), F32),
        ),
        compiler_params=_params(1),
        name="mlstm_prompt",
    )(qk, v, o, gates, gain)


def _exact_onehot_dot(onehot, x):
    hi = x.astype(BF16)
    r1 = x - hi.astype(F32)
    mid = r1.astype(BF16)
    lo = (r1 - mid.astype(F32)).astype(BF16)
    return (_dot(onehot, hi) + _dot(onehot, mid)) + _dot(onehot, lo)


def _mlstm_sample_kernel(qk_ref, v_ref, o_ref, g_ref, mp_ref, gain_ref, c0_ref, n0_ref,
                         hm_ref, c_ref, n_ref, m_ref, cols_ref, *, t_seq):
    L = qk_ref.shape[0]
    G = L // t_seq
    mp_rows = jnp.concatenate([mp_ref[...], jnp.zeros((SUBLANES - M_HEADS, L), F32)], axis=0)
    b, d, ig, m = _gate_rows(g_ref[...], mp_rows, t_seq)
    m_ref[...] = m[0:M_HEADS, :]

    lane = lax.broadcasted_iota(jnp.int32, b.shape, 1)
    pos = lane % t_seq

    def seg_last(x):
        s = 1
        while s < t_seq:
            x = jnp.where(pos >= t_seq - s, x, pltpu.roll(x, L - s, 1))
            s *= 2
        return x

    b_last = seg_last(b)
    m_new = seg_last(m)
    w_rows = jnp.exp(b_last - b + ig - m_new)
    decay_rows = jnp.exp(b_last + mp_rows - m_new)
    cols_ref[...] = _to_cols((b, m, mp_rows, w_rows, decay_rows), L)
    cols = cols_ref[...]
    seq_cols = cols_ref[pl.ds(0, G, stride=t_seq), :]

    qi = lax.broadcasted_iota(jnp.int32, (L, L), 0)
    ki = lax.broadcasted_iota(jnp.int32, (L, L), 1)
    mask = (ki <= qi) & (ki // t_seq == qi // t_seq)
    er = lax.broadcasted_iota(jnp.int32, (L, G * M_DK), 0)
    ec = lax.broadcasted_iota(jnp.int32, (L, G * M_DK), 1)
    own = (ec // M_DK) == (er // t_seq)
    tok_of_seq = (lax.broadcasted_iota(jnp.int32, (L, G), 0) // t_seq
                  == lax.broadcasted_iota(jnp.int32, (L, G), 1)).astype(BF16)
    seq_of_tok = (lax.broadcasted_iota(jnp.int32, (G, L), 1) // t_seq
                  == lax.broadcasted_iota(jnp.int32, (G, L), 0)).astype(BF16)

    def expand(x):
        x2 = jnp.concatenate([x, x], axis=1)
        return jnp.where(own, jnp.tile(x2, (1, G // 2)), 0.0)

    for hd in range(M_HEADS):
        q = qk_ref[:, hd * M_DK:(hd + 1) * M_DK]
        k = qk_ref[:, 256 + hd * M_DK:256 + (hd + 1) * M_DK]
        v = v_ref[:, hd * M_DV:(hd + 1) * M_DV]
        b_c = cols[:, hd:hd + 1]
        m_c = cols[:, 8 + hd:9 + hd]
        mp_c = cols[:, 16 + hd:17 + hd]
        w_c = cols[:, 24 + hd:25 + hd]
        decay_seq = seq_cols[:, 32 + hd:33 + hd]
        n_old = n0_ref[hd]
        c_old = c0_ref[:, hd].reshape(G * M_DK, M_DV)

        p = jnp.exp(jnp.where(mask, b_c + d[hd:hd + 1, :] - m_c, -jnp.inf))
        s = _dot_nt(q, k) * p
        a_c = jnp.exp(b_c + mp_c - m_c)
        qf = q.astype(F32)
        num = _dot(s.astype(BF16), v) + a_c * _dot(expand(qf).astype(BF16), c_old.astype(BF16))
        n_tok = _exact_onehot_dot(tok_of_seq, n_old)
        nq = jnp.sum(s, axis=1, keepdims=True) + a_c * jnp.sum(qf * n_tok, axis=1, keepdims=True)
        den = jnp.maximum(jnp.abs(nq), jnp.exp(-m_c))
        sl = slice(hd * M_DV, (hd + 1) * M_DV)
        hm_ref[:, sl] = _head_out(num / den, o_ref[:, sl], gain_ref[hd:hd + 1, :])

        kw = k.astype(F32) * w_c
        upd = _dot(expand(kw).T.astype(BF16), v)
        for g in range(G):
            c_ref[g, hd] = decay_seq[g:g + 1, :] * c0_ref[g, hd] + upd[g * M_DK:(g + 1) * M_DK, :]
        n_ref[hd] = decay_seq * n_old + _dot(seq_of_tok, kw.astype(BF16))


def _mlstm_sample(qk, v, o, gates, m_prev_rows, gain, c0, n0_t, layer, t_seq):
    n = qk.shape[0]
    G = SAMPLE_GROUP
    L = G * t_seq
    n_seq = n // t_seq
    tok = lambda i: (i, 0)
    return pl.pallas_call(
        functools.partial(_mlstm_sample_kernel, t_seq=t_seq),
        grid=(n // L,),
        in_specs=[
            pl.BlockSpec((L, 512), tok),
            pl.BlockSpec((L, 512), tok),
            pl.BlockSpec((L, 512), tok),
            pl.BlockSpec((SUBLANES, L), lambda i: (0, i)),
            pl.BlockSpec((M_HEADS, L), lambda i: (0, i)),
            _const_spec(gain.shape),
            pl.BlockSpec((None, G, M_HEADS, M_DK, M_DV), lambda i: (layer, i, 0, 0, 0)),
            pl.BlockSpec((M_HEADS, G, M_DK), lambda i: (0, i, 0)),
        ],
        out_specs=(
            pl.BlockSpec((L, 512), tok),
            pl.BlockSpec((G, M_HEADS, M_DK, M_DV), lambda i: (i, 0, 0, 0)),
            pl.BlockSpec((M_HEADS, G, M_DK), lambda i: (0, i, 0)),
            pl.BlockSpec((M_HEADS, L), lambda i: (0, i)),
        ),
        out_shape=(
            jax.ShapeDtypeStruct((n, 512), BF16),
            jax.ShapeDtypeStruct((n_seq, M_HEADS, M_DK, M_DV), F32),
            jax.ShapeDtypeStruct((M_HEADS, n_seq, M_DK), F32),
            jax.ShapeDtypeStruct((M_HEADS, n), F32),
        ),
        scratch_shapes=[pltpu.VMEM((L, LANES), F32)],
        compiler_params=_params(1),
        name="mlstm_sample",
    )(qk, v, o, gates, m_prev_rows, gain, c0, n0_t)


def _attn_prompt_kernel(qt_ref, k_ref, kt_ref, o_ref, m_scr, l_scr, acc_scr):
    tq = qt_ref.shape[2]
    qb = pl.program_id(1)
    m_scr[...] = jnp.full_like(m_scr, -jnp.inf)
    l_scr[...] = jnp.zeros_like(l_scr)
    acc_scr[...] = jnp.zeros_like(acc_scr)

    def block(kb, masked):
        start = pl.multiple_of(kb * tq, tq)
        kblk = k_ref[pl.ds(start, tq), :]
        vt = kt_ref[0:KV_RANK, pl.ds(start, tq)]
        scores = {0: _dot(kblk, qt_ref[0])}
        for hd in range(A_HEADS):
            if hd + 1 < A_HEADS:
                scores[hd + 1] = _dot(kblk, qt_ref[hd + 1])
            st = scores.pop(hd)
            if masked:
                kpos = lax.broadcasted_iota(jnp.int32, st.shape, 0)
                qpos = lax.broadcasted_iota(jnp.int32, st.shape, 1)
                st = jnp.where(kpos <= qpos, st, -jnp.inf)
            m_old = m_scr[hd:hd + 1, :]
            m_new = jnp.maximum(m_old, jnp.max(st, axis=0, keepdims=True))
            alpha = jnp.exp2(m_old - m_new)
            p = jnp.exp2(st - m_new)
            l_scr[hd:hd + 1, :] = alpha * l_scr[hd:hd + 1, :] + jnp.sum(p, axis=0, keepdims=True)
            acc_scr[hd] = alpha * acc_scr[hd] + _dot(vt, p.astype(BF16))
            m_scr[hd:hd + 1, :] = m_new

    def body(kb, carry):
        block(kb, False)
        return carry

    lax.fori_loop(0, qb, body, 0)
    block(qb, True)
    for hd in range(A_HEADS):
        o_ref[hd] = (acc_scr[hd] / l_scr[hd:hd + 1, :]).T.astype(BF16)


def _attn_prompt(qatt_t, katt, katt_t, n_seq):
    n = katt.shape[0]
    t = n // n_seq
    tq = ATT_BLOCK
    nq = t // tq
    return pl.pallas_call(
        _attn_prompt_kernel,
        grid=(n_seq, nq),
        in_specs=[
            pl.BlockSpec((A_HEADS, LAT_PAD, tq), lambda b, i: (0, 0, b * nq + i)),
            pl.BlockSpec((t, LAT_PAD), lambda b, i: (b, 0)),
            pl.BlockSpec((LAT_PAD, t), lambda b, i: (0, b)),
        ],
        out_specs=pl.BlockSpec((A_HEADS, tq, KV_RANK), lambda b, i: (0, b * nq + i, 0)),
        out_shape=jax.ShapeDtypeStruct((A_HEADS, n, KV_RANK), BF16),
        scratch_shapes=[pltpu.VMEM((SUBLANES, tq), F32), pltpu.VMEM((SUBLANES, tq), F32),
                        pltpu.VMEM((A_HEADS, KV_RANK, tq), F32)],
        compiler_params=_params(2),
        name="attn_prompt",
    )(qatt_t, katt, katt_t)


def _attn_sample_kernel(pt_ref, q_ref, knew_ref, *rest, n_pages, t_seq):
    page_refs = rest[:n_pages]
    o_ref = rest[n_pages]
    kbuf = rest[n_pages + 1]
    page = page_refs[0].shape[1]
    for j in range(n_pages):
        kbuf[:, j * page:(j + 1) * page] = page_refs[j][...].astype(BF16)

    q = q_ref[...]
    rows = q.shape[0]
    s_past = _dot(q[:, 0:LAT_DIM], kbuf[...])
    qf = q.astype(F32)
    knew = knew_ref[...].astype(F32)
    tq = lax.broadcasted_iota(jnp.int32, (rows, 1), 0) % t_seq
    s_new = []
    for j in range(t_seq):
        sj = jnp.sum(qf * knew[j:j + 1, :], axis=1, keepdims=True)
        s_new.append(jnp.where(tq >= j, sj, -jnp.inf))
    m = jnp.max(s_past, axis=1, keepdims=True)
    for sj in s_new:
        m = jnp.maximum(m, sj)
    p_past = jnp.exp2(s_past - m)
    p_new = [jnp.exp2(sj - m) for sj in s_new]
    denom = jnp.sum(p_past, axis=1, keepdims=True)
    for pj in p_new:
        denom = denom + pj
    inv = 1.0 / denom
    out = _dot_nt((p_past * inv).astype(BF16), kbuf[0:KV_RANK, :])
    for j in range(t_seq):
        pj = (p_new[j] * inv).astype(BF16).astype(F32)
        out = out + pj * knew[j:j + 1, 0:KV_RANK]
    o_ref[...] = out.astype(BF16)


def _attn_sample(q_seq, knew_seq, cache_t, page_table, layer):
    n_seq, rows, _ = q_seq.shape
    t_seq = knew_seq.shape[1]
    n_pages = page_table.shape[1]
    page = cache_t.shape[3]

    def page_spec(j):
        return pl.BlockSpec((None, None, LAT_DIM, page), lambda s, pt: (layer, pt[s, j], 0, 0))

    grid_spec = pltpu.PrefetchScalarGridSpec(
        num_scalar_prefetch=1,
        grid=(n_seq,),
        in_specs=[pl.BlockSpec((None, rows, LAT_PAD), lambda s, pt: (s, 0, 0)),
                  pl.BlockSpec((None, t_seq, LAT_PAD), lambda s, pt: (s, 0, 0))]
                 + [page_spec(j) for j in range(n_pages)],
        out_specs=pl.BlockSpec((None, rows, KV_RANK), lambda s, pt: (s, 0, 0)),
        scratch_shapes=[pltpu.VMEM((LAT_DIM, n_pages * page), BF16)],
    )
    return pl.pallas_call(
        functools.partial(_attn_sample_kernel, n_pages=n_pages, t_seq=t_seq),
        grid_spec=grid_spec,
        out_shape=jax.ShapeDtypeStruct((n_seq, rows, KV_RANK), BF16),
        compiler_params=_params(1),
        name="attn_sample",
    )(page_table, q_seq, knew_seq, *([cache_t] * n_pages))


def _ffn_kernel(*refs, is_sample, is_last, t_seq, tiles_per_seq):
    (x_ref, hm_ref, ol_ref, p_ref, wuv_ref, wout_ref, gffn_ref, wup_ref, wc_ref, cb_ref,
     wdown_ref, gple_ref, wpg_ref, wpp_ref) = refs[:14]
    refs = refs[14:]
    if is_last:
        gfin_ref, refs = refs[0], refs[1:]
    if is_sample:
        s1_ref, s2_ref, x_out, u_out = refs
    else:
        x_out, tail_ref = refs
    tm = x_ref.shape[0]
    d_ff = wdown_ref.shape[0]

    ha = [_dot(ol_ref[hd], wuv_ref[hd]).astype(BF16) for hd in range(A_HEADS)]
    mix = jnp.concatenate([hm_ref[...]] + ha, axis=1)
    x = x_ref[...] + _dot(mix, wout_ref[...])

    hf = _rms(x, gffn_ref[...]).astype(BF16)
    row = lax.broadcasted_iota(jnp.int32, (tm, FF_BLOCK), 0)
    if not is_sample:
        @pl.when(pl.program_id(0) % tiles_per_seq == 0)
        def _():
            tail_ref[...] = jnp.zeros_like(tail_ref)

    def gate_up_cols(j):
        return (slice(j * FF_BLOCK, (j + 1) * FF_BLOCK),
                slice(d_ff + j * FF_BLOCK, d_ff + (j + 1) * FF_BLOCK))

    def project(j):
        return tuple(_dot(hf, wup_ref[:, cols]) for cols in gate_up_cols(j))

    def conv(u, cols):
        r1 = pltpu.roll(u, 1, 0)
        r2 = pltpu.roll(u, 2, 0)
        if is_sample:
            u_out[:, cols] = u
            pos = row % t_seq
            u1 = jnp.where(pos >= 1, r1, s1_ref[:, cols])
            u2 = jnp.where(pos >= 2, r2, s2_ref[:, cols])
        else:
            prev = tail_ref[:, cols]
            last, last2 = prev[SUBLANES - 1:SUBLANES, :], prev[SUBLANES - 2:SUBLANES - 1, :]
            u1 = jnp.where(row >= 1, r1, last)
            u2 = jnp.where(row >= 2, r2, jnp.where(row == 1, last, last2))
            tail_ref[:, cols] = u[tm - SUBLANES:tm, :]
        return (u2 * wc_ref[0:1, cols] + u1 * wc_ref[1:2, cols]) + u * wc_ref[2:3, cols] + cb_ref[:, cols]

    n_blk = d_ff // FF_BLOCK
    acc = jnp.zeros((tm, x.shape[1]), F32)
    nxt = project(0)
    for j in range(n_blk):
        cur = nxt
        if j + 1 < n_blk:
            nxt = project(j + 1)
        gate, up = (conv(u, cols) for u, cols in zip(cur, gate_up_cols(j)))
        act = (gate * _sigmoid(gate) * up).astype(BF16)
        acc = acc + _dot(act, wdown_ref[j * FF_BLOCK:(j + 1) * FF_BLOCK, :])
    x = x + acc

    hp = _rms(x, gple_ref[...]).astype(BF16)
    x = x + _sigmoid(_dot(hp, wpg_ref[...])) * _dot(p_ref[...].astype(BF16), wpp_ref[...])
    x_out[...] = _rms(x, gfin_ref[...]) if is_last else x


def _ffn(x, hm, olat, p_in, lw, *, final_gain, conv_state, n_seq, t_seq):
    n, d = x.shape
    d_up = lw['w_up'].shape[1]
    is_sample = conv_state is not None
    tm = SAMPLE_FFN_TILE if is_sample else TOK_TILE
    is_last = final_gain is not None
    row = lambda i: (i, 0)
    weights = [lw['w_uv'], lw['w_out'], lw['norm_ffn'], lw['w_up'], lw['ffn_conv'], lw['ffn_conv_b'],
               lw['w_down'], lw['ple_norm'], lw['w_pg'], lw['w_pp']]
    if is_last:
        weights.append(final_gain)
    args = [x, hm, olat, p_in] + weights
    in_specs = [pl.BlockSpec((tm, d), row), pl.BlockSpec((tm, 512), row),
                pl.BlockSpec((A_HEADS, tm, KV_RANK), lambda i: (0, i, 0)),
                pl.BlockSpec((tm, p_in.shape[1]), row)] + [_const_spec(w.shape) for w in weights]
    out_shape = [jax.ShapeDtypeStruct((n, d), F32)]
    out_specs = [pl.BlockSpec((tm, d), row)]
    if is_sample:
        args += list(conv_state)
        in_specs += [pl.BlockSpec((tm, d_up), row)] * 2
        out_shape.append(jax.ShapeDtypeStruct((n, d_up), F32))
        out_specs.append(pl.BlockSpec((tm, d_up), row))
        tiles_per_seq = 1
    else:
        tiles_per_seq = n // n_seq // tm
        out_shape.append(jax.ShapeDtypeStruct((n_seq, SUBLANES, d_up), F32))
        out_specs.append(pl.BlockSpec((None, SUBLANES, d_up), lambda i: (i // tiles_per_seq, 0, 0)))
    return pl.pallas_call(
        functools.partial(_ffn_kernel, is_sample=is_sample, is_last=is_last, t_seq=t_seq,
                          tiles_per_seq=tiles_per_seq),
        grid=(n // tm,),
        in_specs=in_specs,
        out_specs=tuple(out_specs),
        out_shape=tuple(out_shape),
        compiler_params=_params(1),
        name="ffn_sample" if is_sample else "ffn_prompt",
    )(*args)


def _swap_halves(w):
    half = w.shape[-1] // 2
    return jnp.concatenate([w[..., half:], w[..., :half]], axis=-1)


def _layer_weights(l, norm_mix, w_in, m_gate_bias, m_norm, mla_q_norm, mla_w_uq, mla_kv_norm, mla_w_uk,
                   mla_w_uv, w_out, norm_ffn, ffn_w_up, ffn_conv, ffn_conv_b, ffn_w_down, ple_norm,
                   ple_w_gate, ple_w_proj):
    d = w_in.shape[1]
    wi = w_in[l]
    sizes = (256, 256, 512, 512, M_HEADS, M_HEADS, Q_RANK, KV_RANK, A_ROPE)
    offs = [0]
    for s in sizes:
        offs.append(offs[-1] + s)
    qm, km, vm, om, im, fm, cq, ckv, kr = (wi[:, offs[i]:offs[i + 1]] for i in range(len(sizes)))
    gate_pad = jnp.zeros((d, LANES - 2 * M_HEADS), wi.dtype)
    w_in_r = jnp.concatenate([qm, km, vm, om, cq, ckv, kr, _swap_halves(kr), fm, im, gate_pad], axis=1)
    uq = mla_w_uq[l].reshape(Q_RANK, A_HEADS, A_NOPE + A_ROPE)
    uq_rope = uq[:, :, A_NOPE:]
    w_uq_r = jnp.concatenate([uq[:, :, :A_NOPE].reshape(Q_RANK, A_HEADS * A_NOPE),
                              jnp.concatenate([uq_rope, _swap_halves(uq_rope)], axis=-1)
                              .reshape(Q_RANK, A_HEADS * LANES)], axis=1)
    row = lambda v: v.reshape(1, -1)
    return {
        'norm_mix': row(norm_mix[l]),
        'w_in': w_in_r.astype(BF16),
        'gate_bias': jnp.concatenate([m_gate_bias[l, 1], m_gate_bias[l, 0]]).reshape(SUBLANES, 1),
        'm_norm': m_norm[l],
        'mla_q_norm': row(mla_q_norm[l]),
        'w_uq': w_uq_r.astype(BF16),
        'mla_kv_norm': row(mla_kv_norm[l]),
        'w_uk': jnp.transpose(mla_w_uk[l], (1, 2, 0)).astype(BF16),
        'w_uv': jnp.transpose(mla_w_uv[l], (1, 0, 2)).astype(BF16),
        'w_out': w_out[l].astype(BF16),
        'norm_ffn': row(norm_ffn[l]),
        'w_up': ffn_w_up[l].astype(BF16),
        'ffn_conv': ffn_conv[l],
        'ffn_conv_b': row(ffn_conv_b[l]),
        'w_down': ffn_w_down[l].astype(BF16),
        'ple_norm': row(ple_norm[l]),
        'w_pg': ple_w_gate[l].astype(BF16),
        'w_pp': ple_w_proj[l].astype(BF16),
    }


def _rope_table(pos):
    half = A_ROPE // 2
    freqs = ROPE_THETA ** (-jnp.arange(half, dtype=F32) * 2.0 / A_ROPE)
    ang = pos.astype(F32)[:, None] * freqs[None, :]
    cos, sin = jnp.cos(ang), jnp.sin(ang)
    return jnp.concatenate([cos, cos, -sin, sin], axis=1)


def kernel(x_prompt, x_sample, p_prompt, p_sample, cache_mla, state_mlstm_C, state_mlstm_n, state_mlstm_m,
           state_ffn_conv, page_table, norm_mix, w_in, m_gate_bias, m_norm, mla_q_norm, mla_w_uq, mla_kv_norm,
           mla_w_uk, mla_w_uv, w_out, norm_ffn, ffn_w_up, ffn_conv, ffn_conv_b, ffn_w_down, ple_norm,
           ple_w_gate, ple_w_proj, final_norm):
    depth = w_in.shape[0]
    bp, tp, d = x_prompt.shape
    bs, ts, _ = x_sample.shape
    n_p, n_s = bp * tp, bs * ts
    past_len = page_table.shape[1] * cache_mla.shape[2]
    d_up = ffn_w_up.shape[2]

    tab_p = jnp.tile(_rope_table(jnp.arange(tp)), (bp, 1))
    tab_s = jnp.tile(_rope_table(past_len + jnp.arange(ts)), (bs, 1))
    xp = x_prompt.reshape(n_p, d)
    xs = x_sample.reshape(n_s, d)
    final_gain = final_norm.reshape(1, d)
    cache_t = jnp.swapaxes(cache_mla, 2, 3)

    lat_p, lat_s, c_p, n_pl, m_p, c_s, n_sl, m_s, conv_p, conv_s = ([] for _ in range(10))
    for l in range(depth):
        lw = _layer_weights(l, norm_mix, w_in, m_gate_bias, m_norm, mla_q_norm, mla_w_uq, mla_kv_norm,
                            mla_w_uk, mla_w_uv, w_out, norm_ffn, ffn_w_up, ffn_conv, ffn_conv_b,
                            ffn_w_down, ple_norm, ple_w_gate, ple_w_proj)
        fin = final_gain if l == depth - 1 else None

        qk, v, o, gates, lat, katt, katt_t, qatt_t = _proj(xp, tab_p, lw, True, bp)
        by_seq = lambda a: a.reshape(bp, tp, a.shape[-1])
        hm, c_new, n_new, m_new = _mlstm_prompt(by_seq(qk), by_seq(v), by_seq(o), gates, lw['m_norm'], bp)
        olat = _attn_prompt(qatt_t, katt, katt_t, bp)
        xp, tail = _ffn(xp, hm.reshape(n_p, -1), olat, p_prompt[l].reshape(n_p, -1), lw, final_gain=fin,
                        conv_state=None, n_seq=bp, t_seq=tp)
        lat_p.append(lat.reshape(bp, tp, LAT_DIM))
        c_p.append(c_new)
        n_pl.append(n_new)
        m_p.append(m_new[:, :M_HEADS, 0])
        conv_p.append(tail[:, SUBLANES - (CONV_W - 1):])

        qk, v, o, gates, lat, katt, qatt = _proj(xs, tab_s, lw, False, 1)
        m_prev_rows = jnp.repeat(state_mlstm_m[l].T, ts, axis=1)
        hm, c_new, n_new_t, m_rows = _mlstm_sample(qk, v, o, gates, m_prev_rows, lw['m_norm'],
                                                   state_mlstm_C, jnp.transpose(state_mlstm_n[l], (1, 0, 2)),
                                                   l, ts)
        q_seq = jnp.transpose(qatt.reshape(A_HEADS, bs, ts, LAT_PAD), (1, 0, 2, 3)).reshape(bs, A_HEADS * ts, LAT_PAD)
        o_seq = _attn_sample(q_seq, katt.reshape(bs, ts, LAT_PAD), cache_t, page_table, l)
        olat = jnp.transpose(o_seq.reshape(bs, A_HEADS, ts, KV_RANK), (1, 0, 2, 3)).reshape(A_HEADS, n_s, KV_RANK)
        buf0 = state_ffn_conv[l]
        zero = jnp.zeros((bs, d_up), F32)
        s1 = jnp.stack([buf0[:, 1]] + [zero] * (ts - 1), axis=1).reshape(n_s, d_up)
        s2 = jnp.stack([buf0[:, 0], buf0[:, 1]] + [zero] * (ts - 2), axis=1).reshape(n_s, d_up)
        xs, u_s = _ffn(xs, hm, olat, p_sample[l].reshape(n_s, -1), lw, final_gain=fin,
                       conv_state=(s1, s2), n_seq=bs, t_seq=ts)
        lat_s.append(lat.reshape(bs, ts, LAT_DIM))
        c_s.append(c_new)
        n_sl.append(jnp.transpose(n_new_t, (1, 0, 2)))
        m_s.append(m_rows[:, ts - 1::ts].T)
        conv_s.append(u_s.reshape(bs, ts, d_up)[:, ts - (CONV_W - 1):])

    return (xp.reshape(bp, tp, d), xs.reshape(bs, ts, d), jnp.stack(lat_p), jnp.stack(lat_s),
            jnp.stack(c_p), jnp.stack(n_pl), jnp.stack(m_p), jnp.stack(c_s), jnp.stack(n_sl), jnp.stack(m_s),
            jnp.stack(conv_p), jnp.stack(conv_s))
```

```python
import functools

import jax
import jax.numpy as jnp
from jax import lax
from jax.experimental import pallas as pl
from jax.experimental.pallas import tpu as pltpu

F32 = jnp.float32
BF16 = jnp.bfloat16

M_HEADS = 4
M_DK = 64
M_DV = 128
A_HEADS = 4
A_NOPE = 128
A_ROPE = 64
KV_RANK = 256
Q_RANK = 384
LAT_DIM = KV_RANK + A_ROPE
GATE_CAP = 15.0
ROPE_THETA = 10000.0
EPS = 1e-6
CONV_W = 3
ATT_SCALE = (A_NOPE + A_ROPE) ** -0.5
LOG2_E = 1.4426950408889634
Q_SCALE = ATT_SCALE * LOG2_E

LANES = 128
SUBLANES = 8
VMEM_LIMIT_BYTES = 56 * 1024 * 1024

LAT_PAD = 3 * LANES
TOK_TILE = 256
MLSTM_CHUNK = 256
ATT_BLOCK = 512
SAMPLE_GROUP = 32
STATE_ROWS = M_DV + SUBLANES
SAMPLE_FFN_TILE = 128

C_G = 0
C_Q, C_K, C_V, C_O = 128, 384, 640, 1152
C_CQ, C_CKV, C_KR = 1664, 2048, 2304
IN_COLS = 2432


def _const_spec(shape):
    nd = len(shape)
    return pl.BlockSpec(shape, lambda *_: (0,) * nd, pipeline_mode=pl.Buffered(1))


def _params(n_axes):
    return pltpu.CompilerParams(dimension_semantics=("arbitrary",) * n_axes,
                                vmem_limit_bytes=VMEM_LIMIT_BYTES)


def _rms(x, g):
    return x * lax.rsqrt(jnp.mean(x * x, axis=-1, keepdims=True) + EPS) * g


def _sigmoid(x):
    return 1.0 / (1.0 + jnp.exp(-x))


def _dot(a, b):
    return jnp.dot(a, b, preferred_element_type=F32)


def _dot_nt(a, b):
    return lax.dot_general(a, b, (((1,), (1,)), ((), ())), preferred_element_type=F32)


def _fold_halves(p):
    return p + pltpu.roll(p, 64, 1)


def _proj_kernel(x_ref, tab_ref, gmix_ref, win_ref, gb_ref, gq_ref, wuq_ref, gkv_ref, wuk_ref,
                 *out_refs, feature_major):
    if feature_major:
        (km_ref, qt_ref, vt_ref, ot_ref, gscan_ref, gdiff_ref, dcol_ref,
         lat_ref, katt_ref, katt_t_ref, qatt_ref) = out_refs
    else:
        qk_ref, v_ref, o_ref, gates_ref, lat_ref, katt_ref, qatt_ref = out_refs
    h = _rms(x_ref[...], gmix_ref[...]).astype(BF16)
    z = _dot(h, win_ref[...])
    k_m = (z[:, C_K:C_K + 256] * (M_DK ** -0.5)).astype(BF16)
    if feature_major:
        km_ref[...] = k_m
        qt_ref[...] = z[:, C_Q:C_Q + 256].T.astype(BF16)
        vt_ref[...] = z[:, C_V:C_V + 512].T.astype(BF16)
        ot_ref[...] = z[:, C_O:C_O + 512].T
    else:
        qk_ref[:, 0:256] = z[:, C_Q:C_Q + 256].astype(BF16)
        qk_ref[:, 256:512] = k_m
        v_ref[...] = z[:, C_V:C_V + 512].astype(BF16)
        o_ref[...] = z[:, C_O:C_O + 512]

    gt = z[:, C_G:C_G + LANES].T[0:SUBLANES, :] + gb_ref[...]
    capped = GATE_CAP * jnp.tanh(gt / GATE_CAP)
    log_sig = jnp.minimum(capped, 0.0) - jnp.log1p(jnp.exp(-jnp.abs(capped)))
    row = lax.broadcasted_iota(jnp.int32, gt.shape, 0)
    gates = jnp.where(row < M_HEADS, log_sig, capped)
    if feature_major:
        b, d, e = _gate_scans(gates, None)
        gscan_ref[...] = jnp.where(row < M_HEADS, b, pltpu.roll(e, M_HEADS, 0))
        d = jnp.where(row < M_HEADS, d, 0.0)
        gdiff_ref[...] = d
        dcol_ref[...] = _to_cols((d,), d.shape[1])
    else:
        gates_ref[...] = gates

    tab = tab_ref[...]
    lane = lax.broadcasted_iota(jnp.int32, tab.shape, 1)
    ckv_n = _rms(z[:, C_CKV:C_CKV + KV_RANK], gkv_ref[...])
    rk = _fold_halves(z[:, C_KR:C_KR + LANES] * tab)
    lat_ref[:, 0:KV_RANK] = ckv_n
    lat_ref[:, KV_RANK:LAT_DIM] = rk[:, 0:A_ROPE]
    k_cat = jnp.concatenate([ckv_n, jnp.where(lane < A_ROPE, rk, 0.0)], axis=1)
    katt_ref[...] = k_cat.astype(BF16)
    if feature_major:
        katt_t_ref[...] = k_cat.T.astype(BF16)

    qn = _rms(z[:, C_CQ:C_CQ + Q_RANK], gq_ref[...]).astype(BF16)
    qa = _dot(qn, wuq_ref[...])
    for hd in range(A_HEADS):
        q_lat = _dot(qa[:, hd * A_NOPE:(hd + 1) * A_NOPE].astype(BF16), wuk_ref[hd])
        rq = _fold_halves(qa[:, 512 + hd * LANES:512 + (hd + 1) * LANES] * tab)
        q_cat = jnp.concatenate([q_lat, jnp.where(lane < A_ROPE, rq, 0.0)], axis=1) * Q_SCALE
        qatt_ref[hd] = (q_cat.T if feature_major else q_cat).astype(BF16)


def _proj(x, tab, lw, feature_major, gate_seqs):
    n = x.shape[0]
    tm = TOK_TILE
    tiles_per_seq = n // gate_seqs // tm
    row = lambda i: (i, 0)
    col = lambda i: (0, i)
    band = lambda i: (i // tiles_per_seq, i % tiles_per_seq)
    gate_rows = jax.ShapeDtypeStruct((gate_seqs * SUBLANES, n // gate_seqs), F32)
    if feature_major:
        assert tm == MLSTM_CHUNK
        out_shape = [
            jax.ShapeDtypeStruct((n, 256), BF16),
            jax.ShapeDtypeStruct((256, n), BF16),
            jax.ShapeDtypeStruct((512, n), BF16),
            jax.ShapeDtypeStruct((512, n), F32),
            gate_rows,
            gate_rows,
            jax.ShapeDtypeStruct((n, LANES), F32),
        ]
        out_specs = [
            pl.BlockSpec((tm, 256), row),
            pl.BlockSpec((256, tm), col),
            pl.BlockSpec((512, tm), col),
            pl.BlockSpec((512, tm), col),
            pl.BlockSpec((SUBLANES, tm), band),
            pl.BlockSpec((SUBLANES, tm), band),
            pl.BlockSpec((tm, LANES), row),
        ]
    else:
        out_shape = [
            jax.ShapeDtypeStruct((n, 512), BF16),
            jax.ShapeDtypeStruct((n, 512), BF16),
            jax.ShapeDtypeStruct((n, 512), F32),
            gate_rows,
        ]
        out_specs = [
            pl.BlockSpec((tm, 512), row),
            pl.BlockSpec((tm, 512), row),
            pl.BlockSpec((tm, 512), row),
            pl.BlockSpec((SUBLANES, tm), band),
        ]
    out_shape += [jax.ShapeDtypeStruct((n, LAT_DIM), F32),
                  jax.ShapeDtypeStruct((n, LAT_PAD), BF16)]
    out_specs += [pl.BlockSpec((tm, LAT_DIM), row), pl.BlockSpec((tm, LAT_PAD), row)]
    if feature_major:
        out_shape += [jax.ShapeDtypeStruct((LAT_PAD, n), BF16),
                      jax.ShapeDtypeStruct((A_HEADS, LAT_PAD, n), BF16)]
        out_specs += [pl.BlockSpec((LAT_PAD, tm), col),
                      pl.BlockSpec((A_HEADS, LAT_PAD, tm), lambda i: (0, 0, i))]
    else:
        out_shape.append(jax.ShapeDtypeStruct((A_HEADS, n, LAT_PAD), BF16))
        out_specs.append(pl.BlockSpec((A_HEADS, tm, LAT_PAD), lambda i: (0, i, 0)))
    return pl.pallas_call(
        functools.partial(_proj_kernel, feature_major=feature_major),
        grid=(n // tm,),
        in_specs=[
            pl.BlockSpec((tm, x.shape[1]), row),
            pl.BlockSpec((tm, LANES), row),
            _const_spec(lw['norm_mix'].shape),
            _const_spec(lw['w_in'].shape),
            _const_spec(lw['gate_bias'].shape),
            _const_spec(lw['mla_q_norm'].shape),
            _const_spec(lw['w_uq'].shape),
            _const_spec(lw['mla_kv_norm'].shape),
            _const_spec(lw['w_uk'].shape),
        ],
        out_specs=tuple(out_specs),
        out_shape=tuple(out_shape),
        compiler_params=_params(1),
        name="proj_prompt" if feature_major else "proj_sample",
    )(x, tab, lw['norm_mix'], lw['w_in'], lw['gate_bias'], lw['mla_q_norm'], lw['w_uq'],
      lw['mla_kv_norm'], lw['w_uk'])


def _seg_scan(x, op, fill, seg):
    n = x.shape[1]
    lane = lax.broadcasted_iota(jnp.int32, x.shape, 1)
    pos = lane if seg is None else lane % seg
    span = n if seg is None else seg
    s = 1
    while s < span:
        x = op(x, jnp.where(pos >= s, pltpu.roll(x, s, 1), fill))
        s *= 2
    return x


def _gate_scans(gates, seg):
    ig = pltpu.roll(gates, M_HEADS, 0)
    b = _seg_scan(gates, jnp.add, 0.0, seg)
    d = ig - b
    e = _seg_scan(d, jnp.maximum, -jnp.inf, seg)
    return b, d, e


def _gate_rows(gates, m_prev, seg):
    b, d, e = _gate_scans(gates, seg)
    return b, d, pltpu.roll(gates, M_HEADS, 0), b + jnp.maximum(m_prev, e)


def _to_cols(row_blocks, n_tok):
    pad = jnp.zeros((LANES - SUBLANES * len(row_blocks), n_tok), F32)
    return jnp.concatenate(list(row_blocks) + [pad], axis=0).T


def _head_out(hh, og, gain):
    hn = hh * lax.rsqrt(jnp.mean(hh * hh, axis=-1, keepdims=True) + EPS) * gain
    return (hn * _sigmoid(og)).astype(BF16)


def _mlstm_prompt_kernel(*refs, n_seq):
    km_ref, gscan_ref, gdiff_ref, dcol_ref, gain_ref = refs[:5]
    qt_refs = refs[5:5 + n_seq]
    vt_refs = refs[5 + n_seq:5 + 2 * n_seq]
    ot_refs = refs[5 + 2 * n_seq:5 + 3 * n_seq]
    hm_ref, s_ref, m_ref = refs[5 + 3 * n_seq:]
    L = km_ref.shape[1]

    @pl.when(pl.program_id(0) == 0)
    def _():
        s_ref[...] = jnp.zeros_like(s_ref)
        m_ref[...] = jnp.zeros_like(m_ref)

    ki = lax.broadcasted_iota(jnp.int32, (L, L), 0)
    qi = lax.broadcasted_iota(jnp.int32, (L, L), 1)
    causal = ki <= qi
    pad_rows = jnp.zeros((STATE_ROWS - M_DV - 1, L), F32)

    for sq in range(n_seq):
        band = slice(sq * SUBLANES, (sq + 1) * SUBLANES)
        scan = gscan_ref[band, :]
        diff = gdiff_ref[band, :]
        ks = [km_ref[sq, :, hd * M_DK:(hd + 1) * M_DK] for hd in range(M_HEADS)]
        qts = [qt_refs[sq][hd * M_DK:(hd + 1) * M_DK, :] for hd in range(M_HEADS)]
        states = [s_ref[sq, hd] for hd in range(M_HEADS)]
        kq = [_dot(ks[hd], qts[hd]) for hd in range(M_HEADS)]
        q_state = [_dot(states[hd].astype(BF16), qts[hd]) for hd in range(M_HEADS)]

        for hd in range(M_HEADS):
            vt = vt_refs[sq][hd * M_DV:(hd + 1) * M_DV, :]
            b_r = scan[hd:hd + 1, :]
            e_r = scan[M_HEADS + hd:M_HEADS + hd + 1, :]
            d_r = diff[hd:hd + 1, :]
            d_c = dcol_ref[sq, :, hd:hd + 1]
            mp = m_ref[sq, hd:hd + 1, 0:1]
            mx_r = jnp.maximum(mp, e_r)
            mx_last = mx_r[:, L - 1:L]

            st = kq[hd] * jnp.exp(jnp.where(causal, d_c - mx_r, -jnp.inf))
            a_r = jnp.exp(mp - mx_r)
            num = _dot(vt, st.astype(BF16)) + a_r * q_state[hd][0:M_DV, :]
            nq = jnp.sum(st, axis=0, keepdims=True) + a_r * q_state[hd][M_DV:M_DV + 1, :]
            inv = 1.0 / jnp.maximum(jnp.abs(nq), jnp.exp(-(b_r + mx_r)))
            rs = lax.rsqrt(inv * inv * jnp.mean(num * num, axis=0, keepdims=True) + EPS)
            ht = num * (inv * rs) * gain_ref[:, hd:hd + 1] * _sigmoid(ot_refs[sq][hd * M_DV:(hd + 1) * M_DV, :])
            hm_ref[sq, :, hd * M_DV:(hd + 1) * M_DV] = ht.T.astype(BF16)

            w_r = jnp.exp(d_r - mx_last)
            lhs = jnp.concatenate([vt.astype(F32) * w_r, w_r, pad_rows], axis=0).astype(BF16)
            s_ref[sq, hd] = jnp.exp(mp - mx_last) * states[hd] + _dot(lhs, ks[hd])
            m_ref[sq, hd:hd + 1, :] = jnp.broadcast_to(b_r[:, L - 1:L] + mx_last, (1, LANES))


def _mlstm_prompt(km, qt, vt, ot, gscan, gdiff, dcol, gain_t, n_seq):
    t = km.shape[1]
    L = MLSTM_CHUNK
    nc = t // L
    tok = lambda c: (0, c, 0)
    whole4 = lambda c: (0, 0, 0, 0)
    whole3 = lambda c: (0, 0, 0)

    def seq_cols(rows):
        return [pl.BlockSpec((rows, L), lambda c, sq=sq: (0, sq * nc + c)) for sq in range(n_seq)]

    return pl.pallas_call(
        functools.partial(_mlstm_prompt_kernel, n_seq=n_seq),
        grid=(nc,),
        in_specs=[
            pl.BlockSpec((n_seq, L, 256), tok),
            pl.BlockSpec((n_seq * SUBLANES, L), lambda c: (0, c)),
            pl.BlockSpec((n_seq * SUBLANES, L), lambda c: (0, c)),
            pl.BlockSpec((n_seq, L, LANES), tok),
            _const_spec(gain_t.shape),
        ] + seq_cols(256) + seq_cols(512) + seq_cols(512),
        out_specs=(
            pl.BlockSpec((n_seq, L, 512), tok),
            pl.BlockSpec((n_seq, M_HEADS, STATE_ROWS, M_DK), whole4),
            pl.BlockSpec((n_seq, SUBLANES, LANES), whole3),
        ),
        out_shape=(
            jax.ShapeDtypeStruct((n_seq, t, 512), BF16),
            jax.ShapeDtypeStruct((n_seq, M_HEADS, STATE_ROWS, M_DK), F32),
            jax.ShapeDtypeStruct((n_seq, SUBLANES, LANES), F32),
        ),
        compiler_params=_params(1),
        name="mlstm_prompt",
    )(km, gscan, gdiff, dcol, gain_t, *([qt] * n_seq), *([vt] * n_seq), *([ot] * n_seq))


def _exact_onehot_dot(onehot, x):
    hi = x.astype(BF16)
    r1 = x - hi.astype(F32)
    mid = r1.astype(BF16)
    lo = (r1 - mid.astype(F32)).astype(BF16)
    return (_dot(onehot, hi) + _dot(onehot, mid)) + _dot(onehot, lo)


def _mlstm_sample_kernel(qk_ref, v_ref, o_ref, g_ref, mp_ref, gain_ref, c0_ref, n0_ref,
                         hm_ref, c_ref, n_ref, m_ref, cols_ref, *, t_seq):
    L = qk_ref.shape[0]
    G = L // t_seq
    mp_rows = jnp.concatenate([mp_ref[...], jnp.zeros((SUBLANES - M_HEADS, L), F32)], axis=0)
    b, d, ig, m = _gate_rows(g_ref[...], mp_rows, t_seq)
    m_ref[...] = m[0:M_HEADS, :]

    lane = lax.broadcasted_iota(jnp.int32, b.shape, 1)
    pos = lane % t_seq

    def seg_last(x):
        s = 1
        while s < t_seq:
            x = jnp.where(pos >= t_seq - s, x, pltpu.roll(x, L - s, 1))
            s *= 2
        return x

    b_last = seg_last(b)
    m_new = seg_last(m)
    w_rows = jnp.exp(b_last - b + ig - m_new)
    decay_rows = jnp.exp(b_last + mp_rows - m_new)
    cols_ref[...] = _to_cols((b, m, mp_rows, w_rows, decay_rows), L)
    cols = cols_ref[...]
    seq_cols = cols_ref[pl.ds(0, G, stride=t_seq), :]

    qi = lax.broadcasted_iota(jnp.int32, (L, L), 0)
    ki = lax.broadcasted_iota(jnp.int32, (L, L), 1)
    mask = (ki <= qi) & (ki // t_seq == qi // t_seq)
    er = lax.broadcasted_iota(jnp.int32, (L, G * M_DK), 0)
    ec = lax.broadcasted_iota(jnp.int32, (L, G * M_DK), 1)
    own = (ec // M_DK) == (er // t_seq)
    tok_of_seq = (lax.broadcasted_iota(jnp.int32, (L, G), 0) // t_seq
                  == lax.broadcasted_iota(jnp.int32, (L, G), 1)).astype(BF16)
    seq_of_tok = (lax.broadcasted_iota(jnp.int32, (G, L), 1) // t_seq
                  == lax.broadcasted_iota(jnp.int32, (G, L), 0)).astype(BF16)

    def expand(x):
        x2 = jnp.concatenate([x, x], axis=1)
        return jnp.where(own, jnp.tile(x2, (1, G // 2)), 0.0)

    for hd in range(M_HEADS):
        q = qk_ref[:, hd * M_DK:(hd + 1) * M_DK]
        k = qk_ref[:, 256 + hd * M_DK:256 + (hd + 1) * M_DK]
        v = v_ref[:, hd * M_DV:(hd + 1) * M_DV]
        b_c = cols[:, hd:hd + 1]
        m_c = cols[:, 8 + hd:9 + hd]
        mp_c = cols[:, 16 + hd:17 + hd]
        w_c = cols[:, 24 + hd:25 + hd]
        decay_seq = seq_cols[:, 32 + hd:33 + hd]
        n_old = n0_ref[hd]
        c_old = c0_ref[:, hd].reshape(G * M_DK, M_DV)

        p = jnp.exp(jnp.where(mask, b_c + d[hd:hd + 1, :] - m_c, -jnp.inf))
        s = _dot_nt(q, k) * p
        a_c = jnp.exp(b_c + mp_c - m_c)
        qf = q.astype(F32)
        num = _dot(s.astype(BF16), v) + a_c * _dot(expand(qf).astype(BF16), c_old.astype(BF16))
        n_tok = _exact_onehot_dot(tok_of_seq, n_old)
        nq = jnp.sum(s, axis=1, keepdims=True) + a_c * jnp.sum(qf * n_tok, axis=1, keepdims=True)
        den = jnp.maximum(jnp.abs(nq), jnp.exp(-m_c))
        sl = slice(hd * M_DV, (hd + 1) * M_DV)
        hm_ref[:, sl] = _head_out(num / den, o_ref[:, sl], gain_ref[hd:hd + 1, :])

        kw = k.astype(F32) * w_c
        upd = _dot(expand(kw).T.astype(BF16), v)
        for g in range(G):
            c_ref[g, hd] = decay_seq[g:g + 1, :] * c0_ref[g, hd] + upd[g * M_DK:(g + 1) * M_DK, :]
        n_ref[hd] = decay_seq * n_old + _dot(seq_of_tok, kw.astype(BF16))


def _mlstm_sample(qk, v, o, gates, m_prev_rows, gain, c0, n0_t, layer, t_seq):
    n = qk.shape[0]
    G = SAMPLE_GROUP
    L = G * t_seq
    n_seq = n // t_seq
    tok = lambda i: (i, 0)
    return pl.pallas_call(
        functools.partial(_mlstm_sample_kernel, t_seq=t_seq),
        grid=(n // L,),
        in_specs=[
            pl.BlockSpec((L, 512), tok),
            pl.BlockSpec((L, 512), tok),
            pl.BlockSpec((L, 512), tok),
            pl.BlockSpec((SUBLANES, L), lambda i: (0, i)),
            pl.BlockSpec((M_HEADS, L), lambda i: (0, i)),
            _const_spec(gain.shape),
            pl.BlockSpec((None, G, M_HEADS, M_DK, M_DV), lambda i: (layer, i, 0, 0, 0)),
            pl.BlockSpec((M_HEADS, G, M_DK), lambda i: (0, i, 0)),
        ],
        out_specs=(
            pl.BlockSpec((L, 512), tok),
            pl.BlockSpec((G, M_HEADS, M_DK, M_DV), lambda i: (i, 0, 0, 0)),
            pl.BlockSpec((M_HEADS, G, M_DK), lambda i: (0, i, 0)),
            pl.BlockSpec((M_HEADS, L), lambda i: (0, i)),
        ),
        out_shape=(
            jax.ShapeDtypeStruct((n, 512), BF16),
            jax.ShapeDtypeStruct((n_seq, M_HEADS, M_DK, M_DV), F32),
            jax.ShapeDtypeStruct((M_HEADS, n_seq, M_DK), F32),
            jax.ShapeDtypeStruct((M_HEADS, n), F32),
        ),
        scratch_shapes=[pltpu.VMEM((L, LANES), F32)],
        compiler_params=_params(1),
        name="mlstm_sample",
    )(qk, v, o, gates, m_prev_rows, gain, c0, n0_t)


def _attn_prompt_kernel(qt_ref, k_ref, kt_ref, o_ref, m_scr, l_scr, acc_scr):
    tq = qt_ref.shape[2]
    qb = pl.program_id(1)
    m_scr[...] = jnp.full_like(m_scr, -jnp.inf)
    l_scr[...] = jnp.zeros_like(l_scr)
    acc_scr[...] = jnp.zeros_like(acc_scr)

    def block(kb, masked):
        start = pl.multiple_of(kb * tq, tq)
        kblk = k_ref[pl.ds(start, tq), :]
        vt = kt_ref[0:KV_RANK, pl.ds(start, tq)]
        scores = {0: _dot(kblk, qt_ref[0])}
        for hd in range(A_HEADS):
            if hd + 1 < A_HEADS:
                scores[hd + 1] = _dot(kblk, qt_ref[hd + 1])
            st = scores.pop(hd)
            if masked:
                kpos = lax.broadcasted_iota(jnp.int32, st.shape, 0)
                qpos = lax.broadcasted_iota(jnp.int32, st.shape, 1)
                st = jnp.where(kpos <= qpos, st, -jnp.inf)
            m_old = m_scr[hd:hd + 1, :]
            m_new = jnp.maximum(m_old, jnp.max(st, axis=0, keepdims=True))
            alpha = jnp.exp2(m_old - m_new)
            p = jnp.exp2(st - m_new)
            l_scr[hd:hd + 1, :] = alpha * l_scr[hd:hd + 1, :] + jnp.sum(p, axis=0, keepdims=True)
            acc_scr[hd] = alpha * acc_scr[hd] + _dot(vt, p.astype(BF16))
            m_scr[hd:hd + 1, :] = m_new

    def body(kb, carry):
        block(kb, False)
        return carry

    lax.fori_loop(0, qb, body, 0)
    block(qb, True)
    for hd in range(A_HEADS):
        o_ref[hd] = (acc_scr[hd] / l_scr[hd:hd + 1, :]).T.astype(BF16)


def _attn_prompt(qatt_t, katt, katt_t, n_seq):
    n = katt.shape[0]
    t = n // n_seq
    tq = ATT_BLOCK
    nq = t // tq
    return pl.pallas_call(
        _attn_prompt_kernel,
        grid=(n_seq, nq),
        in_specs=[
            pl.BlockSpec((A_HEADS, LAT_PAD, tq), lambda b, i: (0, 0, b * nq + i)),
            pl.BlockSpec((t, LAT_PAD), lambda b, i: (b, 0)),
            pl.BlockSpec((LAT_PAD, t), lambda b, i: (0, b)),
        ],
        out_specs=pl.BlockSpec((A_HEADS, tq, KV_RANK), lambda b, i: (0, b * nq + i, 0)),
        out_shape=jax.ShapeDtypeStruct((A_HEADS, n, KV_RANK), BF16),
        scratch_shapes=[pltpu.VMEM((SUBLANES, tq), F32), pltpu.VMEM((SUBLANES, tq), F32),
                        pltpu.VMEM((A_HEADS, KV_RANK, tq), F32)],
        compiler_params=_params(2),
        name="attn_prompt",
    )(qatt_t, katt, katt_t)


def _attn_sample_kernel(pt_ref, q_ref, knew_ref, cache_ref, o_ref, pages, kbuf, sems, *, n_pages, t_seq, layer):
    s = pl.program_id(0)
    last = pl.num_programs(0) - 1
    slot = s % 2
    nxt = jnp.minimum(s + 1, last)
    page = pages.shape[3]

    def page_copy(seq, j, half):
        return pltpu.make_async_copy(cache_ref.at[layer, pt_ref[seq, j]], pages.at[half, j], sems.at[half])

    @pl.when(s == 0)
    def _():
        for j in range(n_pages):
            page_copy(0, j, 0).start()

    for j in range(n_pages):
        page_copy(s, j, slot).wait()
    for j in range(n_pages):
        page_copy(nxt, j, 1 - slot).start()
    for j in range(n_pages):
        kbuf[:, j * page:(j + 1) * page] = pages[slot, j].astype(BF16)

    q = q_ref[...]
    rows = q.shape[0]
    s_past = _dot(q[:, 0:LAT_DIM], kbuf[...])
    qf = q.astype(F32)
    knew = knew_ref[...].astype(F32)
    tq = lax.broadcasted_iota(jnp.int32, (rows, 1), 0) % t_seq
    s_new = []
    for j in range(t_seq):
        sj = jnp.sum(qf * knew[j:j + 1, :], axis=1, keepdims=True)
        s_new.append(jnp.where(tq >= j, sj, -jnp.inf))
    m = jnp.max(s_past, axis=1, keepdims=True)
    for sj in s_new:
        m = jnp.maximum(m, sj)
    p_past = jnp.exp2(s_past - m)
    p_new = [jnp.exp2(sj - m) for sj in s_new]
    denom = jnp.sum(p_past, axis=1, keepdims=True)
    for pj in p_new:
        denom = denom + pj
    inv = 1.0 / denom
    out = _dot_nt((p_past * inv).astype(BF16), kbuf[0:KV_RANK, :])
    for j in range(t_seq):
        pj = (p_new[j] * inv).astype(BF16).astype(F32)
        out = out + pj * knew[j:j + 1, 0:KV_RANK]
    o_ref[...] = out.astype(BF16)

    @pl.when(s == last)
    def _():
        for j in range(n_pages):
            page_copy(nxt, j, 1 - slot).wait()


def _attn_sample(q_seq, knew_seq, cache_t, page_table, layer):
    n_seq, rows, _ = q_seq.shape
    t_seq = knew_seq.shape[1]
    n_pages = page_table.shape[1]
    page = cache_t.shape[3]
    grid_spec = pltpu.PrefetchScalarGridSpec(
        num_scalar_prefetch=1,
        grid=(n_seq,),
        in_specs=[pl.BlockSpec((None, rows, LAT_PAD), lambda s, pt: (s, 0, 0)),
                  pl.BlockSpec((None, t_seq, LAT_PAD), lambda s, pt: (s, 0, 0)),
                  pl.BlockSpec(memory_space=pl.ANY)],
        out_specs=pl.BlockSpec((None, rows, KV_RANK), lambda s, pt: (s, 0, 0)),
        scratch_shapes=[pltpu.VMEM((2, n_pages, LAT_DIM, page), F32),
                        pltpu.VMEM((LAT_DIM, n_pages * page), BF16),
                        pltpu.SemaphoreType.DMA((2,))],
    )
    return pl.pallas_call(
        functools.partial(_attn_sample_kernel, n_pages=n_pages, t_seq=t_seq, layer=layer),
        grid_spec=grid_spec,
        out_shape=jax.ShapeDtypeStruct((n_seq, rows, KV_RANK), BF16),
        compiler_params=_params(1),
        name="attn_sample",
    )(page_table, q_seq, knew_seq, cache_t)


def _ffn_kernel(*refs, is_sample, is_last, t_seq, tiles_per_seq):
    (x_ref, hm_ref, ol_ref, p_ref, wuv_ref, wout_ref, gffn_ref, wup_ref, wc_ref, cb_ref,
     wdown_ref, gple_ref, wpg_ref, wpp_ref) = refs[:14]
    refs = refs[14:]
    if is_last:
        gfin_ref, refs = refs[0], refs[1:]
    if is_sample:
        s1_ref, s2_ref, x_out, u_out = refs
    else:
        x_out, tail_ref = refs
    tm = x_ref.shape[0]
    d_ff = wdown_ref.shape[0]

    ha = [_dot(ol_ref[hd], wuv_ref[hd]).astype(BF16) for hd in range(A_HEADS)]
    mix = jnp.concatenate([hm_ref[...]] + ha, axis=1)
    x = x_ref[...] + _dot(mix, wout_ref[...])

    hf = _rms(x, gffn_ref[...]).astype(BF16)
    row = lax.broadcasted_iota(jnp.int32, (tm, d_ff), 0)
    if not is_sample:
        @pl.when(pl.program_id(0) % tiles_per_seq == 0)
        def _():
            tail_ref[...] = jnp.zeros_like(tail_ref)

    def conv(cols):
        u = _dot(hf, wup_ref[:, cols])
        r1 = pltpu.roll(u, 1, 0)
        r2 = pltpu.roll(u, 2, 0)
        if is_sample:
            u_out[:, cols] = u
            pos = row % t_seq
            u1 = jnp.where(pos >= 1, r1, s1_ref[:, cols])
            u2 = jnp.where(pos >= 2, r2, s2_ref[:, cols])
        else:
            prev = tail_ref[:, cols]
            last, last2 = prev[SUBLANES - 1:SUBLANES, :], prev[SUBLANES - 2:SUBLANES - 1, :]
            u1 = jnp.where(row >= 1, r1, last)
            u2 = jnp.where(row >= 2, r2, jnp.where(row == 1, last, last2))
            tail_ref[:, cols] = u[tm - SUBLANES:tm, :]
        return (u2 * wc_ref[0:1, cols] + u1 * wc_ref[1:2, cols]) + u * wc_ref[2:3, cols] + cb_ref[:, cols]

    gate = conv(slice(0, d_ff))
    up = conv(slice(d_ff, 2 * d_ff))
    x = x + _dot((gate * _sigmoid(gate) * up).astype(BF16), wdown_ref[...])

    hp = _rms(x, gple_ref[...]).astype(BF16)
    x = x + _sigmoid(_dot(hp, wpg_ref[...])) * _dot(p_ref[...].astype(BF16), wpp_ref[...])
    x_out[...] = _rms(x, gfin_ref[...]) if is_last else x


def _ffn(x, hm, olat, p_in, lw, *, final_gain, conv_state, n_seq, t_seq):
    n, d = x.shape
    d_up = lw['w_up'].shape[1]
    is_sample = conv_state is not None
    tm = SAMPLE_FFN_TILE if is_sample else TOK_TILE
    is_last = final_gain is not None
    row = lambda i: (i, 0)
    weights = [lw['w_uv'], lw['w_out'], lw['norm_ffn'], lw['w_up'], lw['ffn_conv'], lw['ffn_conv_b'],
               lw['w_down'], lw['ple_norm'], lw['w_pg'], lw['w_pp']]
    if is_last:
        weights.append(final_gain)
    args = [x, hm, olat, p_in] + weights
    in_specs = [pl.BlockSpec((tm, d), row), pl.BlockSpec((tm, 512), row),
                pl.BlockSpec((A_HEADS, tm, KV_RANK), lambda i: (0, i, 0)),
                pl.BlockSpec((tm, p_in.shape[1]), row)] + [_const_spec(w.shape) for w in weights]
    out_shape = [jax.ShapeDtypeStruct((n, d), F32)]
    out_specs = [pl.BlockSpec((tm, d), row)]
    if is_sample:
        args += list(conv_state)
        in_specs += [pl.BlockSpec((tm, d_up), row)] * 2
        out_shape.append(jax.ShapeDtypeStruct((n, d_up), F32))
        out_specs.append(pl.BlockSpec((tm, d_up), row))
        tiles_per_seq = 1
    else:
        tiles_per_seq = n // n_seq // tm
        out_shape.append(jax.ShapeDtypeStruct((n_seq, SUBLANES, d_up), F32))
        out_specs.append(pl.BlockSpec((None, SUBLANES, d_up), lambda i: (i // tiles_per_seq, 0, 0)))
    return pl.pallas_call(
        functools.partial(_ffn_kernel, is_sample=is_sample, is_last=is_last, t_seq=t_seq,
                          tiles_per_seq=tiles_per_seq),
        grid=(n // tm,),
        in_specs=in_specs,
        out_specs=tuple(out_specs),
        out_shape=tuple(out_shape),
        compiler_params=_params(1),
        name="ffn_sample" if is_sample else "ffn_prompt",
    )(*args)


def _swap_halves(w):
    half = w.shape[-1] // 2
    return jnp.concatenate([w[..., half:], w[..., :half]], axis=-1)


def _layer_weights(l, norm_mix, w_in, m_gate_bias, m_norm, mla_q_norm, mla_w_uq, mla_kv_norm, mla_w_uk,
                   mla_w_uv, w_out, norm_ffn, ffn_w_up, ffn_conv, ffn_conv_b, ffn_w_down, ple_norm,
                   ple_w_gate, ple_w_proj):
    d = w_in.shape[1]
    wi = w_in[l]
    sizes = (256, 256, 512, 512, M_HEADS, M_HEADS, Q_RANK, KV_RANK, A_ROPE)
    offs = [0]
    for s in sizes:
        offs.append(offs[-1] + s)
    qm, km, vm, om, im, fm, cq, ckv, kr = (wi[:, offs[i]:offs[i + 1]] for i in range(len(sizes)))
    gate_pad = jnp.zeros((d, LANES - 2 * M_HEADS), wi.dtype)
    w_in_r = jnp.concatenate([fm, im, gate_pad, qm, km, vm, om, cq, ckv, kr, _swap_halves(kr)], axis=1)
    uq = mla_w_uq[l].reshape(Q_RANK, A_HEADS, A_NOPE + A_ROPE)
    uq_rope = uq[:, :, A_NOPE:]
    w_uq_r = jnp.concatenate([uq[:, :, :A_NOPE].reshape(Q_RANK, A_HEADS * A_NOPE),
                              jnp.concatenate([uq_rope, _swap_halves(uq_rope)], axis=-1)
                              .reshape(Q_RANK, A_HEADS * LANES)], axis=1)
    row = lambda v: v.reshape(1, -1)
    return {
        'norm_mix': row(norm_mix[l]),
        'w_in': w_in_r.astype(BF16),
        'gate_bias': jnp.concatenate([m_gate_bias[l, 1], m_gate_bias[l, 0]]).reshape(SUBLANES, 1),
        'm_norm': m_norm[l],
        'mla_q_norm': row(mla_q_norm[l]),
        'w_uq': w_uq_r.astype(BF16),
        'mla_kv_norm': row(mla_kv_norm[l]),
        'w_uk': jnp.transpose(mla_w_uk[l], (1, 2, 0)).astype(BF16),
        'w_uv': jnp.transpose(mla_w_uv[l], (1, 0, 2)).astype(BF16),
        'w_out': w_out[l].astype(BF16),
        'norm_ffn': row(norm_ffn[l]),
        'w_up': ffn_w_up[l].astype(BF16),
        'ffn_conv': ffn_conv[l],
        'ffn_conv_b': row(ffn_conv_b[l]),
        'w_down': ffn_w_down[l].astype(BF16),
        'ple_norm': row(ple_norm[l]),
        'w_pg': ple_w_gate[l].astype(BF16),
        'w_pp': ple_w_proj[l].astype(BF16),
    }


def _rope_table(pos):
    half = A_ROPE // 2
    freqs = ROPE_THETA ** (-jnp.arange(half, dtype=F32) * 2.0 / A_ROPE)
    ang = pos.astype(F32)[:, None] * freqs[None, :]
    cos, sin = jnp.cos(ang), jnp.sin(ang)
    return jnp.concatenate([cos, cos, -sin, sin], axis=1)


def kernel(x_prompt, x_sample, p_prompt, p_sample, cache_mla, state_mlstm_C, state_mlstm_n, state_mlstm_m,
           state_ffn_conv, page_table, norm_mix, w_in, m_gate_bias, m_norm, mla_q_norm, mla_w_uq, mla_kv_norm,
           mla_w_uk, mla_w_uv, w_out, norm_ffn, ffn_w_up, ffn_conv, ffn_conv_b, ffn_w_down, ple_norm,
           ple_w_gate, ple_w_proj, final_norm):
    depth = w_in.shape[0]
    bp, tp, d = x_prompt.shape
    bs, ts, _ = x_sample.shape
    n_p, n_s = bp * tp, bs * ts
    past_len = page_table.shape[1] * cache_mla.shape[2]
    d_up = ffn_w_up.shape[2]

    tab_p = jnp.tile(_rope_table(jnp.arange(tp)), (bp, 1))
    tab_s = jnp.tile(_rope_table(past_len + jnp.arange(ts)), (bs, 1))
    xp = x_prompt.reshape(n_p, d)
    xs = x_sample.reshape(n_s, d)
    final_gain = final_norm.reshape(1, d)
    cache_t = jnp.swapaxes(cache_mla, 2, 3)

    lat_p, lat_s, c_p, n_pl, m_p, c_s, n_sl, m_s, conv_p, conv_s = ([] for _ in range(10))
    for l in range(depth):
        lw = _layer_weights(l, norm_mix, w_in, m_gate_bias, m_norm, mla_q_norm, mla_w_uq, mla_kv_norm,
                            mla_w_uk, mla_w_uv, w_out, norm_ffn, ffn_w_up, ffn_conv, ffn_conv_b,
                            ffn_w_down, ple_norm, ple_w_gate, ple_w_proj)
        fin = final_gain if l == depth - 1 else None

        km, qt, vt, ot, gscan, gdiff, dcol, lat, katt, katt_t, qatt_t = _proj(xp, tab_p, lw, True, bp)
        by_seq = lambda a: a.reshape(bp, tp, a.shape[-1])
        hm, state, m_new = _mlstm_prompt(by_seq(km), qt, vt, ot, gscan, gdiff, by_seq(dcol), lw['m_norm'].T, bp)
        c_new = jnp.swapaxes(state[:, :, :M_DV, :], 2, 3)
        n_new = state[:, :, M_DV, :]
        olat = _attn_prompt(qatt_t, katt, katt_t, bp)
        xp, tail = _ffn(xp, hm.reshape(n_p, -1), olat, p_prompt[l].reshape(n_p, -1), lw, final_gain=fin,
                        conv_state=None, n_seq=bp, t_seq=tp)
        lat_p.append(lat.reshape(bp, tp, LAT_DIM))
        c_p.append(c_new)
        n_pl.append(n_new)
        m_p.append(m_new[:, :M_HEADS, 0])
        conv_p.append(tail[:, SUBLANES - (CONV_W - 1):])

        qk, v, o, gates, lat, katt, qatt = _proj(xs, tab_s, lw, False, 1)
        m_prev_rows = jnp.repeat(state_mlstm_m[l].T, ts, axis=1)
        hm, c_new, n_new_t, m_rows = _mlstm_sample(qk, v, o, gates, m_prev_rows, lw['m_norm'],
                                                   state_mlstm_C, jnp.transpose(state_mlstm_n[l], (1, 0, 2)),
                                                   l, ts)
        q_seq = jnp.transpose(qatt.reshape(A_HEADS, bs, ts, LAT_PAD), (1, 0, 2, 3)).reshape(bs, A_HEADS * ts, LAT_PAD)
        o_seq = _attn_sample(q_seq, katt.reshape(bs, ts, LAT_PAD), cache_t, page_table, l)
        olat = jnp.transpose(o_seq.reshape(bs, A_HEADS, ts, KV_RANK), (1, 0, 2, 3)).reshape(A_HEADS, n_s, KV_RANK)
        buf0 = state_ffn_conv[l]
        zero = jnp.zeros((bs, d_up), F32)
        s1 = jnp.stack([buf0[:, 1]] + [zero] * (ts - 1), axis=1).reshape(n_s, d_up)
        s2 = jnp.stack([buf0[:, 0], buf0[:, 1]] + [zero] * (ts - 2), axis=1).reshape(n_s, d_up)
        xs, u_s = _ffn(xs, hm, olat, p_sample[l].reshape(n_s, -1), lw, final_gain=fin,
                       conv_state=(s1, s2), n_seq=bs, t_seq=ts)
        lat_s.append(lat.reshape(bs, ts, LAT_DIM))
        c_s.append(c_new)
        n_sl.append(jnp.transpose(n_new_t, (1, 0, 2)))
        m_s.append(m_rows[:, ts - 1::ts].T)
        conv_s.append(u_s.reshape(bs, ts, d_up)[:, ts - (CONV_W - 1):])

    return (xp.reshape(bp, tp, d), xs.reshape(bs, ts, d), jnp.stack(lat_p), jnp.stack(lat_s),
            jnp.stack(c_p), jnp.stack(n_pl), jnp.stack(m_p), jnp.stack(c_s), jnp.stack(n_sl), jnp.stack(m_s),
            jnp.stack(conv_p), jnp.stack(conv_s))
```

```python
import functools

import jax
import jax.numpy as jnp
from jax import lax
from jax.experimental import pallas as pl
from jax.experimental.pallas import tpu as pltpu

F32 = jnp.float32
BF16 = jnp.bfloat16

M_HEADS = 4
M_DK = 64
M_DV = 128
A_HEADS = 4
A_NOPE = 128
A_ROPE = 64
KV_RANK = 256
Q_RANK = 384
LAT_DIM = KV_RANK + A_ROPE
GATE_CAP = 15.0
ROPE_THETA = 10000.0
EPS = 1e-6
CONV_W = 3
ATT_SCALE = (A_NOPE + A_ROPE) ** -0.5
LOG2_E = 1.4426950408889634
Q_SCALE = ATT_SCALE * LOG2_E

LANES = 128
SUBLANES = 8
VMEM_LIMIT_BYTES = 56 * 1024 * 1024

LAT_PAD = 3 * LANES
TOK_TILE = 256
MLSTM_CHUNK = 256
ATT_BLOCK = 512
SAMPLE_GROUP = 32
SAMPLE_FFN_TILE = 128
STATE_ROWS = M_DV + SUBLANES

C_G, C_CQ, C_CKV, C_KR = 0, 128, 512, 768
C_V, C_O, C_Q, C_K = 896, 1408, 1920, 2176
IN_COLS = 2432


def _const_spec(shape):
    nd = len(shape)
    return pl.BlockSpec(shape, lambda *_: (0,) * nd, pipeline_mode=pl.Buffered(1))


def _params(n_axes):
    return pltpu.CompilerParams(dimension_semantics=("arbitrary",) * n_axes,
                                vmem_limit_bytes=VMEM_LIMIT_BYTES)


def _rms(x, g):
    return x * lax.rsqrt(jnp.mean(x * x, axis=-1, keepdims=True) + EPS) * g


def _sigmoid(x):
    return 1.0 / (1.0 + jnp.exp(-x))


def _dot(a, b):
    return jnp.dot(a, b, preferred_element_type=F32)


def _dot_nt(a, b):
    return lax.dot_general(a, b, (((1,), (1,)), ((), ())), preferred_element_type=F32)


def _fold_halves(p):
    return p + pltpu.roll(p, 64, 1)


def _proj_kernel(x_ref, tab_ref, gmix_ref, win_ref, gb_ref, gq_ref, wuq_ref, gkv_ref, wuk_ref,
                 *refs, feature_major):
    if feature_major:
        (wvoq_ref, km_ref, qt_ref, vt_ref, ot_ref, gscan_ref, gdiff_ref, dcol_ref,
         lat_ref, katt_ref, katt_t_ref, qatt_ref) = refs
    else:
        qk_ref, v_ref, o_ref, gates_ref, lat_ref, katt_ref, qatt_ref = refs
    h = _rms(x_ref[...], gmix_ref[...]).astype(BF16)
    z = _dot(h, win_ref[:, 0:C_V])

    gt = z[:, C_G:C_G + LANES].T[0:SUBLANES, :] + gb_ref[...]
    capped = GATE_CAP * jnp.tanh(gt / GATE_CAP)
    log_sig = jnp.minimum(capped, 0.0) - jnp.log1p(jnp.exp(-jnp.abs(capped)))
    row = lax.broadcasted_iota(jnp.int32, gt.shape, 0)
    gates = jnp.where(row < M_HEADS, log_sig, capped)
    if feature_major:
        b, d, e = _gate_scans(gates, None)
        gscan_ref[...] = jnp.where(row < M_HEADS, b, pltpu.roll(e, M_HEADS, 0))
        d = jnp.where(row < M_HEADS, d, 0.0)
        gdiff_ref[...] = d
        dcol_ref[...] = _to_cols((d,), d.shape[1])
    else:
        gates_ref[...] = gates

    tab = tab_ref[...]
    lane = lax.broadcasted_iota(jnp.int32, tab.shape, 1)
    ckv_n = _rms(z[:, C_CKV:C_CKV + KV_RANK], gkv_ref[...])
    rk = _fold_halves(z[:, C_KR:C_KR + LANES] * tab)
    lat_ref[:, 0:KV_RANK] = ckv_n
    lat_ref[:, KV_RANK:LAT_DIM] = rk[:, 0:A_ROPE]
    k_cat = jnp.concatenate([ckv_n, jnp.where(lane < A_ROPE, rk, 0.0)], axis=1)
    katt_ref[...] = k_cat.astype(BF16)
    if feature_major:
        katt_t_ref[...] = k_cat.T.astype(BF16)

    qn = _rms(z[:, C_CQ:C_CQ + Q_RANK], gq_ref[...]).astype(BF16)
    qa = _dot(qn, wuq_ref[...])
    for hd in range(A_HEADS):
        q_lat = _dot(qa[:, hd * A_NOPE:(hd + 1) * A_NOPE].astype(BF16), wuk_ref[hd])
        rq = _fold_halves(qa[:, 512 + hd * LANES:512 + (hd + 1) * LANES] * tab)
        q_cat = jnp.concatenate([q_lat, jnp.where(lane < A_ROPE, rq, 0.0)], axis=1) * Q_SCALE
        qatt_ref[hd] = (q_cat.T if feature_major else q_cat).astype(BF16)

    k_m = (_dot(h, win_ref[:, C_K:C_K + 256]) * (M_DK ** -0.5)).astype(BF16)
    if feature_major:
        zt = _dot_nt(wvoq_ref[...], h)
        vt_ref[...] = zt[0:512, :].astype(BF16)
        ot_ref[...] = zt[512:1024, :]
        qt_ref[...] = zt[1024:1280, :].astype(BF16)
        km_ref[...] = k_m
    else:
        z2 = _dot(h, win_ref[:, C_V:C_K])
        v_ref[...] = z2[:, 0:512].astype(BF16)
        o_ref[...] = z2[:, 512:1024]
        qk_ref[:, 0:256] = z2[:, 1024:1280].astype(BF16)
        qk_ref[:, 256:512] = k_m


def _proj(x, tab, lw, feature_major, gate_seqs):
    n = x.shape[0]
    tm = TOK_TILE
    tiles_per_seq = n // gate_seqs // tm
    row = lambda i: (i, 0)
    col = lambda i: (0, i)
    band = lambda i: (i // tiles_per_seq, i % tiles_per_seq)
    gate_rows = jax.ShapeDtypeStruct((gate_seqs * SUBLANES, n // gate_seqs), F32)
    if feature_major:
        assert tm == MLSTM_CHUNK
        out_shape = [
            jax.ShapeDtypeStruct((n, 256), BF16),
            jax.ShapeDtypeStruct((256, n), BF16),
            jax.ShapeDtypeStruct((512, n), BF16),
            jax.ShapeDtypeStruct((512, n), F32),
            gate_rows,
            gate_rows,
            jax.ShapeDtypeStruct((n, LANES), F32),
        ]
        out_specs = [
            pl.BlockSpec((tm, 256), row),
            pl.BlockSpec((256, tm), col),
            pl.BlockSpec((512, tm), col),
            pl.BlockSpec((512, tm), col),
            pl.BlockSpec((SUBLANES, tm), band),
            pl.BlockSpec((SUBLANES, tm), band),
            pl.BlockSpec((tm, LANES), row),
        ]
    else:
        out_shape = [
            jax.ShapeDtypeStruct((n, 512), BF16),
            jax.ShapeDtypeStruct((n, 512), BF16),
            jax.ShapeDtypeStruct((n, 512), F32),
            gate_rows,
        ]
        out_specs = [
            pl.BlockSpec((tm, 512), row),
            pl.BlockSpec((tm, 512), row),
            pl.BlockSpec((tm, 512), row),
            pl.BlockSpec((SUBLANES, tm), band),
        ]
    out_shape += [jax.ShapeDtypeStruct((n, LAT_DIM), F32),
                  jax.ShapeDtypeStruct((n, LAT_PAD), BF16)]
    out_specs += [pl.BlockSpec((tm, LAT_DIM), row), pl.BlockSpec((tm, LAT_PAD), row)]
    if feature_major:
        out_shape += [jax.ShapeDtypeStruct((LAT_PAD, n), BF16),
                      jax.ShapeDtypeStruct((A_HEADS, LAT_PAD, n), BF16)]
        out_specs += [pl.BlockSpec((LAT_PAD, tm), col),
                      pl.BlockSpec((A_HEADS, LAT_PAD, tm), lambda i: (0, 0, i))]
    else:
        out_shape.append(jax.ShapeDtypeStruct((A_HEADS, n, LAT_PAD), BF16))
        out_specs.append(pl.BlockSpec((A_HEADS, tm, LAT_PAD), lambda i: (0, i, 0)))
    return pl.pallas_call(
        functools.partial(_proj_kernel, feature_major=feature_major),
        grid=(n // tm,),
        in_specs=[
            pl.BlockSpec((tm, x.shape[1]), row),
            pl.BlockSpec((tm, LANES), row),
            _const_spec(lw['norm_mix'].shape),
            _const_spec(lw['w_in'].shape),
            _const_spec(lw['gate_bias'].shape),
            _const_spec(lw['mla_q_norm'].shape),
            _const_spec(lw['w_uq'].shape),
            _const_spec(lw['mla_kv_norm'].shape),
            _const_spec(lw['w_uk'].shape),
        ] + ([_const_spec(lw['w_voq_t'].shape)] if feature_major else []),
        out_specs=tuple(out_specs),
        out_shape=tuple(out_shape),
        compiler_params=_params(1),
        name="proj_prompt" if feature_major else "proj_sample",
    )(x, tab, lw['norm_mix'], lw['w_in'], lw['gate_bias'], lw['mla_q_norm'], lw['w_uq'],
      lw['mla_kv_norm'], lw['w_uk'], *([lw['w_voq_t']] if feature_major else []))


def _seg_scan(x, op, fill, seg):
    n = x.shape[1]
    lane = lax.broadcasted_iota(jnp.int32, x.shape, 1)
    pos = lane if seg is None else lane % seg
    span = n if seg is None else seg
    s = 1
    while s < span:
        x = op(x, jnp.where(pos >= s, pltpu.roll(x, s, 1), fill))
        s *= 2
    return x


def _gate_scans(gates, seg):
    ig = pltpu.roll(gates, M_HEADS, 0)
    b = _seg_scan(gates, jnp.add, 0.0, seg)
    d = ig - b
    e = _seg_scan(d, jnp.maximum, -jnp.inf, seg)
    return b, d, e


def _gate_rows(gates, m_prev, seg):
    b, d, e = _gate_scans(gates, seg)
    return b, d, pltpu.roll(gates, M_HEADS, 0), b + jnp.maximum(m_prev, e)


def _to_cols(row_blocks, n_tok):
    pad = jnp.zeros((LANES - SUBLANES * len(row_blocks), n_tok), F32)
    return jnp.concatenate(list(row_blocks) + [pad], axis=0).T


def _head_out(hh, og, gain):
    hn = hh * lax.rsqrt(jnp.mean(hh * hh, axis=-1, keepdims=True) + EPS) * gain
    return (hn * _sigmoid(og)).astype(BF16)


def _mlstm_prompt_kernel(*refs, n_seq):
    km_ref, gscan_ref, gdiff_ref, dcol_ref, gain_ref = refs[:5]
    qt_refs = refs[5:5 + n_seq]
    vt_refs = refs[5 + n_seq:5 + 2 * n_seq]
    ot_refs = refs[5 + 2 * n_seq:5 + 3 * n_seq]
    hm_ref, s_ref, m_ref = refs[5 + 3 * n_seq:]
    L = km_ref.shape[1]

    @pl.when(pl.program_id(0) == 0)
    def _():
        s_ref[...] = jnp.zeros_like(s_ref)
        m_ref[...] = jnp.zeros_like(m_ref)

    ki = lax.broadcasted_iota(jnp.int32, (L, L), 0)
    qi = lax.broadcasted_iota(jnp.int32, (L, L), 1)
    causal = ki <= qi
    pad_rows = jnp.zeros((STATE_ROWS - M_DV - 1, L), F32)

    for sq in range(n_seq):
        band = slice(sq * SUBLANES, (sq + 1) * SUBLANES)
        scan = gscan_ref[band, :]
        diff = gdiff_ref[band, :]
        ks = [km_ref[sq, :, hd * M_DK:(hd + 1) * M_DK] for hd in range(M_HEADS)]
        qts = [qt_refs[sq][hd * M_DK:(hd + 1) * M_DK, :] for hd in range(M_HEADS)]
        states = [s_ref[sq, hd] for hd in range(M_HEADS)]
        kq = [_dot(ks[hd], qts[hd]) for hd in range(M_HEADS)]
        q_state = [_dot(states[hd].astype(BF16), qts[hd]) for hd in range(M_HEADS)]

        for hd in range(M_HEADS):
            vt = vt_refs[sq][hd * M_DV:(hd + 1) * M_DV, :]
            b_r = scan[hd:hd + 1, :]
            e_r = scan[M_HEADS + hd:M_HEADS + hd + 1, :]
            d_r = diff[hd:hd + 1, :]
            d_c = dcol_ref[sq, :, hd:hd + 1]
            mp = m_ref[sq, hd:hd + 1, 0:1]
            mx_r = jnp.maximum(mp, e_r)
            mx_last = mx_r[:, L - 1:L]

            st = kq[hd] * jnp.exp(jnp.where(causal, d_c - mx_r, -jnp.inf))
            a_r = jnp.exp(mp - mx_r)
            num = _dot(vt, st.astype(BF16)) + a_r * q_state[hd][0:M_DV, :]
            nq = jnp.sum(st, axis=0, keepdims=True) + a_r * q_state[hd][M_DV:M_DV + 1, :]
            inv = 1.0 / jnp.maximum(jnp.abs(nq), jnp.exp(-(b_r + mx_r)))
            rs = lax.rsqrt(inv * inv * jnp.mean(num * num, axis=0, keepdims=True) + EPS)
            ht = num * (inv * rs) * gain_ref[:, hd:hd + 1] * _sigmoid(ot_refs[sq][hd * M_DV:(hd + 1) * M_DV, :])
            hm_ref[sq, :, hd * M_DV:(hd + 1) * M_DV] = ht.T.astype(BF16)

            w_r = jnp.exp(d_r - mx_last)
            lhs = jnp.concatenate([vt.astype(F32) * w_r, w_r, pad_rows], axis=0).astype(BF16)
            s_ref[sq, hd] = jnp.exp(mp - mx_last) * states[hd] + _dot(lhs, ks[hd])
            m_ref[sq, hd:hd + 1, :] = jnp.broadcast_to(b_r[:, L - 1:L] + mx_last, (1, LANES))


def _mlstm_prompt(km, qt, vt, ot, gscan, gdiff, dcol, gain_t, n_seq):
    t = km.shape[1]
    L = MLSTM_CHUNK
    nc = t // L
    tok = lambda c: (0, c, 0)
    whole4 = lambda c: (0, 0, 0, 0)
    whole3 = lambda c: (0, 0, 0)

    def seq_cols(rows):
        return [pl.BlockSpec((rows, L), lambda c, sq=sq: (0, sq * nc + c)) for sq in range(n_seq)]

    return pl.pallas_call(
        functools.partial(_mlstm_prompt_kernel, n_seq=n_seq),
        grid=(nc,),
        in_specs=[
            pl.BlockSpec((n_seq, L, 256), tok),
            pl.BlockSpec((n_seq * SUBLANES, L), lambda c: (0, c)),
            pl.BlockSpec((n_seq * SUBLANES, L), lambda c: (0, c)),
            pl.BlockSpec((n_seq, L, LANES), tok),
            _const_spec(gain_t.shape),
        ] + seq_cols(256) + seq_cols(512) + seq_cols(512),
        out_specs=(
            pl.BlockSpec((n_seq, L, 512), tok),
            pl.BlockSpec((n_seq, M_HEADS, STATE_ROWS, M_DK), whole4),
            pl.BlockSpec((n_seq, SUBLANES, LANES), whole3),
        ),
        out_shape=(
            jax.ShapeDtypeStruct((n_seq, t, 512), BF16),
            jax.ShapeDtypeStruct((n_seq, M_HEADS, STATE_ROWS, M_DK), F32),
            jax.ShapeDtypeStruct((n_seq, SUBLANES, LANES), F32),
        ),
        compiler_params=_params(1),
        name="mlstm_prompt",
    )(km, gscan, gdiff, dcol, gain_t, *([qt] * n_seq), *([vt] * n_seq), *([ot] * n_seq))


def _exact_onehot_dot(onehot, x):
    hi = x.astype(BF16)
    r1 = x - hi.astype(F32)
    mid = r1.astype(BF16)
    lo = (r1 - mid.astype(F32)).astype(BF16)
    return (_dot(onehot, hi) + _dot(onehot, mid)) + _dot(onehot, lo)


def _mlstm_sample_kernel(qk_ref, v_ref, o_ref, g_ref, mp_ref, gain_ref, c0_ref, n0_ref,
                         hm_ref, c_ref, n_ref, m_ref, cols_ref, *, t_seq):
    L = qk_ref.shape[0]
    G = L // t_seq
    mp_rows = jnp.concatenate([mp_ref[...], jnp.zeros((SUBLANES - M_HEADS, L), F32)], axis=0)
    b, d, ig, m = _gate_rows(g_ref[...], mp_rows, t_seq)
    m_ref[...] = m[0:M_HEADS, :]

    lane = lax.broadcasted_iota(jnp.int32, b.shape, 1)
    pos = lane % t_seq

    def seg_last(x):
        s = 1
        while s < t_seq:
            x = jnp.where(pos >= t_seq - s, x, pltpu.roll(x, L - s, 1))
            s *= 2
        return x

    b_last = seg_last(b)
    m_new = seg_last(m)
    w_rows = jnp.exp(b_last - b + ig - m_new)
    decay_rows = jnp.exp(b_last + mp_rows - m_new)
    cols_ref[...] = _to_cols((b, m, mp_rows, w_rows, decay_rows), L)
    cols = cols_ref[...]
    seq_cols = cols_ref[pl.ds(0, G, stride=t_seq), :]

    qi = lax.broadcasted_iota(jnp.int32, (L, L), 0)
    ki = lax.broadcasted_iota(jnp.int32, (L, L), 1)
    mask = (ki <= qi) & (ki // t_seq == qi // t_seq)
    er = lax.broadcasted_iota(jnp.int32, (L, G * M_DK), 0)
    ec = lax.broadcasted_iota(jnp.int32, (L, G * M_DK), 1)
    own = (ec // M_DK) == (er // t_seq)
    tok_of_seq = (lax.broadcasted_iota(jnp.int32, (L, G), 0) // t_seq
                  == lax.broadcasted_iota(jnp.int32, (L, G), 1)).astype(BF16)
    seq_of_tok = (lax.broadcasted_iota(jnp.int32, (G, L), 1) // t_seq
                  == lax.broadcasted_iota(jnp.int32, (G, L), 0)).astype(BF16)

    def expand(x):
        x2 = jnp.concatenate([x, x], axis=1)
        return jnp.where(own, jnp.tile(x2, (1, G // 2)), 0.0)

    for hd in range(M_HEADS):
        q = qk_ref[:, hd * M_DK:(hd + 1) * M_DK]
        k = qk_ref[:, 256 + hd * M_DK:256 + (hd + 1) * M_DK]
        v = v_ref[:, hd * M_DV:(hd + 1) * M_DV]
        b_c = cols[:, hd:hd + 1]
        m_c = cols[:, 8 + hd:9 + hd]
        mp_c = cols[:, 16 + hd:17 + hd]
        w_c = cols[:, 24 + hd:25 + hd]
        decay_seq = seq_cols[:, 32 + hd:33 + hd]
        n_old = n0_ref[hd]
        c_old = c0_ref[:, hd].reshape(G * M_DK, M_DV)

        p = jnp.exp(jnp.where(mask, b_c + d[hd:hd + 1, :] - m_c, -jnp.inf))
        s = _dot_nt(q, k) * p
        a_c = jnp.exp(b_c + mp_c - m_c)
        qf = q.astype(F32)
        num = _dot(s.astype(BF16), v) + a_c * _dot(expand(qf).astype(BF16), c_old.astype(BF16))
        n_tok = _exact_onehot_dot(tok_of_seq, n_old)
        nq = jnp.sum(s, axis=1, keepdims=True) + a_c * jnp.sum(qf * n_tok, axis=1, keepdims=True)
        den = jnp.maximum(jnp.abs(nq), jnp.exp(-m_c))
        sl = slice(hd * M_DV, (hd + 1) * M_DV)
        hm_ref[:, sl] = _head_out(num / den, o_ref[:, sl], gain_ref[hd:hd + 1, :])

        kw = k.astype(F32) * w_c
        upd = _dot(expand(kw).T.astype(BF16), v)
        for g in range(G):
            c_ref[g, hd] = decay_seq[g:g + 1, :] * c0_ref[g, hd] + upd[g * M_DK:(g + 1) * M_DK, :]
        n_ref[hd] = decay_seq * n_old + _dot(seq_of_tok, kw.astype(BF16))


def _mlstm_sample(qk, v, o, gates, m_prev_rows, gain, c0, n0_t, layer, t_seq):
    n = qk.shape[0]
    G = SAMPLE_GROUP
    L = G * t_seq
    n_seq = n // t_seq
    tok = lambda i: (i, 0)
    return pl.pallas_call(
        functools.partial(_mlstm_sample_kernel, t_seq=t_seq),
        grid=(n // L,),
        in_specs=[
            pl.BlockSpec((L, 512), tok),
            pl.BlockSpec((L, 512), tok),
            pl.BlockSpec((L, 512), tok),
            pl.BlockSpec((SUBLANES, L), lambda i: (0, i)),
            pl.BlockSpec((M_HEADS, L), lambda i: (0, i)),
            _const_spec(gain.shape),
            pl.BlockSpec((None, G, M_HEADS, M_DK, M_DV), lambda i: (layer, i, 0, 0, 0)),
            pl.BlockSpec((M_HEADS, G, M_DK), lambda i: (0, i, 0)),
        ],
        out_specs=(
            pl.BlockSpec((L, 512), tok),
            pl.BlockSpec((G, M_HEADS, M_DK, M_DV), lambda i: (i, 0, 0, 0)),
            pl.BlockSpec((M_HEADS, G, M_DK), lambda i: (0, i, 0)),
            pl.BlockSpec((M_HEADS, L), lambda i: (0, i)),
        ),
        out_shape=(
            jax.ShapeDtypeStruct((n, 512), BF16),
            jax.ShapeDtypeStruct((n_seq, M_HEADS, M_DK, M_DV), F32),
            jax.ShapeDtypeStruct((M_HEADS, n_seq, M_DK), F32),
            jax.ShapeDtypeStruct((M_HEADS, n), F32),
        ),
        scratch_shapes=[pltpu.VMEM((L, LANES), F32)],
        compiler_params=_params(1),
        name="mlstm_sample",
    )(qk, v, o, gates, m_prev_rows, gain, c0, n0_t)


def _attn_prompt_kernel(qt_ref, k_ref, kt_ref, o_ref, m_scr, l_scr, acc_scr, s0_scr):
    tq = qt_ref.shape[2]
    qb = pl.program_id(1)
    m_scr[...] = jnp.full_like(m_scr, -jnp.inf)
    l_scr[...] = jnp.zeros_like(l_scr)
    acc_scr[...] = jnp.zeros_like(acc_scr)

    def keys(kb):
        return k_ref[pl.ds(pl.multiple_of(kb * tq, tq), tq), :]

    s0_scr[...] = _dot(keys(0), qt_ref[0])

    def block(kb, last):
        kblk = keys(kb)
        vt = kt_ref[0:KV_RANK, pl.ds(pl.multiple_of(kb * tq, tq), tq)]
        scores = {0: s0_scr[...]}
        s0_next = None
        for hd in range(A_HEADS):
            if hd + 1 < A_HEADS:
                scores[hd + 1] = _dot(kblk, qt_ref[hd + 1])
            elif not last:
                s0_next = _dot(keys(kb + 1), qt_ref[0])
            st = scores.pop(hd)
            if last:
                kpos = lax.broadcasted_iota(jnp.int32, st.shape, 0)
                qpos = lax.broadcasted_iota(jnp.int32, st.shape, 1)
                st = jnp.where(kpos <= qpos, st, -jnp.inf)
            m_old = m_scr[hd:hd + 1, :]
            m_new = jnp.maximum(m_old, jnp.max(st, axis=0, keepdims=True))
            alpha = jnp.exp2(m_old - m_new)
            p = jnp.exp2(st - m_new)
            l_scr[hd:hd + 1, :] = alpha * l_scr[hd:hd + 1, :] + jnp.sum(p, axis=0, keepdims=True)
            acc_scr[hd] = alpha * acc_scr[hd] + _dot(vt, p.astype(BF16))
            m_scr[hd:hd + 1, :] = m_new
        if not last:
            s0_scr[...] = s0_next

    def body(kb, carry):
        block(kb, False)
        return carry

    lax.fori_loop(0, qb, body, 0)
    block(qb, True)
    for hd in range(A_HEADS):
        o_ref[hd] = (acc_scr[hd] / l_scr[hd:hd + 1, :]).T.astype(BF16)


def _attn_prompt(qatt_t, katt, katt_t, n_seq):
    n = katt.shape[0]
    t = n // n_seq
    tq = ATT_BLOCK
    nq = t // tq
    return pl.pallas_call(
        _attn_prompt_kernel,
        grid=(n_seq, nq),
        in_specs=[
            pl.BlockSpec((A_HEADS, LAT_PAD, tq), lambda b, i: (0, 0, b * nq + i)),
            pl.BlockSpec((t, LAT_PAD), lambda b, i: (b, 0)),
            pl.BlockSpec((LAT_PAD, t), lambda b, i: (0, b)),
        ],
        out_specs=pl.BlockSpec((A_HEADS, tq, KV_RANK), lambda b, i: (0, b * nq + i, 0)),
        out_shape=jax.ShapeDtypeStruct((A_HEADS, n, KV_RANK), BF16),
        scratch_shapes=[pltpu.VMEM((SUBLANES, tq), F32), pltpu.VMEM((SUBLANES, tq), F32),
                        pltpu.VMEM((A_HEADS, KV_RANK, tq), F32), pltpu.VMEM((tq, tq), F32)],
        compiler_params=_params(2),
        name="attn_prompt",
    )(qatt_t, katt, katt_t)


def _attn_sample_kernel(pt_ref, q_ref, knew_ref, cache_ref, o_ref, pages, kbuf, sems, *, n_pages, t_seq, layer):
    s = pl.program_id(0)
    last = pl.num_programs(0) - 1
    slot = s % 2
    nxt = jnp.minimum(s + 1, last)
    page = pages.shape[3]

    def page_copy(seq, j, half):
        return pltpu.make_async_copy(cache_ref.at[layer, pt_ref[seq, j]], pages.at[half, j], sems.at[half])

    @pl.when(s == 0)
    def _():
        for j in range(n_pages):
            page_copy(0, j, 0).start(priority=j % 2)

    for j in range(n_pages):
        page_copy(s, j, slot).wait()
    for j in range(n_pages):
        page_copy(nxt, j, 1 - slot).start(priority=j % 2)
    for j in range(n_pages):
        kbuf[:, j * page:(j + 1) * page] = pages[slot, j].astype(BF16)

    q = q_ref[...]
    rows = q.shape[0]
    s_past = _dot(q[:, 0:LAT_DIM], kbuf[...])
    qf = q.astype(F32)
    knew = knew_ref[...].astype(F32)
    tq = lax.broadcasted_iota(jnp.int32, (rows, 1), 0) % t_seq
    s_new = []
    for j in range(t_seq):
        sj = jnp.sum(qf * knew[j:j + 1, :], axis=1, keepdims=True)
        s_new.append(jnp.where(tq >= j, sj, -jnp.inf))
    m = jnp.max(s_past, axis=1, keepdims=True)
    for sj in s_new:
        m = jnp.maximum(m, sj)
    p_past = jnp.exp2(s_past - m)
    p_new = [jnp.exp2(sj - m) for sj in s_new]
    denom = jnp.sum(p_past, axis=1, keepdims=True)
    for pj in p_new:
        denom = denom + pj
    inv = 1.0 / denom
    out = _dot_nt((p_past * inv).astype(BF16), kbuf[0:KV_RANK, :])
    for j in range(t_seq):
        pj = (p_new[j] * inv).astype(BF16).astype(F32)
        out = out + pj * knew[j:j + 1, 0:KV_RANK]
    o_ref[...] = out.astype(BF16)

    @pl.when(s == last)
    def _():
        for j in range(n_pages):
            page_copy(nxt, j, 1 - slot).wait()


def _attn_sample(q_seq, knew_seq, cache_t, page_table, layer):
    n_seq, rows, _ = q_seq.shape
    t_seq = knew_seq.shape[1]
    n_pages = page_table.shape[1]
    page = cache_t.shape[3]
    grid_spec = pltpu.PrefetchScalarGridSpec(
        num_scalar_prefetch=1,
        grid=(n_seq,),
        in_specs=[pl.BlockSpec((None, rows, LAT_PAD), lambda s, pt: (s, 0, 0)),
                  pl.BlockSpec((None, t_seq, LAT_PAD), lambda s, pt: (s, 0, 0)),
                  pl.BlockSpec(memory_space=pl.ANY)],
        out_specs=pl.BlockSpec((None, rows, KV_RANK), lambda s, pt: (s, 0, 0)),
        scratch_shapes=[pltpu.VMEM((2, n_pages, LAT_DIM, page), F32),
                        pltpu.VMEM((LAT_DIM, n_pages * page), BF16),
                        pltpu.SemaphoreType.DMA((2,))],
    )
    return pl.pallas_call(
        functools.partial(_attn_sample_kernel, n_pages=n_pages, t_seq=t_seq, layer=layer),
        grid_spec=grid_spec,
        out_shape=jax.ShapeDtypeStruct((n_seq, rows, KV_RANK), BF16),
        compiler_params=_params(1),
        name="attn_sample",
    )(page_table, q_seq, knew_seq, cache_t)


def _ffn_kernel(*refs, is_sample, is_last, t_seq, tiles_per_seq):
    (x_ref, hm_ref, ol_ref, p_ref, wuv_ref, wout_ref, gffn_ref, wup_ref, wc_ref, cb_ref,
     wdown_ref, gple_ref, wpg_ref, wpp_ref) = refs[:14]
    refs = refs[14:]
    if is_last:
        gfin_ref, refs = refs[0], refs[1:]
    if is_sample:
        st_ref, x_out, st_out = refs
    else:
        x_out, tail_ref = refs
    tm = x_ref.shape[0]
    d_ff = wdown_ref.shape[0]

    ha = [_dot(ol_ref[hd], wuv_ref[hd]).astype(BF16) for hd in range(A_HEADS)]
    mix = jnp.concatenate([hm_ref[...]] + ha, axis=1)
    x = x_ref[...] + _dot(mix, wout_ref[...])

    hf = _rms(x, gffn_ref[...]).astype(BF16)
    row = lax.broadcasted_iota(jnp.int32, (tm, d_ff), 0)
    if is_sample:
        n_st = (CONV_W - 1) * (tm // t_seq)
        tok = lax.broadcasted_iota(jnp.int32, (tm, n_st), 0)
        srow = lax.broadcasted_iota(jnp.int32, (tm, n_st), 1)
        seq0 = (CONV_W - 1) * (tok // t_seq)
        tpos = tok % t_seq
        prev1 = ((tpos == 0) & (srow == seq0 + 1)).astype(BF16)
        prev2 = ((tpos <= 1) & (srow == seq0 + tpos)).astype(BF16)
        srow_t = lax.broadcasted_iota(jnp.int32, (n_st, tm), 0)
        tok_t = lax.broadcasted_iota(jnp.int32, (n_st, tm), 1)
        keep = (tok_t == t_seq * (srow_t // (CONV_W - 1)) + t_seq - (CONV_W - 1)
                + srow_t % (CONV_W - 1)).astype(BF16)
    else:
        @pl.when(pl.program_id(0) % tiles_per_seq == 0)
        def _():
            tail_ref[...] = jnp.zeros_like(tail_ref)

    def conv(cols):
        u = _dot(hf, wup_ref[:, cols])
        r1 = pltpu.roll(u, 1, 0)
        r2 = pltpu.roll(u, 2, 0)
        if is_sample:
            state = st_ref[:, cols]
            pos = row % t_seq
            u1 = jnp.where(pos >= 1, r1, _exact_onehot_dot(prev1, state))
            u2 = jnp.where(pos >= 2, r2, _exact_onehot_dot(prev2, state))
            st_out[:, cols] = _exact_onehot_dot(keep, u)
        else:
            prev = tail_ref[:, cols]
            last, last2 = prev[SUBLANES - 1:SUBLANES, :], prev[SUBLANES - 2:SUBLANES - 1, :]
            u1 = jnp.where(row >= 1, r1, last)
            u2 = jnp.where(row >= 2, r2, jnp.where(row == 1, last, last2))
            tail_ref[:, cols] = u[tm - SUBLANES:tm, :]
        return (u2 * wc_ref[0:1, cols] + u1 * wc_ref[1:2, cols]) + u * wc_ref[2:3, cols] + cb_ref[:, cols]

    gate = conv(slice(0, d_ff))
    up = conv(slice(d_ff, 2 * d_ff))
    x = x + _dot((gate * _sigmoid(gate) * up).astype(BF16), wdown_ref[...])

    hp = _rms(x, gple_ref[...]).astype(BF16)
    x = x + _sigmoid(_dot(hp, wpg_ref[...])) * _dot(p_ref[...].astype(BF16), wpp_ref[...])
    x_out[...] = _rms(x, gfin_ref[...]) if is_last else x


def _ffn(x, hm, olat, p_all, layer, lw, *, final_gain, conv_state, n_seq, t_seq):
    n, d = x.shape
    d_up = lw['w_up'].shape[1]
    is_sample = conv_state is not None
    tm = SAMPLE_FFN_TILE if is_sample else TOK_TILE
    is_last = final_gain is not None
    row = lambda i: (i, 0)
    weights = [lw['w_uv'], lw['w_out'], lw['norm_ffn'], lw['w_up'], lw['ffn_conv'], lw['ffn_conv_b'],
               lw['w_down'], lw['ple_norm'], lw['w_pg'], lw['w_pp']]
    if is_last:
        weights.append(final_gain)
    args = [x, hm, olat, p_all] + weights
    in_specs = [pl.BlockSpec((tm, d), row), pl.BlockSpec((tm, 512), row),
                pl.BlockSpec((A_HEADS, tm, KV_RANK), lambda i: (0, i, 0)),
                pl.BlockSpec((None, tm, p_all.shape[2]), lambda i: (layer, i, 0))] + [_const_spec(w.shape) for w in weights]
    out_shape = [jax.ShapeDtypeStruct((n, d), F32)]
    out_specs = [pl.BlockSpec((tm, d), row)]
    if is_sample:
        n_st = (CONV_W - 1) * (tm // t_seq)
        args.append(conv_state)
        in_specs.append(pl.BlockSpec((n_st, d_up), row))
        out_shape.append(jax.ShapeDtypeStruct(conv_state.shape, F32))
        out_specs.append(pl.BlockSpec((n_st, d_up), row))
        tiles_per_seq = 1
    else:
        tiles_per_seq = n // n_seq // tm
        out_shape.append(jax.ShapeDtypeStruct((n_seq, SUBLANES, d_up), F32))
        out_specs.append(pl.BlockSpec((None, SUBLANES, d_up), lambda i: (i // tiles_per_seq, 0, 0)))
    return pl.pallas_call(
        functools.partial(_ffn_kernel, is_sample=is_sample, is_last=is_last, t_seq=t_seq,
                          tiles_per_seq=tiles_per_seq),
        grid=(n // tm,),
        in_specs=in_specs,
        out_specs=tuple(out_specs),
        out_shape=tuple(out_shape),
        compiler_params=_params(1),
        name="ffn_sample" if is_sample else "ffn_prompt",
    )(*args)


def _swap_halves(w):
    half = w.shape[-1] // 2
    return jnp.concatenate([w[..., half:], w[..., :half]], axis=-1)


def _layer_weights(l, norm_mix, w_in, m_gate_bias, m_norm, mla_q_norm, mla_w_uq, mla_kv_norm, mla_w_uk,
                   mla_w_uv, w_out, norm_ffn, ffn_w_up, ffn_conv, ffn_conv_b, ffn_w_down, ple_norm,
                   ple_w_gate, ple_w_proj):
    d = w_in.shape[1]
    wi = w_in[l]
    sizes = (256, 256, 512, 512, M_HEADS, M_HEADS, Q_RANK, KV_RANK, A_ROPE)
    offs = [0]
    for s in sizes:
        offs.append(offs[-1] + s)
    qm, km, vm, om, im, fm, cq, ckv, kr = (wi[:, offs[i]:offs[i + 1]] for i in range(len(sizes)))
    gate_pad = jnp.zeros((d, LANES - 2 * M_HEADS), wi.dtype)
    w_in_r = jnp.concatenate([fm, im, gate_pad, cq, ckv, kr, _swap_halves(kr), vm, om, qm, km], axis=1)
    uq = mla_w_uq[l].reshape(Q_RANK, A_HEADS, A_NOPE + A_ROPE)
    uq_rope = uq[:, :, A_NOPE:]
    w_uq_r = jnp.concatenate([uq[:, :, :A_NOPE].reshape(Q_RANK, A_HEADS * A_NOPE),
                              jnp.concatenate([uq_rope, _swap_halves(uq_rope)], axis=-1)
                              .reshape(Q_RANK, A_HEADS * LANES)], axis=1)
    row = lambda v: v.reshape(1, -1)
    return {
        'norm_mix': row(norm_mix[l]),
        'w_in': w_in_r.astype(BF16),
        'w_voq_t': w_in_r[:, C_V:C_K].T.astype(BF16),
        'gate_bias': jnp.concatenate([m_gate_bias[l, 1], m_gate_bias[l, 0]]).reshape(SUBLANES, 1),
        'm_norm': m_norm[l],
        'mla_q_norm': row(mla_q_norm[l]),
        'w_uq': w_uq_r.astype(BF16),
        'mla_kv_norm': row(mla_kv_norm[l]),
        'w_uk': jnp.transpose(mla_w_uk[l], (1, 2, 0)).astype(BF16),
        'w_uv': jnp.transpose(mla_w_uv[l], (1, 0, 2)).astype(BF16),
        'w_out': w_out[l].astype(BF16),
        'norm_ffn': row(norm_ffn[l]),
        'w_up': ffn_w_up[l].astype(BF16),
        'ffn_conv': ffn_conv[l],
        'ffn_conv_b': row(ffn_conv_b[l]),
        'w_down': ffn_w_down[l].astype(BF16),
        'ple_norm': row(ple_norm[l]),
        'w_pg': ple_w_gate[l].astype(BF16),
        'w_pp': ple_w_proj[l].astype(BF16),
    }


def _rope_table(pos):
    half = A_ROPE // 2
    freqs = ROPE_THETA ** (-jnp.arange(half, dtype=F32) * 2.0 / A_ROPE)
    ang = pos.astype(F32)[:, None] * freqs[None, :]
    cos, sin = jnp.cos(ang), jnp.sin(ang)
    return jnp.concatenate([cos, cos, -sin, sin], axis=1)


def kernel(x_prompt, x_sample, p_prompt, p_sample, cache_mla, state_mlstm_C, state_mlstm_n, state_mlstm_m,
           state_ffn_conv, page_table, norm_mix, w_in, m_gate_bias, m_norm, mla_q_norm, mla_w_uq, mla_kv_norm,
           mla_w_uk, mla_w_uv, w_out, norm_ffn, ffn_w_up, ffn_conv, ffn_conv_b, ffn_w_down, ple_norm,
           ple_w_gate, ple_w_proj, final_norm):
    depth = w_in.shape[0]
    bp, tp, d = x_prompt.shape
    bs, ts, _ = x_sample.shape
    n_p, n_s = bp * tp, bs * ts
    past_len = page_table.shape[1] * cache_mla.shape[2]
    d_up = ffn_w_up.shape[2]

    tab_p = jnp.tile(_rope_table(jnp.arange(tp)), (bp, 1))
    tab_s = jnp.tile(_rope_table(past_len + jnp.arange(ts)), (bs, 1))
    xp = x_prompt.reshape(n_p, d)
    xs = x_sample.reshape(n_s, d)
    final_gain = final_norm.reshape(1, d)
    cache_t = jnp.swapaxes(cache_mla, 2, 3)

    lat_p, lat_s, c_p, n_pl, m_p, c_s, n_sl, m_s, conv_p, conv_s = ([] for _ in range(10))
    for l in range(depth):
        lw = _layer_weights(l, norm_mix, w_in, m_gate_bias, m_norm, mla_q_norm, mla_w_uq, mla_kv_norm,
                            mla_w_uk, mla_w_uv, w_out, norm_ffn, ffn_w_up, ffn_conv, ffn_conv_b,
                            ffn_w_down, ple_norm, ple_w_gate, ple_w_proj)
        fin = final_gain if l == depth - 1 else None

        km, qt, vt, ot, gscan, gdiff, dcol, lat, katt, katt_t, qatt_t = _proj(xp, tab_p, lw, True, bp)
        by_seq = lambda a: a.reshape(bp, tp, a.shape[-1])
        hm, state, m_new = _mlstm_prompt(by_seq(km), qt, vt, ot, gscan, gdiff, by_seq(dcol), lw['m_norm'].T, bp)
        c_new = jnp.swapaxes(state[:, :, :M_DV, :], 2, 3)
        n_new = state[:, :, M_DV, :]
        olat = _attn_prompt(qatt_t, katt, katt_t, bp)
        xp, tail = _ffn(xp, hm.reshape(n_p, -1), olat, p_prompt.reshape(depth, n_p, -1), l, lw, final_gain=fin,
                        conv_state=None, n_seq=bp, t_seq=tp)
        lat_p.append(lat.reshape(bp, tp, LAT_DIM))
        c_p.append(c_new)
        n_pl.append(n_new)
        m_p.append(m_new[:, :M_HEADS, 0])
        conv_p.append(tail[:, SUBLANES - (CONV_W - 1):])

        qk, v, o, gates, lat, katt, qatt = _proj(xs, tab_s, lw, False, 1)
        m_prev_rows = jnp.repeat(state_mlstm_m[l].T, ts, axis=1)
        hm, c_new, n_new_t, m_rows = _mlstm_sample(qk, v, o, gates, m_prev_rows, lw['m_norm'],
                                                   state_mlstm_C, jnp.transpose(state_mlstm_n[l], (1, 0, 2)),
                                                   l, ts)
        q_seq = jnp.transpose(qatt.reshape(A_HEADS, bs, ts, LAT_PAD), (1, 0, 2, 3)).reshape(bs, A_HEADS * ts, LAT_PAD)
        o_seq = _attn_sample(q_seq, katt.reshape(bs, ts, LAT_PAD), cache_t, page_table, l)
        olat = jnp.transpose(o_seq.reshape(bs, A_HEADS, ts, KV_RANK), (1, 0, 2, 3)).reshape(A_HEADS, n_s, KV_RANK)
        xs, st_new = _ffn(xs, hm, olat, p_sample.reshape(depth, n_s, -1), l, lw, final_gain=fin,
                          conv_state=state_ffn_conv[l].reshape(bs * (CONV_W - 1), d_up), n_seq=bs, t_seq=ts)
        lat_s.append(lat.reshape(bs, ts, LAT_DIM))
        c_s.append(c_new)
        n_sl.append(jnp.transpose(n_new_t, (1, 0, 2)))
        m_s.append(m_rows[:, ts - 1::ts].T)
        conv_s.append(st_new.reshape(bs, CONV_W - 1, d_up))

    return (xp.reshape(bp, tp, d), xs.reshape(bs, ts, d), jnp.stack(lat_p), jnp.stack(lat_s),
            jnp.stack(c_p), jnp.stack(n_pl), jnp.stack(m_p), jnp.stack(c_s), jnp.stack(n_sl), jnp.stack(m_s),
            jnp.stack(conv_p), jnp.stack(conv_s))
```

```python
import functools

import jax
import jax.numpy as jnp
from jax import lax
from jax.experimental import pallas as pl
from jax.experimental.pallas import tpu as pltpu

F32 = jnp.float32
BF16 = jnp.bfloat16

M_HEADS = 4
M_DK = 64
M_DV = 128
A_HEADS = 4
A_NOPE = 128
A_ROPE = 64
KV_RANK = 256
Q_RANK = 384
LAT_DIM = KV_RANK + A_ROPE
GATE_CAP = 15.0
ROPE_THETA = 10000.0
EPS = 1e-6
CONV_W = 3
ATT_SCALE = (A_NOPE + A_ROPE) ** -0.5
LOG2_E = 1.4426950408889634
Q_SCALE = ATT_SCALE * LOG2_E

LANES = 128
SUBLANES = 8
VMEM_LIMIT_BYTES = 56 * 1024 * 1024

LAT_PAD = 3 * LANES
TOK_TILE = 256
MLSTM_CHUNK = 256
ATT_BLOCK = 512
SAMPLE_GROUP = 32
PAGE_RING = 3
SAMPLE_FFN_TILE = 128
STATE_ROWS = M_DV + SUBLANES

C_G, C_CQ, C_CKV, C_KR = 0, 128, 512, 768
C_V, C_O, C_Q, C_K = 896, 1408, 1920, 2176
IN_COLS = 2432


def _const_spec(shape):
    nd = len(shape)
    return pl.BlockSpec(shape, lambda *_: (0,) * nd, pipeline_mode=pl.Buffered(1))


def _weight(w):
    if not isinstance(w, tuple):
        return _const_spec(w.shape), w
    arr, layer = w
    tail = (0,) * (arr.ndim - 1)
    return pl.BlockSpec((None,) + arr.shape[1:], lambda *_: (layer,) + tail, pipeline_mode=pl.Buffered(1)), arr


def _params(n_axes):
    return pltpu.CompilerParams(dimension_semantics=("arbitrary",) * n_axes,
                                vmem_limit_bytes=VMEM_LIMIT_BYTES)


def _rms(x, g):
    return x * lax.rsqrt(jnp.mean(x * x, axis=-1, keepdims=True) + EPS) * g


def _sigmoid(x):
    return 1.0 / (1.0 + jnp.exp(-x))


def _dot(a, b):
    return jnp.dot(a, b, preferred_element_type=F32)


def _dot_nt(a, b):
    return lax.dot_general(a, b, (((1,), (1,)), ((), ())), preferred_element_type=F32)


def _fold_halves(p):
    return p + pltpu.roll(p, 64, 1)


def _proj_kernel(x_ref, tab_ref, gmix_ref, win_ref, gb_ref, gq_ref, wuq_ref, gkv_ref, wuk_ref,
                 *refs, feature_major):
    if feature_major:
        (wvoq_ref, km_ref, qt_ref, vt_ref, ot_ref, gscan_ref, gdiff_ref, dcol_ref,
         lat_ref, katt_ref, katt_t_ref, qatt_ref) = refs
    else:
        qk_ref, v_ref, o_ref, gates_ref, lat_ref, katt_ref, qatt_ref = refs
    h = _rms(x_ref[...], gmix_ref[...]).astype(BF16)
    z = _dot(h, win_ref[:, 0:C_V])

    gt = z[:, C_G:C_G + LANES].T[0:SUBLANES, :] + gb_ref[...]
    capped = GATE_CAP * jnp.tanh(gt / GATE_CAP)
    log_sig = jnp.minimum(capped, 0.0) - jnp.log1p(jnp.exp(-jnp.abs(capped)))
    row = lax.broadcasted_iota(jnp.int32, gt.shape, 0)
    gates = jnp.where(row < M_HEADS, log_sig, capped)
    if feature_major:
        b, d, e = _gate_scans(gates, None)
        gscan_ref[...] = jnp.where(row < M_HEADS, b, pltpu.roll(e, M_HEADS, 0))
        d = jnp.where(row < M_HEADS, d, 0.0)
        gdiff_ref[...] = d
        dcol_ref[...] = _to_cols((d,), d.shape[1])
    else:
        gates_ref[...] = gates

    tab = tab_ref[...]
    lane = lax.broadcasted_iota(jnp.int32, tab.shape, 1)
    ckv_n = _rms(z[:, C_CKV:C_CKV + KV_RANK], gkv_ref[...])
    rk = _fold_halves(z[:, C_KR:C_KR + LANES] * tab)
    lat_ref[:, 0:KV_RANK] = ckv_n
    lat_ref[:, KV_RANK:LAT_DIM] = rk[:, 0:A_ROPE]
    k_cat = jnp.concatenate([ckv_n, jnp.where(lane < A_ROPE, rk, 0.0)], axis=1)
    katt_ref[...] = k_cat.astype(BF16)
    if feature_major:
        katt_t_ref[...] = k_cat.T.astype(BF16)

    qn = _rms(z[:, C_CQ:C_CQ + Q_RANK], gq_ref[...]).astype(BF16)
    qa = _dot(qn, wuq_ref[...])
    for hd in range(A_HEADS):
        q_lat = _dot(qa[:, hd * A_NOPE:(hd + 1) * A_NOPE].astype(BF16), wuk_ref[hd])
        rq = _fold_halves(qa[:, 512 + hd * LANES:512 + (hd + 1) * LANES] * tab)
        q_cat = jnp.concatenate([q_lat, jnp.where(lane < A_ROPE, rq, 0.0)], axis=1) * Q_SCALE
        qatt_ref[hd] = (q_cat.T if feature_major else q_cat).astype(BF16)

    k_m = (_dot(h, win_ref[:, C_K:C_K + 256]) * (M_DK ** -0.5)).astype(BF16)
    if feature_major:
        zt = _dot_nt(wvoq_ref[...], h)
        vt_ref[...] = zt[0:512, :].astype(BF16)
        ot_ref[...] = zt[512:1024, :]
        qt_ref[...] = zt[1024:1280, :].astype(BF16)
        km_ref[...] = k_m
    else:
        z2 = _dot(h, win_ref[:, C_V:C_K])
        v_ref[...] = z2[:, 0:512].astype(BF16)
        o_ref[...] = z2[:, 512:1024]
        qk_ref[:, 0:256] = z2[:, 1024:1280].astype(BF16)
        qk_ref[:, 256:512] = k_m


def _proj(x, tab, lw, feature_major, gate_seqs):
    n = x.shape[0]
    tm = TOK_TILE
    tiles_per_seq = n // gate_seqs // tm
    row = lambda i: (i, 0)
    col = lambda i: (0, i)
    band = lambda i: (i // tiles_per_seq, i % tiles_per_seq)
    gate_rows = jax.ShapeDtypeStruct((gate_seqs * SUBLANES, n // gate_seqs), F32)
    if feature_major:
        assert tm == MLSTM_CHUNK
        out_shape = [
            jax.ShapeDtypeStruct((n, 256), BF16),
            jax.ShapeDtypeStruct((256, n), BF16),
            jax.ShapeDtypeStruct((512, n), BF16),
            jax.ShapeDtypeStruct((512, n), F32),
            gate_rows,
            gate_rows,
            jax.ShapeDtypeStruct((n, LANES), F32),
        ]
        out_specs = [
            pl.BlockSpec((tm, 256), row),
            pl.BlockSpec((256, tm), col),
            pl.BlockSpec((512, tm), col),
            pl.BlockSpec((512, tm), col),
            pl.BlockSpec((SUBLANES, tm), band),
            pl.BlockSpec((SUBLANES, tm), band),
            pl.BlockSpec((tm, LANES), row),
        ]
    else:
        out_shape = [
            jax.ShapeDtypeStruct((n, 512), BF16),
            jax.ShapeDtypeStruct((n, 512), BF16),
            jax.ShapeDtypeStruct((n, 512), F32),
            gate_rows,
        ]
        out_specs = [
            pl.BlockSpec((tm, 512), row),
            pl.BlockSpec((tm, 512), row),
            pl.BlockSpec((tm, 512), row),
            pl.BlockSpec((SUBLANES, tm), band),
        ]
    out_shape += [jax.ShapeDtypeStruct((n, LAT_DIM), F32),
                  jax.ShapeDtypeStruct((n, LAT_PAD), BF16)]
    out_specs += [pl.BlockSpec((tm, LAT_DIM), row), pl.BlockSpec((tm, LAT_PAD), row)]
    if feature_major:
        out_shape += [jax.ShapeDtypeStruct((LAT_PAD, n), BF16),
                      jax.ShapeDtypeStruct((A_HEADS, LAT_PAD, n), BF16)]
        out_specs += [pl.BlockSpec((LAT_PAD, tm), col),
                      pl.BlockSpec((A_HEADS, LAT_PAD, tm), lambda i: (0, 0, i))]
    else:
        out_shape.append(jax.ShapeDtypeStruct((A_HEADS, n, LAT_PAD), BF16))
        out_specs.append(pl.BlockSpec((A_HEADS, tm, LAT_PAD), lambda i: (0, i, 0)))
    return pl.pallas_call(
        functools.partial(_proj_kernel, feature_major=feature_major),
        grid=(n // tm,),
        in_specs=[
            pl.BlockSpec((tm, x.shape[1]), row),
            pl.BlockSpec((tm, LANES), row),
            _const_spec(lw['norm_mix'].shape),
            _const_spec(lw['w_in'].shape),
            _const_spec(lw['gate_bias'].shape),
            _const_spec(lw['mla_q_norm'].shape),
            _const_spec(lw['w_uq'].shape),
            _const_spec(lw['mla_kv_norm'].shape),
            _const_spec(lw['w_uk'].shape),
        ] + ([_const_spec(lw['w_voq_t'].shape)] if feature_major else []),
        out_specs=tuple(out_specs),
        out_shape=tuple(out_shape),
        compiler_params=_params(1),
        name="proj_prompt" if feature_major else "proj_sample",
    )(x, tab, lw['norm_mix'], lw['w_in'], lw['gate_bias'], lw['mla_q_norm'], lw['w_uq'],
      lw['mla_kv_norm'], lw['w_uk'], *([lw['w_voq_t']] if feature_major else []))


def _seg_scan(x, op, fill, seg):
    n = x.shape[1]
    lane = lax.broadcasted_iota(jnp.int32, x.shape, 1)
    pos = lane if seg is None else lane % seg
    span = n if seg is None else seg
    s = 1
    while s < span:
        x = op(x, jnp.where(pos >= s, pltpu.roll(x, s, 1), fill))
        s *= 2
    return x


def _gate_scans(gates, seg):
    ig = pltpu.roll(gates, M_HEADS, 0)
    b = _seg_scan(gates, jnp.add, 0.0, seg)
    d = ig - b
    e = _seg_scan(d, jnp.maximum, -jnp.inf, seg)
    return b, d, e


def _gate_rows(gates, m_prev, seg):
    b, d, e = _gate_scans(gates, seg)
    return b, d, pltpu.roll(gates, M_HEADS, 0), b + jnp.maximum(m_prev, e)


def _to_cols(row_blocks, n_tok):
    pad = jnp.zeros((LANES - SUBLANES * len(row_blocks), n_tok), F32)
    return jnp.concatenate(list(row_blocks) + [pad], axis=0).T


def _head_out(hh, og, gain):
    hn = hh * lax.rsqrt(jnp.mean(hh * hh, axis=-1, keepdims=True) + EPS) * gain
    return (hn * _sigmoid(og)).astype(BF16)


def _mlstm_prompt_kernel(*refs, n_seq):
    km_ref, gscan_ref, gdiff_ref, dcol_ref, gain_ref = refs[:5]
    qt_refs = refs[5:5 + n_seq]
    vt_refs = refs[5 + n_seq:5 + 2 * n_seq]
    ot_refs = refs[5 + 2 * n_seq:5 + 3 * n_seq]
    hm_ref, s_ref, m_ref = refs[5 + 3 * n_seq:]
    L = km_ref.shape[1]

    @pl.when(pl.program_id(0) == 0)
    def _():
        s_ref[...] = jnp.zeros_like(s_ref)
        m_ref[...] = jnp.zeros_like(m_ref)

    ki = lax.broadcasted_iota(jnp.int32, (L, L), 0)
    qi = lax.broadcasted_iota(jnp.int32, (L, L), 1)
    causal = ki <= qi
    pad_rows = jnp.zeros((STATE_ROWS - M_DV - 1, L), F32)

    for sq in range(n_seq):
        band = slice(sq * SUBLANES, (sq + 1) * SUBLANES)
        scan = gscan_ref[band, :]
        diff = gdiff_ref[band, :]
        ks = [km_ref[sq, :, hd * M_DK:(hd + 1) * M_DK] for hd in range(M_HEADS)]
        qts = [qt_refs[sq][hd * M_DK:(hd + 1) * M_DK, :] for hd in range(M_HEADS)]
        states = [s_ref[sq, hd] for hd in range(M_HEADS)]
        kq = [_dot(ks[hd], qts[hd]) for hd in range(M_HEADS)]
        q_state = [_dot(states[hd].astype(BF16), qts[hd]) for hd in range(M_HEADS)]

        for hd in range(M_HEADS):
            vt = vt_refs[sq][hd * M_DV:(hd + 1) * M_DV, :]
            b_r = scan[hd:hd + 1, :]
            e_r = scan[M_HEADS + hd:M_HEADS + hd + 1, :]
            d_r = diff[hd:hd + 1, :]
            d_c = dcol_ref[sq, :, hd:hd + 1]
            mp = m_ref[sq, hd:hd + 1, 0:1]
            mx_r = jnp.maximum(mp, e_r)
            mx_last = mx_r[:, L - 1:L]

            st = kq[hd] * jnp.exp(jnp.where(causal, d_c - mx_r, -jnp.inf))
            a_r = jnp.exp(mp - mx_r)
            num = _dot(vt, st.astype(BF16)) + a_r * q_state[hd][0:M_DV, :]
            nq = jnp.sum(st, axis=0, keepdims=True) + a_r * q_state[hd][M_DV:M_DV + 1, :]
            inv = 1.0 / jnp.maximum(jnp.abs(nq), jnp.exp(-(b_r + mx_r)))
            rs = lax.rsqrt(inv * inv * jnp.mean(num * num, axis=0, keepdims=True) + EPS)
            ht = num * (inv * rs) * gain_ref[:, hd:hd + 1] * _sigmoid(ot_refs[sq][hd * M_DV:(hd + 1) * M_DV, :])
            hm_ref[sq, :, hd * M_DV:(hd + 1) * M_DV] = ht.T.astype(BF16)

            w_r = jnp.exp(d_r - mx_last)
            lhs = jnp.concatenate([vt.astype(F32) * w_r, w_r, pad_rows], axis=0).astype(BF16)
            s_ref[sq, hd] = jnp.exp(mp - mx_last) * states[hd] + _dot(lhs, ks[hd])
            m_ref[sq, hd:hd + 1, :] = jnp.broadcast_to(b_r[:, L - 1:L] + mx_last, (1, LANES))


def _mlstm_prompt(km, qt, vt, ot, gscan, gdiff, dcol, gain_t, n_seq):
    t = km.shape[1]
    L = MLSTM_CHUNK
    nc = t // L
    tok = lambda c: (0, c, 0)
    whole4 = lambda c: (0, 0, 0, 0)
    whole3 = lambda c: (0, 0, 0)

    def seq_cols(rows):
        return [pl.BlockSpec((rows, L), lambda c, sq=sq: (0, sq * nc + c)) for sq in range(n_seq)]

    return pl.pallas_call(
        functools.partial(_mlstm_prompt_kernel, n_seq=n_seq),
        grid=(nc,),
        in_specs=[
            pl.BlockSpec((n_seq, L, 256), tok),
            pl.BlockSpec((n_seq * SUBLANES, L), lambda c: (0, c)),
            pl.BlockSpec((n_seq * SUBLANES, L), lambda c: (0, c)),
            pl.BlockSpec((n_seq, L, LANES), tok),
            _const_spec(gain_t.shape),
        ] + seq_cols(256) + seq_cols(512) + seq_cols(512),
        out_specs=(
            pl.BlockSpec((n_seq, L, 512), tok),
            pl.BlockSpec((n_seq, M_HEADS, STATE_ROWS, M_DK), whole4),
            pl.BlockSpec((n_seq, SUBLANES, LANES), whole3),
        ),
        out_shape=(
            jax.ShapeDtypeStruct((n_seq, t, 512), BF16),
            jax.ShapeDtypeStruct((n_seq, M_HEADS, STATE_ROWS, M_DK), F32),
            jax.ShapeDtypeStruct((n_seq, SUBLANES, LANES), F32),
        ),
        compiler_params=_params(1),
        name="mlstm_prompt",
    )(km, gscan, gdiff, dcol, gain_t, *([qt] * n_seq), *([vt] * n_seq), *([ot] * n_seq))


def _exact_onehot_dot(onehot, x):
    hi = x.astype(BF16)
    r1 = x - hi.astype(F32)
    mid = r1.astype(BF16)
    lo = (r1 - mid.astype(F32)).astype(BF16)
    return (_dot(onehot, hi) + _dot(onehot, mid)) + _dot(onehot, lo)


def _mlstm_sample_kernel(qk_ref, v_ref, o_ref, g_ref, mp_ref, gain_ref, c0_ref, n0_ref,
                         hm_ref, c_ref, n_ref, m_ref, cols_ref, *, t_seq):
    L = qk_ref.shape[0]
    G = L // t_seq
    mp_rows = jnp.concatenate([mp_ref[...], jnp.zeros((SUBLANES - M_HEADS, L), F32)], axis=0)
    b, d, ig, m = _gate_rows(g_ref[...], mp_rows, t_seq)
    m_ref[...] = m[0:M_HEADS, :]

    lane = lax.broadcasted_iota(jnp.int32, b.shape, 1)
    pos = lane % t_seq

    def seg_last(x):
        s = 1
        while s < t_seq:
            x = jnp.where(pos >= t_seq - s, x, pltpu.roll(x, L - s, 1))
            s *= 2
        return x

    b_last = seg_last(b)
    m_new = seg_last(m)
    w_rows = jnp.exp(b_last - b + ig - m_new)
    decay_rows = jnp.exp(b_last + mp_rows - m_new)
    cols_ref[...] = _to_cols((b, m, mp_rows, w_rows, decay_rows), L)
    cols = cols_ref[...]
    seq_cols = cols_ref[pl.ds(0, G, stride=t_seq), :]

    qi = lax.broadcasted_iota(jnp.int32, (L, L), 0)
    ki = lax.broadcasted_iota(jnp.int32, (L, L), 1)
    mask = (ki <= qi) & (ki // t_seq == qi // t_seq)
    er = lax.broadcasted_iota(jnp.int32, (L, G * M_DK), 0)
    ec = lax.broadcasted_iota(jnp.int32, (L, G * M_DK), 1)
    own = (ec // M_DK) == (er // t_seq)
    tok_of_seq = (lax.broadcasted_iota(jnp.int32, (L, G), 0) // t_seq
                  == lax.broadcasted_iota(jnp.int32, (L, G), 1)).astype(BF16)
    seq_of_tok = (lax.broadcasted_iota(jnp.int32, (G, L), 1) // t_seq
                  == lax.broadcasted_iota(jnp.int32, (G, L), 0)).astype(BF16)

    def expand(x):
        x2 = jnp.concatenate([x, x], axis=1)
        return jnp.where(own, jnp.tile(x2, (1, G // 2)), 0.0)

    for hd in range(M_HEADS):
        q = qk_ref[:, hd * M_DK:(hd + 1) * M_DK]
        k = qk_ref[:, 256 + hd * M_DK:256 + (hd + 1) * M_DK]
        v = v_ref[:, hd * M_DV:(hd + 1) * M_DV]
        b_c = cols[:, hd:hd + 1]
        m_c = cols[:, 8 + hd:9 + hd]
        mp_c = cols[:, 16 + hd:17 + hd]
        w_c = cols[:, 24 + hd:25 + hd]
        decay_seq = seq_cols[:, 32 + hd:33 + hd]
        n_old = n0_ref[hd]
        c_old = c0_ref[:, hd].reshape(G * M_DK, M_DV)

        p = jnp.exp(jnp.where(mask, b_c + d[hd:hd + 1, :] - m_c, -jnp.inf))
        s = _dot_nt(q, k) * p
        a_c = jnp.exp(b_c + mp_c - m_c)
        qf = q.astype(F32)
        num = _dot(s.astype(BF16), v) + a_c * _dot(expand(qf).astype(BF16), c_old.astype(BF16))
        n_tok = _exact_onehot_dot(tok_of_seq, n_old)
        nq = jnp.sum(s, axis=1, keepdims=True) + a_c * jnp.sum(qf * n_tok, axis=1, keepdims=True)
        den = jnp.maximum(jnp.abs(nq), jnp.exp(-m_c))
        sl = slice(hd * M_DV, (hd + 1) * M_DV)
        hm_ref[:, sl] = _head_out(num / den, o_ref[:, sl], gain_ref[hd:hd + 1, :])

        kw = k.astype(F32) * w_c
        upd = _dot(expand(kw).T.astype(BF16), v)
        for g in range(G):
            c_ref[g, hd] = decay_seq[g:g + 1, :] * c0_ref[g, hd] + upd[g * M_DK:(g + 1) * M_DK, :]
        n_ref[hd] = decay_seq * n_old + _dot(seq_of_tok, kw.astype(BF16))


def _mlstm_sample(qk, v, o, gates, m_prev_rows, gain, c0, n0_t, layer, t_seq):
    n = qk.shape[0]
    G = SAMPLE_GROUP
    L = G * t_seq
    n_seq = n // t_seq
    tok = lambda i: (i, 0)
    return pl.pallas_call(
        functools.partial(_mlstm_sample_kernel, t_seq=t_seq),
        grid=(n // L,),
        in_specs=[
            pl.BlockSpec((L, 512), tok),
            pl.BlockSpec((L, 512), tok),
            pl.BlockSpec((L, 512), tok),
            pl.BlockSpec((SUBLANES, L), lambda i: (0, i)),
            pl.BlockSpec((M_HEADS, L), lambda i: (0, i)),
            _const_spec(gain.shape),
            pl.BlockSpec((None, G, M_HEADS, M_DK, M_DV), lambda i: (layer, i, 0, 0, 0)),
            pl.BlockSpec((M_HEADS, G, M_DK), lambda i: (0, i, 0)),
        ],
        out_specs=(
            pl.BlockSpec((L, 512), tok),
            pl.BlockSpec((G, M_HEADS, M_DK, M_DV), lambda i: (i, 0, 0, 0)),
            pl.BlockSpec((M_HEADS, G, M_DK), lambda i: (0, i, 0)),
            pl.BlockSpec((M_HEADS, L), lambda i: (0, i)),
        ),
        out_shape=(
            jax.ShapeDtypeStruct((n, 512), BF16),
            jax.ShapeDtypeStruct((n_seq, M_HEADS, M_DK, M_DV), F32),
            jax.ShapeDtypeStruct((M_HEADS, n_seq, M_DK), F32),
            jax.ShapeDtypeStruct((M_HEADS, n), F32),
        ),
        scratch_shapes=[pltpu.VMEM((L, LANES), F32)],
        compiler_params=_params(1),
        name="mlstm_sample",
    )(qk, v, o, gates, m_prev_rows, gain, c0, n0_t)


def _attn_prompt_kernel(qt_ref, k_ref, kt_ref, o_ref, m_scr, l_scr, acc_scr, s0_scr):
    tq = qt_ref.shape[2]
    qb = pl.program_id(1)
    m_scr[...] = jnp.full_like(m_scr, -jnp.inf)
    l_scr[...] = jnp.zeros_like(l_scr)
    acc_scr[...] = jnp.zeros_like(acc_scr)

    def keys(kb):
        return k_ref[pl.ds(pl.multiple_of(kb * tq, tq), tq), :]

    s0_scr[...] = _dot(keys(0), qt_ref[0])

    def block(kb, last):
        kblk = keys(kb)
        vt = kt_ref[0:KV_RANK, pl.ds(pl.multiple_of(kb * tq, tq), tq)]
        scores = {0: s0_scr[...]}
        s0_next = None
        for hd in range(A_HEADS):
            if hd + 1 < A_HEADS:
                scores[hd + 1] = _dot(kblk, qt_ref[hd + 1])
            elif not last:
                s0_next = _dot(keys(kb + 1), qt_ref[0])
            st = scores.pop(hd)
            if last:
                kpos = lax.broadcasted_iota(jnp.int32, st.shape, 0)
                qpos = lax.broadcasted_iota(jnp.int32, st.shape, 1)
                st = jnp.where(kpos <= qpos, st, -jnp.inf)
            m_old = m_scr[hd:hd + 1, :]
            m_new = jnp.maximum(m_old, jnp.max(st, axis=0, keepdims=True))
            alpha = jnp.exp2(m_old - m_new)
            p = jnp.exp2(st - m_new)
            l_scr[hd:hd + 1, :] = alpha * l_scr[hd:hd + 1, :] + jnp.sum(p, axis=0, keepdims=True)
            acc_scr[hd] = alpha * acc_scr[hd] + _dot(vt, p.astype(BF16))
            m_scr[hd:hd + 1, :] = m_new
        if not last:
            s0_scr[...] = s0_next

    def body(kb, carry):
        block(kb, False)
        return carry

    lax.fori_loop(0, qb, body, 0)
    block(qb, True)
    for hd in range(A_HEADS):
        o_ref[hd] = (acc_scr[hd] / l_scr[hd:hd + 1, :]).T.astype(BF16)


def _attn_prompt(qatt_t, katt, katt_t, n_seq):
    n = katt.shape[0]
    t = n // n_seq
    tq = ATT_BLOCK
    nq = t // tq
    return pl.pallas_call(
        _attn_prompt_kernel,
        grid=(n_seq, nq),
        in_specs=[
            pl.BlockSpec((A_HEADS, LAT_PAD, tq), lambda b, i: (0, 0, b * nq + i)),
            pl.BlockSpec((t, LAT_PAD), lambda b, i: (b, 0)),
            pl.BlockSpec((LAT_PAD, t), lambda b, i: (0, b)),
        ],
        out_specs=pl.BlockSpec((A_HEADS, tq, KV_RANK), lambda b, i: (0, b * nq + i, 0)),
        out_shape=jax.ShapeDtypeStruct((A_HEADS, n, KV_RANK), BF16),
        scratch_shapes=[pltpu.VMEM((SUBLANES, tq), F32), pltpu.VMEM((SUBLANES, tq), F32),
                        pltpu.VMEM((A_HEADS, KV_RANK, tq), F32), pltpu.VMEM((tq, tq), F32)],
        compiler_params=_params(2),
        name="attn_prompt",
    )(qatt_t, katt, katt_t)


def _attn_sample_kernel(pt_ref, q_ref, knew_ref, cache_ref, o_ref, pages, kbuf, sems, *, n_pages, t_seq, layer):
    s = pl.program_id(0)
    n_seq = pl.num_programs(0)
    page = pages.shape[3]

    def page_copy(seq, j):
        slot = seq % PAGE_RING
        return pltpu.make_async_copy(cache_ref.at[layer, pt_ref[seq, j]], pages.at[slot, j], sems.at[slot])

    def request(seq):
        for j in range(n_pages):
            page_copy(seq, j).start(priority=j % 2)

    @pl.when(s == 0)
    def _():
        for ahead in range(PAGE_RING - 1):
            request(ahead)

    for j in range(n_pages):
        page_copy(s, j).wait()

    @pl.when(s + PAGE_RING - 1 < n_seq)
    def _():
        request(s + PAGE_RING - 1)

    slot = s % PAGE_RING
    for j in range(n_pages):
        kbuf[:, j * page:(j + 1) * page] = pages[slot, j].astype(BF16)

    q = q_ref[...]
    rows = q.shape[0]
    s_past = _dot(q[:, 0:LAT_DIM], kbuf[...])
    qf = q.astype(F32)
    knew = knew_ref[...].astype(F32)
    tq = lax.broadcasted_iota(jnp.int32, (rows, 1), 0) % t_seq
    s_new = []
    for j in range(t_seq):
        sj = jnp.sum(qf * knew[j:j + 1, :], axis=1, keepdims=True)
        s_new.append(jnp.where(tq >= j, sj, -jnp.inf))
    m = jnp.max(s_past, axis=1, keepdims=True)
    for sj in s_new:
        m = jnp.maximum(m, sj)
    p_past = jnp.exp2(s_past - m)
    p_new = [jnp.exp2(sj - m) for sj in s_new]
    denom = jnp.sum(p_past, axis=1, keepdims=True)
    for pj in p_new:
        denom = denom + pj
    inv = 1.0 / denom
    out = _dot_nt((p_past * inv).astype(BF16), kbuf[0:KV_RANK, :])
    for j in range(t_seq):
        pj = (p_new[j] * inv).astype(BF16).astype(F32)
        out = out + pj * knew[j:j + 1, 0:KV_RANK]
    o_ref[...] = out.astype(BF16)


def _attn_sample(q_seq, knew_seq, cache_t, page_table, layer):
    n_seq, rows, _ = q_seq.shape
    t_seq = knew_seq.shape[1]
    n_pages = page_table.shape[1]
    page = cache_t.shape[3]
    assert n_seq >= PAGE_RING - 1
    grid_spec = pltpu.PrefetchScalarGridSpec(
        num_scalar_prefetch=1,
        grid=(n_seq,),
        in_specs=[pl.BlockSpec((None, rows, LAT_PAD), lambda s, pt: (s, 0, 0)),
                  pl.BlockSpec((None, t_seq, LAT_PAD), lambda s, pt: (s, 0, 0)),
                  pl.BlockSpec(memory_space=pl.ANY)],
        out_specs=pl.BlockSpec((None, rows, KV_RANK), lambda s, pt: (s, 0, 0)),
        scratch_shapes=[pltpu.VMEM((PAGE_RING, n_pages, LAT_DIM, page), F32),
                        pltpu.VMEM((LAT_DIM, n_pages * page), BF16),
                        pltpu.SemaphoreType.DMA((PAGE_RING,))],
    )
    return pl.pallas_call(
        functools.partial(_attn_sample_kernel, n_pages=n_pages, t_seq=t_seq, layer=layer),
        grid_spec=grid_spec,
        out_shape=jax.ShapeDtypeStruct((n_seq, rows, KV_RANK), BF16),
        compiler_params=_params(1),
        name="attn_sample",
    )(page_table, q_seq, knew_seq, cache_t)


def _ffn_kernel(*refs, is_sample, is_last, t_seq, tiles_per_seq):
    (x_ref, hm_ref, ol_ref, p_ref, wuv_ref, wout_ref, gffn_ref, wup_ref, wc_ref, cb_ref,
     wdown_ref, gple_ref, wpg_ref, wpp_ref) = refs[:14]
    refs = refs[14:]
    if is_last:
        gfin_ref, refs = refs[0], refs[1:]
    if is_sample:
        st_ref, x_out, st_out = refs
    else:
        x_out, tail_ref = refs
    tm = x_ref.shape[0]
    d_ff = wdown_ref.shape[0]

    ha = [_dot(ol_ref[hd], wuv_ref[hd]).astype(BF16) for hd in range(A_HEADS)]
    mix = jnp.concatenate([hm_ref[...]] + ha, axis=1)
    x = x_ref[...] + _dot(mix, wout_ref[...])

    hf = _rms(x, gffn_ref[...]).astype(BF16)
    row = lax.broadcasted_iota(jnp.int32, (tm, d_ff), 0)
    if is_sample:
        n_st = (CONV_W - 1) * (tm // t_seq)
        tok = lax.broadcasted_iota(jnp.int32, (tm, n_st), 0)
        srow = lax.broadcasted_iota(jnp.int32, (tm, n_st), 1)
        seq0 = (CONV_W - 1) * (tok // t_seq)
        tpos = tok % t_seq
        prev1 = ((tpos == 0) & (srow == seq0 + 1)).astype(BF16)
        prev2 = ((tpos <= 1) & (srow == seq0 + tpos)).astype(BF16)
        srow_t = lax.broadcasted_iota(jnp.int32, (n_st, tm), 0)
        tok_t = lax.broadcasted_iota(jnp.int32, (n_st, tm), 1)
        keep = (tok_t == t_seq * (srow_t // (CONV_W - 1)) + t_seq - (CONV_W - 1)
                + srow_t % (CONV_W - 1)).astype(BF16)
    else:
        @pl.when(pl.program_id(0) % tiles_per_seq == 0)
        def _():
            tail_ref[...] = jnp.zeros_like(tail_ref)

    def conv(cols):
        u = _dot(hf, wup_ref[:, cols])
        r1 = pltpu.roll(u, 1, 0)
        r2 = pltpu.roll(u, 2, 0)
        if is_sample:
            state = st_ref[:, cols]
            pos = row % t_seq
            u1 = jnp.where(pos >= 1, r1, _exact_onehot_dot(prev1, state))
            u2 = jnp.where(pos >= 2, r2, _exact_onehot_dot(prev2, state))
            st_out[:, cols] = _exact_onehot_dot(keep, u)
        else:
            prev = tail_ref[:, cols]
            last, last2 = prev[SUBLANES - 1:SUBLANES, :], prev[SUBLANES - 2:SUBLANES - 1, :]
            u1 = jnp.where(row >= 1, r1, last)
            u2 = jnp.where(row >= 2, r2, jnp.where(row == 1, last, last2))
            tail_ref[:, cols] = u[tm - SUBLANES:tm, :]
        return (u2 * wc_ref[0:1, cols] + u1 * wc_ref[1:2, cols]) + u * wc_ref[2:3, cols] + cb_ref[:, cols]

    gate = conv(slice(0, d_ff))
    up = conv(slice(d_ff, 2 * d_ff))
    x = x + _dot((gate * _sigmoid(gate) * up).astype(BF16), wdown_ref[...])

    hp = _rms(x, gple_ref[...]).astype(BF16)
    x = x + _sigmoid(_dot(hp, wpg_ref[...])) * _dot(p_ref[...].astype(BF16), wpp_ref[...])
    x_out[...] = _rms(x, gfin_ref[...]) if is_last else x


def _ffn(x, hm, olat, p_all, layer, lw, *, final_gain, conv_state, n_seq, t_seq):
    n, d = x.shape
    d_up = lw['w_up'][0].shape[2]
    is_sample = conv_state is not None
    tm = SAMPLE_FFN_TILE if is_sample else TOK_TILE
    is_last = final_gain is not None
    row = lambda i: (i, 0)
    weights = [lw['w_uv'], lw['w_out'], lw['norm_ffn'], lw['w_up'], lw['ffn_conv'], lw['ffn_conv_b'],
               lw['w_down'], lw['ple_norm'], lw['w_pg'], lw['w_pp']]
    if is_last:
        weights.append(final_gain)
    weight_specs, weight_args = zip(*(_weight(w) for w in weights))
    args = [x, hm, olat, p_all] + list(weight_args)
    in_specs = [pl.BlockSpec((tm, d), row), pl.BlockSpec((tm, 512), row),
                pl.BlockSpec((A_HEADS, tm, KV_RANK), lambda i: (0, i, 0)),
                pl.BlockSpec((None, tm, p_all.shape[2]), lambda i: (layer, i, 0))] + list(weight_specs)
    out_shape = [jax.ShapeDtypeStruct((n, d), F32)]
    out_specs = [pl.BlockSpec((tm, d), row)]
    if is_sample:
        n_st = (CONV_W - 1) * (tm // t_seq)
        args.append(conv_state)
        in_specs.append(pl.BlockSpec((n_st, d_up), row))
        out_shape.append(jax.ShapeDtypeStruct(conv_state.shape, F32))
        out_specs.append(pl.BlockSpec((n_st, d_up), row))
        tiles_per_seq = 1
    else:
        tiles_per_seq = n // n_seq // tm
        out_shape.append(jax.ShapeDtypeStruct((n_seq, SUBLANES, d_up), F32))
        out_specs.append(pl.BlockSpec((None, SUBLANES, d_up), lambda i: (i // tiles_per_seq, 0, 0)))
    return pl.pallas_call(
        functools.partial(_ffn_kernel, is_sample=is_sample, is_last=is_last, t_seq=t_seq,
                          tiles_per_seq=tiles_per_seq),
        grid=(n // tm,),
        in_specs=in_specs,
        out_specs=tuple(out_specs),
        out_shape=tuple(out_shape),
        compiler_params=_params(1),
        name="ffn_sample" if is_sample else "ffn_prompt",
    )(*args)


def _swap_halves(w):
    half = w.shape[-1] // 2
    return jnp.concatenate([w[..., half:], w[..., :half]], axis=-1)


def _stacked_weights(w_out, ffn_w_up, ffn_w_down, ple_w_gate):
    return {'w_out': w_out.astype(BF16), 'w_up': ffn_w_up.astype(BF16),
            'w_down': ffn_w_down.astype(BF16), 'w_pg': ple_w_gate.astype(BF16)}


def _layer_weights(l, stacked, norm_mix, w_in, m_gate_bias, m_norm, mla_q_norm, mla_w_uq, mla_kv_norm, mla_w_uk,
                   mla_w_uv, w_out, norm_ffn, ffn_w_up, ffn_conv, ffn_conv_b, ffn_w_down, ple_norm,
                   ple_w_gate, ple_w_proj):
    d = w_in.shape[1]
    wi = w_in[l]
    sizes = (256, 256, 512, 512, M_HEADS, M_HEADS, Q_RANK, KV_RANK, A_ROPE)
    offs = [0]
    for s in sizes:
        offs.append(offs[-1] + s)
    qm, km, vm, om, im, fm, cq, ckv, kr = (wi[:, offs[i]:offs[i + 1]] for i in range(len(sizes)))
    gate_pad = jnp.zeros((d, LANES - 2 * M_HEADS), wi.dtype)
    w_in_r = jnp.concatenate([fm, im, gate_pad, cq, ckv, kr, _swap_halves(kr), vm, om, qm, km], axis=1)
    uq = mla_w_uq[l].reshape(Q_RANK, A_HEADS, A_NOPE + A_ROPE)
    uq_rope = uq[:, :, A_NOPE:]
    w_uq_r = jnp.concatenate([uq[:, :, :A_NOPE].reshape(Q_RANK, A_HEADS * A_NOPE),
                              jnp.concatenate([uq_rope, _swap_halves(uq_rope)], axis=-1)
                              .reshape(Q_RANK, A_HEADS * LANES)], axis=1)
    row = lambda v: v.reshape(1, -1)
    return {
        'norm_mix': row(norm_mix[l]),
        'w_in': w_in_r.astype(BF16),
        'w_voq_t': w_in_r[:, C_V:C_K].T.astype(BF16),
        'gate_bias': jnp.concatenate([m_gate_bias[l, 1], m_gate_bias[l, 0]]).reshape(SUBLANES, 1),
        'm_norm': m_norm[l],
        'mla_q_norm': row(mla_q_norm[l]),
        'w_uq': w_uq_r.astype(BF16),
        'mla_kv_norm': row(mla_kv_norm[l]),
        'w_uk': jnp.transpose(mla_w_uk[l], (1, 2, 0)).astype(BF16),
        'w_uv': jnp.transpose(mla_w_uv[l], (1, 0, 2)).astype(BF16),
        'w_out': (stacked['w_out'], l),
        'norm_ffn': row(norm_ffn[l]),
        'w_up': (stacked['w_up'], l),
        'ffn_conv': ffn_conv[l],
        'ffn_conv_b': row(ffn_conv_b[l]),
        'w_down': (stacked['w_down'], l),
        'ple_norm': row(ple_norm[l]),
        'w_pg': (stacked['w_pg'], l),
        'w_pp': ple_w_proj[l].astype(BF16),
    }


def _rope_table(pos):
    half = A_ROPE // 2
    freqs = ROPE_THETA ** (-jnp.arange(half, dtype=F32) * 2.0 / A_ROPE)
    ang = pos.astype(F32)[:, None] * freqs[None, :]
    cos, sin = jnp.cos(ang), jnp.sin(ang)
    return jnp.concatenate([cos, cos, -sin, sin], axis=1)


def kernel(x_prompt, x_sample, p_prompt, p_sample, cache_mla, state_mlstm_C, state_mlstm_n, state_mlstm_m,
           state_ffn_conv, page_table, norm_mix, w_in, m_gate_bias, m_norm, mla_q_norm, mla_w_uq, mla_kv_norm,
           mla_w_uk, mla_w_uv, w_out, norm_ffn, ffn_w_up, ffn_conv, ffn_conv_b, ffn_w_down, ple_norm,
           ple_w_gate, ple_w_proj, final_norm):
    depth = w_in.shape[0]
    bp, tp, d = x_prompt.shape
    bs, ts, _ = x_sample.shape
    n_p, n_s = bp * tp, bs * ts
    past_len = page_table.shape[1] * cache_mla.shape[2]
    d_up = ffn_w_up.shape[2]

    tab_p = jnp.tile(_rope_table(jnp.arange(tp)), (bp, 1))
    tab_s = jnp.tile(_rope_table(past_len + jnp.arange(ts)), (bs, 1))
    xp = x_prompt.reshape(n_p, d)
    xs = x_sample.reshape(n_s, d)
    final_gain = final_norm.reshape(1, d)
    cache_t = jnp.swapaxes(cache_mla, 2, 3)

    stacked = _stacked_weights(w_out, ffn_w_up, ffn_w_down, ple_w_gate)
    lat_p, lat_s, c_p, n_pl, m_p, c_s, n_sl, m_s, conv_p, conv_s = ([] for _ in range(10))
    for l in range(depth):
        lw = _layer_weights(l, stacked, norm_mix, w_in, m_gate_bias, m_norm, mla_q_norm, mla_w_uq, mla_kv_norm,
                            mla_w_uk, mla_w_uv, w_out, norm_ffn, ffn_w_up, ffn_conv, ffn_conv_b,
                            ffn_w_down, ple_norm, ple_w_gate, ple_w_proj)
        fin = final_gain if l == depth - 1 else None

        km, qt, vt, ot, gscan, gdiff, dcol, lat, katt, katt_t, qatt_t = _proj(xp, tab_p, lw, True, bp)
        by_seq = lambda a: a.reshape(bp, tp, a.shape[-1])
        hm, state, m_new = _mlstm_prompt(by_seq(km), qt, vt, ot, gscan, gdiff, by_seq(dcol), lw['m_norm'].T, bp)
        c_new = jnp.swapaxes(state[:, :, :M_DV, :], 2, 3)
        n_new = state[:, :, M_DV, :]
        olat = _attn_prompt(qatt_t, katt, katt_t, bp)
        xp, tail = _ffn(xp, hm.reshape(n_p, -1), olat, p_prompt.reshape(depth, n_p, -1), l, lw, final_gain=fin,
                        conv_state=None, n_seq=bp, t_seq=tp)
        lat_p.append(lat.reshape(bp, tp, LAT_DIM))
        c_p.append(c_new)
        n_pl.append(n_new)
        m_p.append(m_new[:, :M_HEADS, 0])
        conv_p.append(tail[:, SUBLANES - (CONV_W - 1):])

        qk, v, o, gates, lat, katt, qatt = _proj(xs, tab_s, lw, False, 1)
        m_prev_rows = jnp.repeat(state_mlstm_m[l].T, ts, axis=1)
        hm, c_new, n_new_t, m_rows = _mlstm_sample(qk, v, o, gates, m_prev_rows, lw['m_norm'],
                                                   state_mlstm_C, jnp.transpose(state_mlstm_n[l], (1, 0, 2)),
                                                   l, ts)
        q_seq = jnp.transpose(qatt.reshape(A_HEADS, bs, ts, LAT_PAD), (1, 0, 2, 3)).reshape(bs, A_HEADS * ts, LAT_PAD)
        o_seq = _attn_sample(q_seq, katt.reshape(bs, ts, LAT_PAD), cache_t, page_table, l)
        olat = jnp.transpose(o_seq.reshape(bs, A_HEADS, ts, KV_RANK), (1, 0, 2, 3)).reshape(A_HEADS, n_s, KV_RANK)
        xs, st_new = _ffn(xs, hm, olat, p_sample.reshape(depth, n_s, -1), l, lw, final_gain=fin,
                          conv_state=state_ffn_conv[l].reshape(bs * (CONV_W - 1), d_up), n_seq=bs, t_seq=ts)
        lat_s.append(lat.reshape(bs, ts, LAT_DIM))
        c_s.append(c_new)
        n_sl.append(jnp.transpose(n_new_t, (1, 0, 2)))
        m_s.append(m_rows[:, ts - 1::ts].T)
        conv_s.append(st_new.reshape(bs, CONV_W - 1, d_up))

    return (xp.reshape(bp, tp, d), xs.reshape(bs, ts, d), jnp.stack(lat_p), jnp.stack(lat_s),
            jnp.stack(c_p), jnp.stack(n_pl), jnp.stack(m_p), jnp.stack(c_s), jnp.stack(n_sl), jnp.stack(m_s),
            jnp.stack(conv_p), jnp.stack(conv_s))
```

```python
import functools

import jax
import jax.numpy as jnp
from jax import lax
from jax.experimental import pallas as pl
from jax.experimental.pallas import tpu as pltpu

F32 = jnp.float32
BF16 = jnp.bfloat16

M_HEADS = 4
M_DK = 64
M_DV = 128
A_HEADS = 4
A_NOPE = 128
A_ROPE = 64
KV_RANK = 256
Q_RANK = 384
LAT_DIM = KV_RANK + A_ROPE
GATE_CAP = 15.0
ROPE_THETA = 10000.0
EPS = 1e-6
CONV_W = 3
ATT_SCALE = (A_NOPE + A_ROPE) ** -0.5
LOG2_E = 1.4426950408889634
Q_SCALE = ATT_SCALE * LOG2_E

LANES = 128
SUBLANES = 8
VMEM_LIMIT_BYTES = 56 * 1024 * 1024

LAT_PAD = 3 * LANES
TOK_TILE = 256
MLSTM_CHUNK = 256
ATT_BLOCK = 1024
SAMPLE_GROUP = 32
PAGE_RING = 3
SAMPLE_FFN_TILE = 128
STATE_ROWS = M_DV + SUBLANES

C_G, C_CQ, C_CKV, C_KR = 0, 128, 512, 768
C_V, C_O, C_Q, C_K = 896, 1408, 1920, 2176
IN_COLS = 2432


def _const_spec(shape):
    nd = len(shape)
    return pl.BlockSpec(shape, lambda *_: (0,) * nd, pipeline_mode=pl.Buffered(1))


def _weight(w):
    if not isinstance(w, tuple):
        return _const_spec(w.shape), w
    arr, layer = w
    tail = (0,) * (arr.ndim - 1)
    return pl.BlockSpec((None,) + arr.shape[1:], lambda *_: (layer,) + tail, pipeline_mode=pl.Buffered(1)), arr


def _params(n_axes):
    return pltpu.CompilerParams(dimension_semantics=("arbitrary",) * n_axes,
                                vmem_limit_bytes=VMEM_LIMIT_BYTES)


def _rms(x, g):
    return x * lax.rsqrt(jnp.mean(x * x, axis=-1, keepdims=True) + EPS) * g


def _sigmoid(x):
    return 1.0 / (1.0 + jnp.exp(-x))


def _dot(a, b):
    return jnp.dot(a, b, preferred_element_type=F32)


def _dot_nt(a, b):
    return lax.dot_general(a, b, (((1,), (1,)), ((), ())), preferred_element_type=F32)


def _fold_halves(p):
    return p + pltpu.roll(p, 64, 1)


def _proj_kernel(x_ref, tab_ref, gmix_ref, win_ref, gb_ref, gq_ref, wuq_ref, gkv_ref, wuk_ref,
                 *refs, feature_major):
    if feature_major:
        (wvoq_ref, km_ref, qt_ref, vt_ref, ot_ref, gscan_ref, gdiff_ref, dcol_ref,
         lat_ref, katt_ref, katt_t_ref, qatt_ref) = refs
    else:
        qk_ref, v_ref, o_ref, gates_ref, lat_ref, katt_ref, qatt_ref = refs
    h = _rms(x_ref[...], gmix_ref[...]).astype(BF16)
    z = _dot(h, win_ref[:, 0:C_V])

    gt = z[:, C_G:C_G + LANES].T[0:SUBLANES, :] + gb_ref[...]
    capped = GATE_CAP * jnp.tanh(gt / GATE_CAP)
    log_sig = jnp.minimum(capped, 0.0) - jnp.log1p(jnp.exp(-jnp.abs(capped)))
    row = lax.broadcasted_iota(jnp.int32, gt.shape, 0)
    gates = jnp.where(row < M_HEADS, log_sig, capped)
    if feature_major:
        b, d, e = _gate_scans(gates, None)
        gscan_ref[...] = jnp.where(row < M_HEADS, b, pltpu.roll(e, M_HEADS, 0))
        d = jnp.where(row < M_HEADS, d, 0.0)
        gdiff_ref[...] = d
        dcol_ref[...] = _to_cols((d,), d.shape[1])
    else:
        gates_ref[...] = gates

    tab = tab_ref[...]
    lane = lax.broadcasted_iota(jnp.int32, tab.shape, 1)
    ckv_n = _rms(z[:, C_CKV:C_CKV + KV_RANK], gkv_ref[...])
    rk = _fold_halves(z[:, C_KR:C_KR + LANES] * tab)
    lat_ref[:, 0:KV_RANK] = ckv_n
    lat_ref[:, KV_RANK:LAT_DIM] = rk[:, 0:A_ROPE]
    k_cat = jnp.concatenate([ckv_n, jnp.where(lane < A_ROPE, rk, 0.0)], axis=1)
    katt_ref[...] = k_cat.astype(BF16)
    if feature_major:
        katt_t_ref[...] = k_cat.T.astype(BF16)

    qn = _rms(z[:, C_CQ:C_CQ + Q_RANK], gq_ref[...]).astype(BF16)
    qa = _dot(qn, wuq_ref[...])
    for hd in range(A_HEADS):
        q_lat = _dot(qa[:, hd * A_NOPE:(hd + 1) * A_NOPE].astype(BF16), wuk_ref[hd])
        rq = _fold_halves(qa[:, 512 + hd * LANES:512 + (hd + 1) * LANES] * tab)
        q_cat = jnp.concatenate([q_lat, jnp.where(lane < A_ROPE, rq, 0.0)], axis=1) * Q_SCALE
        qatt_ref[hd] = (q_cat.T if feature_major else q_cat).astype(BF16)

    k_m = (_dot(h, win_ref[:, C_K:C_K + 256]) * (M_DK ** -0.5)).astype(BF16)
    if feature_major:
        zt = _dot_nt(wvoq_ref[...], h)
        vt_ref[...] = zt[0:512, :].astype(BF16)
        ot_ref[...] = zt[512:1024, :]
        qt_ref[...] = zt[1024:1280, :].astype(BF16)
        km_ref[...] = k_m
    else:
        z2 = _dot(h, win_ref[:, C_V:C_K])
        v_ref[...] = z2[:, 0:512].astype(BF16)
        o_ref[...] = z2[:, 512:1024]
        qk_ref[:, 0:256] = z2[:, 1024:1280].astype(BF16)
        qk_ref[:, 256:512] = k_m


def _proj(x, tab, lw, feature_major, gate_seqs):
    n = x.shape[0]
    tm = TOK_TILE
    tiles_per_seq = n // gate_seqs // tm
    row = lambda i: (i, 0)
    col = lambda i: (0, i)
    band = lambda i: (i // tiles_per_seq, i % tiles_per_seq)
    gate_rows = jax.ShapeDtypeStruct((gate_seqs * SUBLANES, n // gate_seqs), F32)
    if feature_major:
        assert tm == MLSTM_CHUNK
        out_shape = [
            jax.ShapeDtypeStruct((n, 256), BF16),
            jax.ShapeDtypeStruct((256, n), BF16),
            jax.ShapeDtypeStruct((512, n), BF16),
            jax.ShapeDtypeStruct((512, n), F32),
            gate_rows,
            gate_rows,
            jax.ShapeDtypeStruct((n, LANES), F32),
        ]
        out_specs = [
            pl.BlockSpec((tm, 256), row),
            pl.BlockSpec((256, tm), col),
            pl.BlockSpec((512, tm), col),
            pl.BlockSpec((512, tm), col),
            pl.BlockSpec((SUBLANES, tm), band),
            pl.BlockSpec((SUBLANES, tm), band),
            pl.BlockSpec((tm, LANES), row),
        ]
    else:
        out_shape = [
            jax.ShapeDtypeStruct((n, 512), BF16),
            jax.ShapeDtypeStruct((n, 512), BF16),
            jax.ShapeDtypeStruct((n, 512), F32),
            gate_rows,
        ]
        out_specs = [
            pl.BlockSpec((tm, 512), row),
            pl.BlockSpec((tm, 512), row),
            pl.BlockSpec((tm, 512), row),
            pl.BlockSpec((SUBLANES, tm), band),
        ]
    out_shape += [jax.ShapeDtypeStruct((n, LAT_DIM), F32),
                  jax.ShapeDtypeStruct((n, LAT_PAD), BF16)]
    out_specs += [pl.BlockSpec((tm, LAT_DIM), row), pl.BlockSpec((tm, LAT_PAD), row)]
    if feature_major:
        out_shape += [jax.ShapeDtypeStruct((LAT_PAD, n), BF16),
                      jax.ShapeDtypeStruct((A_HEADS, LAT_PAD, n), BF16)]
        out_specs += [pl.BlockSpec((LAT_PAD, tm), col),
                      pl.BlockSpec((A_HEADS, LAT_PAD, tm), lambda i: (0, 0, i))]
    else:
        out_shape.append(jax.ShapeDtypeStruct((A_HEADS, n, LAT_PAD), BF16))
        out_specs.append(pl.BlockSpec((A_HEADS, tm, LAT_PAD), lambda i: (0, i, 0)))
    return pl.pallas_call(
        functools.partial(_proj_kernel, feature_major=feature_major),
        grid=(n // tm,),
        in_specs=[
            pl.BlockSpec((tm, x.shape[1]), row),
            pl.BlockSpec((tm, LANES), row),
            _const_spec(lw['norm_mix'].shape),
            _const_spec(lw['w_in'].shape),
            _const_spec(lw['gate_bias'].shape),
            _const_spec(lw['mla_q_norm'].shape),
            _const_spec(lw['w_uq'].shape),
            _const_spec(lw['mla_kv_norm'].shape),
            _const_spec(lw['w_uk'].shape),
        ] + ([_const_spec(lw['w_voq_t'].shape)] if feature_major else []),
        out_specs=tuple(out_specs),
        out_shape=tuple(out_shape),
        compiler_params=_params(1),
        name="proj_prompt" if feature_major else "proj_sample",
    )(x, tab, lw['norm_mix'], lw['w_in'], lw['gate_bias'], lw['mla_q_norm'], lw['w_uq'],
      lw['mla_kv_norm'], lw['w_uk'], *([lw['w_voq_t']] if feature_major else []))


def _seg_scan(x, op, fill, seg):
    n = x.shape[1]
    lane = lax.broadcasted_iota(jnp.int32, x.shape, 1)
    pos = lane if seg is None else lane % seg
    span = n if seg is None else seg
    s = 1
    while s < span:
        x = op(x, jnp.where(pos >= s, pltpu.roll(x, s, 1), fill))
        s *= 2
    return x


def _gate_scans(gates, seg):
    ig = pltpu.roll(gates, M_HEADS, 0)
    b = _seg_scan(gates, jnp.add, 0.0, seg)
    d = ig - b
    e = _seg_scan(d, jnp.maximum, -jnp.inf, seg)
    return b, d, e


def _gate_rows(gates, m_prev, seg):
    b, d, e = _gate_scans(gates, seg)
    return b, d, pltpu.roll(gates, M_HEADS, 0), b + jnp.maximum(m_prev, e)


def _to_cols(row_blocks, n_tok):
    pad = jnp.zeros((LANES - SUBLANES * len(row_blocks), n_tok), F32)
    return jnp.concatenate(list(row_blocks) + [pad], axis=0).T


def _head_out(hh, og, gain):
    hn = hh * lax.rsqrt(jnp.mean(hh * hh, axis=-1, keepdims=True) + EPS) * gain
    return (hn * _sigmoid(og)).astype(BF16)


def _mlstm_prompt_kernel(*refs, n_seq):
    km_ref, gscan_ref, gdiff_ref, dcol_ref, gain_ref = refs[:5]
    qt_refs = refs[5:5 + n_seq]
    vt_refs = refs[5 + n_seq:5 + 2 * n_seq]
    ot_refs = refs[5 + 2 * n_seq:5 + 3 * n_seq]
    hm_ref, s_ref, m_ref = refs[5 + 3 * n_seq:]
    L = km_ref.shape[1]

    @pl.when(pl.program_id(0) == 0)
    def _():
        s_ref[...] = jnp.zeros_like(s_ref)
        m_ref[...] = jnp.zeros_like(m_ref)

    ki = lax.broadcasted_iota(jnp.int32, (L, L), 0)
    qi = lax.broadcasted_iota(jnp.int32, (L, L), 1)
    causal = ki <= qi
    pad_rows = jnp.zeros((STATE_ROWS - M_DV - 1, L), F32)

    for sq in range(n_seq):
        band = slice(sq * SUBLANES, (sq + 1) * SUBLANES)
        scan = gscan_ref[band, :]
        diff = gdiff_ref[band, :]
        ks = [km_ref[sq, :, hd * M_DK:(hd + 1) * M_DK] for hd in range(M_HEADS)]
        qts = [qt_refs[sq][hd * M_DK:(hd + 1) * M_DK, :] for hd in range(M_HEADS)]
        states = [s_ref[sq, hd] for hd in range(M_HEADS)]
        kq = [_dot(ks[hd], qts[hd]) for hd in range(M_HEADS)]
        q_state = [_dot(states[hd].astype(BF16), qts[hd]) for hd in range(M_HEADS)]

        for hd in range(M_HEADS):
            vt = vt_refs[sq][hd * M_DV:(hd + 1) * M_DV, :]
            b_r = scan[hd:hd + 1, :]
            e_r = scan[M_HEADS + hd:M_HEADS + hd + 1, :]
            d_r = diff[hd:hd + 1, :]
            d_c = dcol_ref[sq, :, hd:hd + 1]
            mp = m_ref[sq, hd:hd + 1, 0:1]
            mx_r = jnp.maximum(mp, e_r)
            mx_last = mx_r[:, L - 1:L]

            st = kq[hd] * jnp.exp(jnp.where(causal, d_c - mx_r, -jnp.inf))
            a_r = jnp.exp(mp - mx_r)
            num = _dot(vt, st.astype(BF16)) + a_r * q_state[hd][0:M_DV, :]
            nq = jnp.sum(st, axis=0, keepdims=True) + a_r * q_state[hd][M_DV:M_DV + 1, :]
            inv = 1.0 / jnp.maximum(jnp.abs(nq), jnp.exp(-(b_r + mx_r)))
            rs = lax.rsqrt(inv * inv * jnp.mean(num * num, axis=0, keepdims=True) + EPS)
            ht = num * (inv * rs) * gain_ref[:, hd:hd + 1] * _sigmoid(ot_refs[sq][hd * M_DV:(hd + 1) * M_DV, :])
            hm_ref[sq, :, hd * M_DV:(hd + 1) * M_DV] = ht.T.astype(BF16)

            w_r = jnp.exp(d_r - mx_last)
            lhs = jnp.concatenate([vt.astype(F32) * w_r, w_r, pad_rows], axis=0).astype(BF16)
            s_ref[sq, hd] = jnp.exp(mp - mx_last) * states[hd] + _dot(lhs, ks[hd])
            m_ref[sq, hd:hd + 1, :] = jnp.broadcast_to(b_r[:, L - 1:L] + mx_last, (1, LANES))


def _mlstm_prompt(km, qt, vt, ot, gscan, gdiff, dcol, gain_t, n_seq):
    t = km.shape[1]
    L = MLSTM_CHUNK
    nc = t // L
    tok = lambda c: (0, c, 0)
    whole4 = lambda c: (0, 0, 0, 0)
    whole3 = lambda c: (0, 0, 0)

    def seq_cols(rows):
        return [pl.BlockSpec((rows, L), lambda c, sq=sq: (0, sq * nc + c)) for sq in range(n_seq)]

    return pl.pallas_call(
        functools.partial(_mlstm_prompt_kernel, n_seq=n_seq),
        grid=(nc,),
        in_specs=[
            pl.BlockSpec((n_seq, L, 256), tok),
            pl.BlockSpec((n_seq * SUBLANES, L), lambda c: (0, c)),
            pl.BlockSpec((n_seq * SUBLANES, L), lambda c: (0, c)),
            pl.BlockSpec((n_seq, L, LANES), tok),
            _const_spec(gain_t.shape),
        ] + seq_cols(256) + seq_cols(512) + seq_cols(512),
        out_specs=(
            pl.BlockSpec((n_seq, L, 512), tok),
            pl.BlockSpec((n_seq, M_HEADS, STATE_ROWS, M_DK), whole4),
            pl.BlockSpec((n_seq, SUBLANES, LANES), whole3),
        ),
        out_shape=(
            jax.ShapeDtypeStruct((n_seq, t, 512), BF16),
            jax.ShapeDtypeStruct((n_seq, M_HEADS, STATE_ROWS, M_DK), F32),
            jax.ShapeDtypeStruct((n_seq, SUBLANES, LANES), F32),
        ),
        compiler_params=_params(1),
        name="mlstm_prompt",
    )(km, gscan, gdiff, dcol, gain_t, *([qt] * n_seq), *([vt] * n_seq), *([ot] * n_seq))


def _exact_onehot_dot(onehot, x):
    hi = x.astype(BF16)
    r1 = x - hi.astype(F32)
    mid = r1.astype(BF16)
    lo = (r1 - mid.astype(F32)).astype(BF16)
    return (_dot(onehot, hi) + _dot(onehot, mid)) + _dot(onehot, lo)


def _mlstm_sample_kernel(qk_ref, v_ref, o_ref, g_ref, mp_ref, gain_ref, c0_ref, n0_ref,
                         hm_ref, c_ref, n_ref, m_ref, cols_ref, *, t_seq):
    L = qk_ref.shape[0]
    G = L // t_seq
    mp_rows = jnp.concatenate([mp_ref[...], jnp.zeros((SUBLANES - M_HEADS, L), F32)], axis=0)
    b, d, ig, m = _gate_rows(g_ref[...], mp_rows, t_seq)
    m_ref[...] = m[0:M_HEADS, :]

    lane = lax.broadcasted_iota(jnp.int32, b.shape, 1)
    pos = lane % t_seq

    def seg_last(x):
        s = 1
        while s < t_seq:
            x = jnp.where(pos >= t_seq - s, x, pltpu.roll(x, L - s, 1))
            s *= 2
        return x

    b_last = seg_last(b)
    m_new = seg_last(m)
    w_rows = jnp.exp(b_last - b + ig - m_new)
    decay_rows = jnp.exp(b_last + mp_rows - m_new)
    cols_ref[...] = _to_cols((b, m, mp_rows, w_rows, decay_rows), L)
    cols = cols_ref[...]
    seq_cols = cols_ref[pl.ds(0, G, stride=t_seq), :]

    qi = lax.broadcasted_iota(jnp.int32, (L, L), 0)
    ki = lax.broadcasted_iota(jnp.int32, (L, L), 1)
    mask = (ki <= qi) & (ki // t_seq == qi // t_seq)
    er = lax.broadcasted_iota(jnp.int32, (L, G * M_DK), 0)
    ec = lax.broadcasted_iota(jnp.int32, (L, G * M_DK), 1)
    own = (ec // M_DK) == (er // t_seq)
    tok_of_seq = (lax.broadcasted_iota(jnp.int32, (L, G), 0) // t_seq
                  == lax.broadcasted_iota(jnp.int32, (L, G), 1)).astype(BF16)
    seq_of_tok = (lax.broadcasted_iota(jnp.int32, (G, L), 1) // t_seq
                  == lax.broadcasted_iota(jnp.int32, (G, L), 0)).astype(BF16)

    def expand(x):
        x2 = jnp.concatenate([x, x], axis=1)
        return jnp.where(own, jnp.tile(x2, (1, G // 2)), 0.0)

    for hd in range(M_HEADS):
        q = qk_ref[:, hd * M_DK:(hd + 1) * M_DK]
        k = qk_ref[:, 256 + hd * M_DK:256 + (hd + 1) * M_DK]
        v = v_ref[:, hd * M_DV:(hd + 1) * M_DV]
        b_c = cols[:, hd:hd + 1]
        m_c = cols[:, 8 + hd:9 + hd]
        mp_c = cols[:, 16 + hd:17 + hd]
        w_c = cols[:, 24 + hd:25 + hd]
        decay_seq = seq_cols[:, 32 + hd:33 + hd]
        n_old = n0_ref[hd]
        c_old = c0_ref[:, hd].reshape(G * M_DK, M_DV)

        p = jnp.exp(jnp.where(mask, b_c + d[hd:hd + 1, :] - m_c, -jnp.inf))
        s = _dot_nt(q, k) * p
        a_c = jnp.exp(b_c + mp_c - m_c)
        qf = q.astype(F32)
        num = _dot(s.astype(BF16), v) + a_c * _dot(expand(qf).astype(BF16), c_old.astype(BF16))
        n_tok = _exact_onehot_dot(tok_of_seq, n_old)
        nq = jnp.sum(s, axis=1, keepdims=True) + a_c * jnp.sum(qf * n_tok, axis=1, keepdims=True)
        den = jnp.maximum(jnp.abs(nq), jnp.exp(-m_c))
        sl = slice(hd * M_DV, (hd + 1) * M_DV)
        hm_ref[:, sl] = _head_out(num / den, o_ref[:, sl], gain_ref[hd:hd + 1, :])

        kw = k.astype(F32) * w_c
        upd = _dot(expand(kw).T.astype(BF16), v)
        for g in range(G):
            c_ref[g, hd] = decay_seq[g:g + 1, :] * c0_ref[g, hd] + upd[g * M_DK:(g + 1) * M_DK, :]
        n_ref[hd] = decay_seq * n_old + _dot(seq_of_tok, kw.astype(BF16))


def _mlstm_sample(qk, v, o, gates, m_prev_rows, gain, c0, n0_t, layer, t_seq):
    n = qk.shape[0]
    G = SAMPLE_GROUP
    L = G * t_seq
    n_seq = n // t_seq
    tok = lambda i: (i, 0)
    return pl.pallas_call(
        functools.partial(_mlstm_sample_kernel, t_seq=t_seq),
        grid=(n // L,),
        in_specs=[
            pl.BlockSpec((L, 512), tok),
            pl.BlockSpec((L, 512), tok),
            pl.BlockSpec((L, 512), tok),
            pl.BlockSpec((SUBLANES, L), lambda i: (0, i)),
            pl.BlockSpec((M_HEADS, L), lambda i: (0, i)),
            _const_spec(gain.shape),
            pl.BlockSpec((None, G, M_HEADS, M_DK, M_DV), lambda i: (layer, i, 0, 0, 0)),
            pl.BlockSpec((M_HEADS, G, M_DK), lambda i: (0, i, 0)),
        ],
        out_specs=(
            pl.BlockSpec((L, 512), tok),
            pl.BlockSpec((G, M_HEADS, M_DK, M_DV), lambda i: (i, 0, 0, 0)),
            pl.BlockSpec((M_HEADS, G, M_DK), lambda i: (0, i, 0)),
            pl.BlockSpec((M_HEADS, L), lambda i: (0, i)),
        ),
        out_shape=(
            jax.ShapeDtypeStruct((n, 512), BF16),
            jax.ShapeDtypeStruct((n_seq, M_HEADS, M_DK, M_DV), F32),
            jax.ShapeDtypeStruct((M_HEADS, n_seq, M_DK), F32),
            jax.ShapeDtypeStruct((M_HEADS, n), F32),
        ),
        scratch_shapes=[pltpu.VMEM((L, LANES), F32)],
        compiler_params=_params(1),
        name="mlstm_sample",
    )(qk, v, o, gates, m_prev_rows, gain, c0, n0_t)


def _attn_prompt_kernel(qt_ref, k_ref, kt_ref, o_ref, m_scr, l_scr, acc_scr, s0_scr):
    tq = qt_ref.shape[2]
    qb = pl.program_id(1)
    m_scr[...] = jnp.full_like(m_scr, -jnp.inf)
    l_scr[...] = jnp.zeros_like(l_scr)
    acc_scr[...] = jnp.zeros_like(acc_scr)

    def keys(kb):
        return k_ref[pl.ds(pl.multiple_of(kb * tq, tq), tq), :]

    s0_scr[...] = _dot(keys(0), qt_ref[0])

    def block(kb, last):
        kblk = keys(kb)
        vt = kt_ref[0:KV_RANK, pl.ds(pl.multiple_of(kb * tq, tq), tq)]
        scores = {0: s0_scr[...]}
        s0_next = None
        for hd in range(A_HEADS):
            if hd + 1 < A_HEADS:
                scores[hd + 1] = _dot(kblk, qt_ref[hd + 1])
            elif not last:
                s0_next = _dot(keys(kb + 1), qt_ref[0])
            st = scores.pop(hd)
            if last:
                kpos = lax.broadcasted_iota(jnp.int32, st.shape, 0)
                qpos = lax.broadcasted_iota(jnp.int32, st.shape, 1)
                st = jnp.where(kpos <= qpos, st, -jnp.inf)
            m_old = m_scr[hd:hd + 1, :]
            m_new = jnp.maximum(m_old, jnp.max(st, axis=0, keepdims=True))
            alpha = jnp.exp2(m_old - m_new)
            p = jnp.exp2(st - m_new)
            l_scr[hd:hd + 1, :] = alpha * l_scr[hd:hd + 1, :] + jnp.sum(p, axis=0, keepdims=True)
            acc_scr[hd] = alpha * acc_scr[hd] + _dot(vt, p.astype(BF16))
            m_scr[hd:hd + 1, :] = m_new
        if not last:
            s0_scr[...] = s0_next

    def body(kb, carry):
        block(kb, False)
        return carry

    lax.fori_loop(0, qb, body, 0)
    block(qb, True)
    for hd in range(A_HEADS):
        o_ref[hd] = (acc_scr[hd] / l_scr[hd:hd + 1, :]).T.astype(BF16)


def _attn_prompt(qatt_t, katt, katt_t, n_seq):
    n = katt.shape[0]
    t = n // n_seq
    tq = ATT_BLOCK
    nq = t // tq
    return pl.pallas_call(
        _attn_prompt_kernel,
        grid=(n_seq, nq),
        in_specs=[
            pl.BlockSpec((A_HEADS, LAT_PAD, tq), lambda b, i: (0, 0, b * nq + i)),
            pl.BlockSpec((t, LAT_PAD), lambda b, i: (b, 0)),
            pl.BlockSpec((LAT_PAD, t), lambda b, i: (0, b)),
        ],
        out_specs=pl.BlockSpec((A_HEADS, tq, KV_RANK), lambda b, i: (0, b * nq + i, 0)),
        out_shape=jax.ShapeDtypeStruct((A_HEADS, n, KV_RANK), BF16),
        scratch_shapes=[pltpu.VMEM((SUBLANES, tq), F32), pltpu.VMEM((SUBLANES, tq), F32),
                        pltpu.VMEM((A_HEADS, KV_RANK, tq), F32), pltpu.VMEM((tq, tq), F32)],
        compiler_params=_params(2),
        name="attn_prompt",
    )(qatt_t, katt, katt_t)


def _attn_sample_kernel(pt_ref, q_ref, knew_ref, cache_ref, o_ref, pages, kbuf, sems, *, n_pages, t_seq, layer):
    s = pl.program_id(0)
    n_seq = pl.num_programs(0)
    page = pages.shape[3]

    def page_copy(seq, j):
        slot = seq % PAGE_RING
        return pltpu.make_async_copy(cache_ref.at[layer, pt_ref[seq, j]], pages.at[slot, j], sems.at[slot])

    def request(seq):
        for j in range(n_pages):
            page_copy(seq, j).start(priority=j % 2)

    @pl.when(s == 0)
    def _():
        for ahead in range(PAGE_RING - 1):
            request(ahead)

    for j in range(n_pages):
        page_copy(s, j).wait()

    @pl.when(s + PAGE_RING - 1 < n_seq)
    def _():
        request(s + PAGE_RING - 1)

    slot = s % PAGE_RING
    for j in range(n_pages):
        kbuf[:, j * page:(j + 1) * page] = pages[slot, j].astype(BF16)

    q = q_ref[...]
    rows = q.shape[0]
    s_past = _dot(q[:, 0:LAT_DIM], kbuf[...])
    qf = q.astype(F32)
    knew = knew_ref[...].astype(F32)
    tq = lax.broadcasted_iota(jnp.int32, (rows, 1), 0) % t_seq
    s_new = []
    for j in range(t_seq):
        sj = jnp.sum(qf * knew[j:j + 1, :], axis=1, keepdims=True)
        s_new.append(jnp.where(tq >= j, sj, -jnp.inf))
    m = jnp.max(s_past, axis=1, keepdims=True)
    for sj in s_new:
        m = jnp.maximum(m, sj)
    p_past = jnp.exp2(s_past - m)
    p_new = [jnp.exp2(sj - m) for sj in s_new]
    denom = jnp.sum(p_past, axis=1, keepdims=True)
    for pj in p_new:
        denom = denom + pj
    inv = 1.0 / denom
    out = _dot_nt((p_past * inv).astype(BF16), kbuf[0:KV_RANK, :])
    for j in range(t_seq):
        pj = (p_new[j] * inv).astype(BF16).astype(F32)
        out = out + pj * knew[j:j + 1, 0:KV_RANK]
    o_ref[...] = out.astype(BF16)


def _attn_sample(q_seq, knew_seq, cache_t, page_table, layer):
    n_seq, rows, _ = q_seq.shape
    t_seq = knew_seq.shape[1]
    n_pages = page_table.shape[1]
    page = cache_t.shape[3]
    assert n_seq >= PAGE_RING - 1
    grid_spec = pltpu.PrefetchScalarGridSpec(
        num_scalar_prefetch=1,
        grid=(n_seq,),
        in_specs=[pl.BlockSpec((None, rows, LAT_PAD), lambda s, pt: (s, 0, 0)),
                  pl.BlockSpec((None, t_seq, LAT_PAD), lambda s, pt: (s, 0, 0)),
                  pl.BlockSpec(memory_space=pl.ANY)],
        out_specs=pl.BlockSpec((None, rows, KV_RANK), lambda s, pt: (s, 0, 0)),
        scratch_shapes=[pltpu.VMEM((PAGE_RING, n_pages, LAT_DIM, page), F32),
                        pltpu.VMEM((LAT_DIM, n_pages * page), BF16),
                        pltpu.SemaphoreType.DMA((PAGE_RING,))],
    )
    return pl.pallas_call(
        functools.partial(_attn_sample_kernel, n_pages=n_pages, t_seq=t_seq, layer=layer),
        grid_spec=grid_spec,
        out_shape=jax.ShapeDtypeStruct((n_seq, rows, KV_RANK), BF16),
        compiler_params=_params(1),
        name="attn_sample",
    )(page_table, q_seq, knew_seq, cache_t)


def _ffn_kernel(*refs, is_sample, is_last, t_seq, tiles_per_seq):
    (x_ref, hm_ref, ol_ref, p_ref, wuv_ref, wout_ref, gffn_ref, wup_ref, wc_ref, cb_ref,
     wdown_ref, gple_ref, wpg_ref, wpp_ref) = refs[:14]
    refs = refs[14:]
    if is_last:
        gfin_ref, refs = refs[0], refs[1:]
    if is_sample:
        st_ref, x_out, st_out = refs
    else:
        x_out, tail_ref = refs
    tm = x_ref.shape[0]
    d_ff = wdown_ref.shape[0]

    ha = [_dot(ol_ref[hd], wuv_ref[hd]).astype(BF16) for hd in range(A_HEADS)]
    mix = jnp.concatenate([hm_ref[...]] + ha, axis=1)
    x = x_ref[...] + _dot(mix, wout_ref[...])

    hf = _rms(x, gffn_ref[...]).astype(BF16)
    row = lax.broadcasted_iota(jnp.int32, (tm, d_ff), 0)
    if is_sample:
        n_st = (CONV_W - 1) * (tm // t_seq)
        tok = lax.broadcasted_iota(jnp.int32, (tm, n_st), 0)
        srow = lax.broadcasted_iota(jnp.int32, (tm, n_st), 1)
        seq0 = (CONV_W - 1) * (tok // t_seq)
        tpos = tok % t_seq
        prev1 = ((tpos == 0) & (srow == seq0 + 1)).astype(BF16)
        prev2 = ((tpos <= 1) & (srow == seq0 + tpos)).astype(BF16)
        srow_t = lax.broadcasted_iota(jnp.int32, (n_st, tm), 0)
        tok_t = lax.broadcasted_iota(jnp.int32, (n_st, tm), 1)
        keep = (tok_t == t_seq * (srow_t // (CONV_W - 1)) + t_seq - (CONV_W - 1)
                + srow_t % (CONV_W - 1)).astype(BF16)
    else:
        @pl.when(pl.program_id(0) % tiles_per_seq == 0)
        def _():
            tail_ref[...] = jnp.zeros_like(tail_ref)

    def conv(cols):
        u = _dot(hf, wup_ref[:, cols])
        r1 = pltpu.roll(u, 1, 0)
        r2 = pltpu.roll(u, 2, 0)
        if is_sample:
            state = st_ref[:, cols]
            pos = row % t_seq
            u1 = jnp.where(pos >= 1, r1, _exact_onehot_dot(prev1, state))
            u2 = jnp.where(pos >= 2, r2, _exact_onehot_dot(prev2, state))
            st_out[:, cols] = _exact_onehot_dot(keep, u)
        else:
            prev = tail_ref[:, cols]
            last, last2 = prev[SUBLANES - 1:SUBLANES, :], prev[SUBLANES - 2:SUBLANES - 1, :]
            u1 = jnp.where(row >= 1, r1, last)
            u2 = jnp.where(row >= 2, r2, jnp.where(row == 1, last, last2))
            tail_ref[:, cols] = u[tm - SUBLANES:tm, :]
        return (u2 * wc_ref[0:1, cols] + u1 * wc_ref[1:2, cols]) + u * wc_ref[2:3, cols] + cb_ref[:, cols]

    gate = conv(slice(0, d_ff))
    up = conv(slice(d_ff, 2 * d_ff))
    x = x + _dot((gate * _sigmoid(gate) * up).astype(BF16), wdown_ref[...])

    hp = _rms(x, gple_ref[...]).astype(BF16)
    x = x + _sigmoid(_dot(hp, wpg_ref[...])) * _dot(p_ref[...].astype(BF16), wpp_ref[...])
    x_out[...] = _rms(x, gfin_ref[...]) if is_last else x


def _ffn(x, hm, olat, p_all, layer, lw, *, final_gain, conv_state, n_seq, t_seq):
    n, d = x.shape
    d_up = lw['w_up'][0].shape[2]
    is_sample = conv_state is not None
    tm = SAMPLE_FFN_TILE if is_sample else TOK_TILE
    is_last = final_gain is not None
    row = lambda i: (i, 0)
    weights = [lw['w_uv'], lw['w_out'], lw['norm_ffn'], lw['w_up'], lw['ffn_conv'], lw['ffn_conv_b'],
               lw['w_down'], lw['ple_norm'], lw['w_pg'], lw['w_pp']]
    if is_last:
        weights.append(final_gain)
    weight_specs, weight_args = zip(*(_weight(w) for w in weights))
    args = [x, hm, olat, p_all] + list(weight_args)
    in_specs = [pl.BlockSpec((tm, d), row), pl.BlockSpec((tm, 512), row),
                pl.BlockSpec((A_HEADS, tm, KV_RANK), lambda i: (0, i, 0)),
                pl.BlockSpec((None, tm, p_all.shape[2]), lambda i: (layer, i, 0))] + list(weight_specs)
    out_shape = [jax.ShapeDtypeStruct((n, d), F32)]
    out_specs = [pl.BlockSpec((tm, d), row)]
    if is_sample:
        n_st = (CONV_W - 1) * (tm // t_seq)
        args.append(conv_state)
        in_specs.append(pl.BlockSpec((n_st, d_up), row))
        out_shape.append(jax.ShapeDtypeStruct(conv_state.shape, F32))
        out_specs.append(pl.BlockSpec((n_st, d_up), row))
        tiles_per_seq = 1
    else:
        tiles_per_seq = n // n_seq // tm
        out_shape.append(jax.ShapeDtypeStruct((n_seq, SUBLANES, d_up), F32))
        out_specs.append(pl.BlockSpec((None, SUBLANES, d_up), lambda i: (i // tiles_per_seq, 0, 0)))
    return pl.pallas_call(
        functools.partial(_ffn_kernel, is_sample=is_sample, is_last=is_last, t_seq=t_seq,
                          tiles_per_seq=tiles_per_seq),
        grid=(n // tm,),
        in_specs=in_specs,
        out_specs=tuple(out_specs),
        out_shape=tuple(out_shape),
        compiler_params=_params(1),
        name="ffn_sample" if is_sample else "ffn_prompt",
    )(*args)


def _swap_halves(w):
    half = w.shape[-1] // 2
    return jnp.concatenate([w[..., half:], w[..., :half]], axis=-1)


def _stacked_weights(w_out, ffn_w_up, ffn_w_down, ple_w_gate):
    return {'w_out': w_out.astype(BF16), 'w_up': ffn_w_up.astype(BF16),
            'w_down': ffn_w_down.astype(BF16), 'w_pg': ple_w_gate.astype(BF16)}


def _layer_weights(l, stacked, norm_mix, w_in, m_gate_bias, m_norm, mla_q_norm, mla_w_uq, mla_kv_norm, mla_w_uk,
                   mla_w_uv, w_out, norm_ffn, ffn_w_up, ffn_conv, ffn_conv_b, ffn_w_down, ple_norm,
                   ple_w_gate, ple_w_proj):
    d = w_in.shape[1]
    wi = w_in[l]
    sizes = (256, 256, 512, 512, M_HEADS, M_HEADS, Q_RANK, KV_RANK, A_ROPE)
    offs = [0]
    for s in sizes:
        offs.append(offs[-1] + s)
    qm, km, vm, om, im, fm, cq, ckv, kr = (wi[:, offs[i]:offs[i + 1]] for i in range(len(sizes)))
    gate_pad = jnp.zeros((d, LANES - 2 * M_HEADS), wi.dtype)
    w_in_r = jnp.concatenate([fm, im, gate_pad, cq, ckv, kr, _swap_halves(kr), vm, om, qm, km], axis=1)
    uq = mla_w_uq[l].reshape(Q_RANK, A_HEADS, A_NOPE + A_ROPE)
    uq_rope = uq[:, :, A_NOPE:]
    w_uq_r = jnp.concatenate([uq[:, :, :A_NOPE].reshape(Q_RANK, A_HEADS * A_NOPE),
                              jnp.concatenate([uq_rope, _swap_halves(uq_rope)], axis=-1)
                              .reshape(Q_RANK, A_HEADS * LANES)], axis=1)
    row = lambda v: v.reshape(1, -1)
    return {
        'norm_mix': row(norm_mix[l]),
        'w_in': w_in_r.astype(BF16),
        'w_voq_t': w_in_r[:, C_V:C_K].T.astype(BF16),
        'gate_bias': jnp.concatenate([m_gate_bias[l, 1], m_gate_bias[l, 0]]).reshape(SUBLANES, 1),
        'm_norm': m_norm[l],
        'mla_q_norm': row(mla_q_norm[l]),
        'w_uq': w_uq_r.astype(BF16),
        'mla_kv_norm': row(mla_kv_norm[l]),
        'w_uk': jnp.transpose(mla_w_uk[l], (1, 2, 0)).astype(BF16),
        'w_uv': jnp.transpose(mla_w_uv[l], (1, 0, 2)).astype(BF16),
        'w_out': (stacked['w_out'], l),
        'norm_ffn': row(norm_ffn[l]),
        'w_up': (stacked['w_up'], l),
        'ffn_conv': ffn_conv[l],
        'ffn_conv_b': row(ffn_conv_b[l]),
        'w_down': (stacked['w_down'], l),
        'ple_norm': row(ple_norm[l]),
        'w_pg': (stacked['w_pg'], l),
        'w_pp': ple_w_proj[l].astype(BF16),
    }


def _rope_table(pos):
    half = A_ROPE // 2
    freqs = ROPE_THETA ** (-jnp.arange(half, dtype=F32) * 2.0 / A_ROPE)
    ang = pos.astype(F32)[:, None] * freqs[None, :]
    cos, sin = jnp.cos(ang), jnp.sin(ang)
    return jnp.concatenate([cos, cos, -sin, sin], axis=1)


def kernel(x_prompt, x_sample, p_prompt, p_sample, cache_mla, state_mlstm_C, state_mlstm_n, state_mlstm_m,
           state_ffn_conv, page_table, norm_mix, w_in, m_gate_bias, m_norm, mla_q_norm, mla_w_uq, mla_kv_norm,
           mla_w_uk, mla_w_uv, w_out, norm_ffn, ffn_w_up, ffn_conv, ffn_conv_b, ffn_w_down, ple_norm,
           ple_w_gate, ple_w_proj, final_norm):
    depth = w_in.shape[0]
    bp, tp, d = x_prompt.shape
    bs, ts, _ = x_sample.shape
    n_p, n_s = bp * tp, bs * ts
    past_len = page_table.shape[1] * cache_mla.shape[2]
    d_up = ffn_w_up.shape[2]

    tab_p = jnp.tile(_rope_table(jnp.arange(tp)), (bp, 1))
    tab_s = jnp.tile(_rope_table(past_len + jnp.arange(ts)), (bs, 1))
    xp = x_prompt.reshape(n_p, d)
    xs = x_sample.reshape(n_s, d)
    final_gain = final_norm.reshape(1, d)
    cache_t = jnp.swapaxes(cache_mla, 2, 3)

    stacked = _stacked_weights(w_out, ffn_w_up, ffn_w_down, ple_w_gate)
    lat_p, lat_s, c_p, n_pl, m_p, c_s, n_sl, m_s, conv_p, conv_s = ([] for _ in range(10))
    for l in range(depth):
        lw = _layer_weights(l, stacked, norm_mix, w_in, m_gate_bias, m_norm, mla_q_norm, mla_w_uq, mla_kv_norm,
                            mla_w_uk, mla_w_uv, w_out, norm_ffn, ffn_w_up, ffn_conv, ffn_conv_b,
                            ffn_w_down, ple_norm, ple_w_gate, ple_w_proj)
        fin = final_gain if l == depth - 1 else None

        km, qt, vt, ot, gscan, gdiff, dcol, lat, katt, katt_t, qatt_t = _proj(xp, tab_p, lw, True, bp)
        by_seq = lambda a: a.reshape(bp, tp, a.shape[-1])
        hm, state, m_new = _mlstm_prompt(by_seq(km), qt, vt, ot, gscan, gdiff, by_seq(dcol), lw['m_norm'].T, bp)
        c_new = jnp.swapaxes(state[:, :, :M_DV, :], 2, 3)
        n_new = state[:, :, M_DV, :]
        olat = _attn_prompt(qatt_t, katt, katt_t, bp)
        xp, tail = _ffn(xp, hm.reshape(n_p, -1), olat, p_prompt.reshape(depth, n_p, -1), l, lw, final_gain=fin,
                        conv_state=None, n_seq=bp, t_seq=tp)
        lat_p.append(lat.reshape(bp, tp, LAT_DIM))
        c_p.append(c_new)
        n_pl.append(n_new)
        m_p.append(m_new[:, :M_HEADS, 0])
        conv_p.append(tail[:, SUBLANES - (CONV_W - 1):])

        qk, v, o, gates, lat, katt, qatt = _proj(xs, tab_s, lw, False, 1)
        m_prev_rows = jnp.repeat(state_mlstm_m[l].T, ts, axis=1)
        hm, c_new, n_new_t, m_rows = _mlstm_sample(qk, v, o, gates, m_prev_rows, lw['m_norm'],
                                                   state_mlstm_C, jnp.transpose(state_mlstm_n[l], (1, 0, 2)),
                                                   l, ts)
        q_seq = jnp.transpose(qatt.reshape(A_HEADS, bs, ts, LAT_PAD), (1, 0, 2, 3)).reshape(bs, A_HEADS * ts, LAT_PAD)
        o_seq = _attn_sample(q_seq, katt.reshape(bs, ts, LAT_PAD), cache_t, page_table, l)
        olat = jnp.transpose(o_seq.reshape(bs, A_HEADS, ts, KV_RANK), (1, 0, 2, 3)).reshape(A_HEADS, n_s, KV_RANK)
        xs, st_new = _ffn(xs, hm, olat, p_sample.reshape(depth, n_s, -1), l, lw, final_gain=fin,
                          conv_state=state_ffn_conv[l].reshape(bs * (CONV_W - 1), d_up), n_seq=bs, t_seq=ts)
        lat_s.append(lat.reshape(bs, ts, LAT_DIM))
        c_s.append(c_new)
        n_sl.append(jnp.transpose(n_new_t, (1, 0, 2)))
        m_s.append(m_rows[:, ts - 1::ts].T)
        conv_s.append(st_new.reshape(bs, CONV_W - 1, d_up))

    return (xp.reshape(bp, tp, d), xs.reshape(bs, ts, d), jnp.stack(lat_p), jnp.stack(lat_s),
            jnp.stack(c_p), jnp.stack(n_pl), jnp.stack(m_p), jnp.stack(c_s), jnp.stack(n_sl), jnp.stack(m_s),
            jnp.stack(conv_p), jnp.stack(conv_s))
```

```python
import functools

import jax
import jax.numpy as jnp
from jax import lax
from jax.experimental import pallas as pl
from jax.experimental.pallas import tpu as pltpu

F32 = jnp.float32
BF16 = jnp.bfloat16

M_HEADS = 4
M_DK = 64
M_DV = 128
A_HEADS = 4
A_NOPE = 128
A_ROPE = 64
KV_RANK = 256
Q_RANK = 384
LAT_DIM = KV_RANK + A_ROPE
GATE_CAP = 15.0
ROPE_THETA = 10000.0
EPS = 1e-6
CONV_W = 3
ATT_SCALE = (A_NOPE + A_ROPE) ** -0.5
LOG2_E = 1.4426950408889634
Q_SCALE = ATT_SCALE * LOG2_E

LANES = 128
SUBLANES = 8
VMEM_LIMIT_BYTES = 56 * 1024 * 1024

LAT_PAD = 3 * LANES
TOK_TILE = 256
MLSTM_CHUNK = 256
ATT_BLOCK = 1024
SAMPLE_GROUP = 32
FFN_TILE = 512
PROJ_TILE = 512
PAGE_RING = 3
SAMPLE_FFN_TILE = 128
STATE_ROWS = M_DV + SUBLANES

C_G, C_CQ, C_CKV, C_KR = 0, 128, 512, 768
C_V, C_O, C_Q, C_K = 896, 1408, 1920, 2176
IN_COLS = 2432


def _const_spec(shape):
    nd = len(shape)
    return pl.BlockSpec(shape, lambda *_: (0,) * nd, pipeline_mode=pl.Buffered(1))


def _weight(w):
    if not isinstance(w, tuple):
        return _const_spec(w.shape), w
    arr, layer = w
    tail = (0,) * (arr.ndim - 1)
    return pl.BlockSpec((None,) + arr.shape[1:], lambda *_: (layer,) + tail, pipeline_mode=pl.Buffered(1)), arr


def _params(n_axes):
    return pltpu.CompilerParams(dimension_semantics=("arbitrary",) * n_axes,
                                vmem_limit_bytes=VMEM_LIMIT_BYTES)


def _rms(x, g):
    return x * lax.rsqrt(jnp.mean(x * x, axis=-1, keepdims=True) + EPS) * g


def _sigmoid(x):
    return 1.0 / (1.0 + jnp.exp(-x))


def _dot(a, b):
    return jnp.dot(a, b, preferred_element_type=F32)


def _dot_nt(a, b):
    return lax.dot_general(a, b, (((1,), (1,)), ((), ())), preferred_element_type=F32)


def _fold_halves(p):
    return p + pltpu.roll(p, 64, 1)


def _proj_kernel(x_ref, tab_ref, gmix_ref, win_ref, gb_ref, gq_ref, wuq_ref, gkv_ref, wuk_ref,
                 *refs, feature_major):
    if feature_major:
        (wvoq_ref, km_ref, qt_ref, vt_ref, ot_ref, gscan_ref, gdiff_ref, dcol_ref,
         lat_ref, katt_ref, katt_t_ref, qatt_ref) = refs
    else:
        qk_ref, v_ref, o_ref, gates_ref, lat_ref, katt_ref, qatt_ref = refs
    h = _rms(x_ref[...], gmix_ref[...]).astype(BF16)
    z = _dot(h, win_ref[:, 0:C_V])

    gt = z[:, C_G:C_G + LANES].T[0:SUBLANES, :] + gb_ref[...]
    capped = GATE_CAP * jnp.tanh(gt / GATE_CAP)
    log_sig = jnp.minimum(capped, 0.0) - jnp.log1p(jnp.exp(-jnp.abs(capped)))
    row = lax.broadcasted_iota(jnp.int32, gt.shape, 0)
    gates = jnp.where(row < M_HEADS, log_sig, capped)
    if feature_major:
        b, d, e = _gate_scans(gates, MLSTM_CHUNK)
        gscan_ref[...] = jnp.where(row < M_HEADS, b, pltpu.roll(e, M_HEADS, 0))
        d = jnp.where(row < M_HEADS, d, 0.0)
        gdiff_ref[...] = d
        dcol_ref[...] = _to_cols((d,), d.shape[1])
    else:
        gates_ref[...] = gates

    tab = tab_ref[...]
    lane = lax.broadcasted_iota(jnp.int32, tab.shape, 1)
    ckv_n = _rms(z[:, C_CKV:C_CKV + KV_RANK], gkv_ref[...])
    rk = _fold_halves(z[:, C_KR:C_KR + LANES] * tab)
    lat_ref[:, 0:KV_RANK] = ckv_n
    lat_ref[:, KV_RANK:LAT_DIM] = rk[:, 0:A_ROPE]
    k_cat = jnp.concatenate([ckv_n, jnp.where(lane < A_ROPE, rk, 0.0)], axis=1)
    katt_ref[...] = k_cat.astype(BF16)
    if feature_major:
        katt_t_ref[...] = k_cat.T.astype(BF16)

    qn = _rms(z[:, C_CQ:C_CQ + Q_RANK], gq_ref[...]).astype(BF16)
    qa = _dot(qn, wuq_ref[...])
    for hd in range(A_HEADS):
        q_lat = _dot(qa[:, hd * A_NOPE:(hd + 1) * A_NOPE].astype(BF16), wuk_ref[hd])
        rq = _fold_halves(qa[:, 512 + hd * LANES:512 + (hd + 1) * LANES] * tab)
        q_cat = jnp.concatenate([q_lat, jnp.where(lane < A_ROPE, rq, 0.0)], axis=1) * Q_SCALE
        qatt_ref[hd] = (q_cat.T if feature_major else q_cat).astype(BF16)

    k_m = (_dot(h, win_ref[:, C_K:C_K + 256]) * (M_DK ** -0.5)).astype(BF16)
    if feature_major:
        zt = _dot_nt(wvoq_ref[...], h)
        vt_ref[...] = zt[0:512, :].astype(BF16)
        ot_ref[...] = zt[512:1024, :]
        qt_ref[...] = zt[1024:1280, :].astype(BF16)
        km_ref[...] = k_m
    else:
        z2 = _dot(h, win_ref[:, C_V:C_K])
        v_ref[...] = z2[:, 0:512].astype(BF16)
        o_ref[...] = z2[:, 512:1024]
        qk_ref[:, 0:256] = z2[:, 1024:1280].astype(BF16)
        qk_ref[:, 256:512] = k_m


def _proj(x, tab, lw, feature_major, gate_seqs):
    n = x.shape[0]
    tm = PROJ_TILE if feature_major else TOK_TILE
    tiles_per_seq = n // gate_seqs // tm
    row = lambda i: (i, 0)
    col = lambda i: (0, i)
    band = lambda i: (i // tiles_per_seq, i % tiles_per_seq)
    gate_rows = jax.ShapeDtypeStruct((gate_seqs * SUBLANES, n // gate_seqs), F32)
    if feature_major:
        assert tm % MLSTM_CHUNK == 0
        out_shape = [
            jax.ShapeDtypeStruct((n, 256), BF16),
            jax.ShapeDtypeStruct((256, n), BF16),
            jax.ShapeDtypeStruct((512, n), BF16),
            jax.ShapeDtypeStruct((512, n), F32),
            gate_rows,
            gate_rows,
            jax.ShapeDtypeStruct((n, LANES), F32),
        ]
        out_specs = [
            pl.BlockSpec((tm, 256), row),
            pl.BlockSpec((256, tm), col),
            pl.BlockSpec((512, tm), col),
            pl.BlockSpec((512, tm), col),
            pl.BlockSpec((SUBLANES, tm), band),
            pl.BlockSpec((SUBLANES, tm), band),
            pl.BlockSpec((tm, LANES), row),
        ]
    else:
        out_shape = [
            jax.ShapeDtypeStruct((n, 512), BF16),
            jax.ShapeDtypeStruct((n, 512), BF16),
            jax.ShapeDtypeStruct((n, 512), F32),
            gate_rows,
        ]
        out_specs = [
            pl.BlockSpec((tm, 512), row),
            pl.BlockSpec((tm, 512), row),
            pl.BlockSpec((tm, 512), row),
            pl.BlockSpec((SUBLANES, tm), band),
        ]
    out_shape += [jax.ShapeDtypeStruct((n, LAT_DIM), F32),
                  jax.ShapeDtypeStruct((n, LAT_PAD), BF16)]
    out_specs += [pl.BlockSpec((tm, LAT_DIM), row), pl.BlockSpec((tm, LAT_PAD), row)]
    if feature_major:
        out_shape += [jax.ShapeDtypeStruct((LAT_PAD, n), BF16),
                      jax.ShapeDtypeStruct((A_HEADS, LAT_PAD, n), BF16)]
        out_specs += [pl.BlockSpec((LAT_PAD, tm), col),
                      pl.BlockSpec((A_HEADS, LAT_PAD, tm), lambda i: (0, 0, i))]
    else:
        out_shape.append(jax.ShapeDtypeStruct((A_HEADS, n, LAT_PAD), BF16))
        out_specs.append(pl.BlockSpec((A_HEADS, tm, LAT_PAD), lambda i: (0, i, 0)))
    return pl.pallas_call(
        functools.partial(_proj_kernel, feature_major=feature_major),
        grid=(n // tm,),
        in_specs=[
            pl.BlockSpec((tm, x.shape[1]), row),
            pl.BlockSpec((tm, LANES), row),
            _const_spec(lw['norm_mix'].shape),
            _const_spec(lw['w_in'].shape),
            _const_spec(lw['gate_bias'].shape),
            _const_spec(lw['mla_q_norm'].shape),
            _const_spec(lw['w_uq'].shape),
            _const_spec(lw['mla_kv_norm'].shape),
            _const_spec(lw['w_uk'].shape),
        ] + ([_const_spec(lw['w_voq_t'].shape)] if feature_major else []),
        out_specs=tuple(out_specs),
        out_shape=tuple(out_shape),
        compiler_params=_params(1),
        name="proj_prompt" if feature_major else "proj_sample",
    )(x, tab, lw['norm_mix'], lw['w_in'], lw['gate_bias'], lw['mla_q_norm'], lw['w_uq'],
      lw['mla_kv_norm'], lw['w_uk'], *([lw['w_voq_t']] if feature_major else []))


def _seg_scan(x, op, fill, seg):
    n = x.shape[1]
    lane = lax.broadcasted_iota(jnp.int32, x.shape, 1)
    pos = lane if seg is None else lane % seg
    span = n if seg is None else seg
    s = 1
    while s < span:
        x = op(x, jnp.where(pos >= s, pltpu.roll(x, s, 1), fill))
        s *= 2
    return x


def _gate_scans(gates, seg):
    ig = pltpu.roll(gates, M_HEADS, 0)
    b = _seg_scan(gates, jnp.add, 0.0, seg)
    d = ig - b
    e = _seg_scan(d, jnp.maximum, -jnp.inf, seg)
    return b, d, e


def _gate_rows(gates, m_prev, seg):
    b, d, e = _gate_scans(gates, seg)
    return b, d, pltpu.roll(gates, M_HEADS, 0), b + jnp.maximum(m_prev, e)


def _to_cols(row_blocks, n_tok):
    pad = jnp.zeros((LANES - SUBLANES * len(row_blocks), n_tok), F32)
    return jnp.concatenate(list(row_blocks) + [pad], axis=0).T


def _head_out(hh, og, gain):
    hn = hh * lax.rsqrt(jnp.mean(hh * hh, axis=-1, keepdims=True) + EPS) * gain
    return (hn * _sigmoid(og)).astype(BF16)


def _mlstm_prompt_kernel(*refs, n_seq):
    km_ref, gscan_ref, gdiff_ref, dcol_ref, gain_ref = refs[:5]
    qt_refs = refs[5:5 + n_seq]
    vt_refs = refs[5 + n_seq:5 + 2 * n_seq]
    ot_refs = refs[5 + 2 * n_seq:5 + 3 * n_seq]
    hm_ref, s_ref, m_ref = refs[5 + 3 * n_seq:]
    L = km_ref.shape[1]

    @pl.when(pl.program_id(0) == 0)
    def _():
        s_ref[...] = jnp.zeros_like(s_ref)
        m_ref[...] = jnp.zeros_like(m_ref)

    ki = lax.broadcasted_iota(jnp.int32, (L, L), 0)
    qi = lax.broadcasted_iota(jnp.int32, (L, L), 1)
    causal = ki <= qi
    pad_rows = jnp.zeros((STATE_ROWS - M_DV - 1, L), F32)

    for sq in range(n_seq):
        band = slice(sq * SUBLANES, (sq + 1) * SUBLANES)
        scan = gscan_ref[band, :]
        diff = gdiff_ref[band, :]
        ks = [km_ref[sq, :, hd * M_DK:(hd + 1) * M_DK] for hd in range(M_HEADS)]
        qts = [qt_refs[sq][hd * M_DK:(hd + 1) * M_DK, :] for hd in range(M_HEADS)]
        states = [s_ref[sq, hd] for hd in range(M_HEADS)]
        kq = [_dot(ks[hd], qts[hd]) for hd in range(M_HEADS)]
        q_state = [_dot(states[hd].astype(BF16), qts[hd]) for hd in range(M_HEADS)]

        for hd in range(M_HEADS):
            vt = vt_refs[sq][hd * M_DV:(hd + 1) * M_DV, :]
            b_r = scan[hd:hd + 1, :]
            e_r = scan[M_HEADS + hd:M_HEADS + hd + 1, :]
            d_r = diff[hd:hd + 1, :]
            d_c = dcol_ref[sq, :, hd:hd + 1]
            mp = m_ref[sq, hd:hd + 1, 0:1]
            mx_r = jnp.maximum(mp, e_r)
            mx_last = mx_r[:, L - 1:L]

            st = kq[hd] * jnp.exp(jnp.where(causal, d_c - mx_r, -jnp.inf))
            a_r = jnp.exp(mp - mx_r)
            num = _dot(vt, st.astype(BF16)) + a_r * q_state[hd][0:M_DV, :]
            nq = jnp.sum(st, axis=0, keepdims=True) + a_r * q_state[hd][M_DV:M_DV + 1, :]
            inv = 1.0 / jnp.maximum(jnp.abs(nq), jnp.exp(-(b_r + mx_r)))
            rs = lax.rsqrt(inv * inv * jnp.mean(num * num, axis=0, keepdims=True) + EPS)
            ht = num * (inv * rs) * gain_ref[:, hd:hd + 1] * _sigmoid(ot_refs[sq][hd * M_DV:(hd + 1) * M_DV, :])
            hm_ref[sq, :, hd * M_DV:(hd + 1) * M_DV] = ht.T.astype(BF16)

            w_r = jnp.exp(d_r - mx_last)
            lhs = jnp.concatenate([vt.astype(F32) * w_r, w_r, pad_rows], axis=0).astype(BF16)
            s_ref[sq, hd] = jnp.exp(mp - mx_last) * states[hd] + _dot(lhs, ks[hd])
            m_ref[sq, hd:hd + 1, :] = jnp.broadcast_to(b_r[:, L - 1:L] + mx_last, (1, LANES))


def _mlstm_prompt(km, qt, vt, ot, gscan, gdiff, dcol, gain_t, n_seq):
    t = km.shape[1]
    L = MLSTM_CHUNK
    nc = t // L
    tok = lambda c: (0, c, 0)
    whole4 = lambda c: (0, 0, 0, 0)
    whole3 = lambda c: (0, 0, 0)

    def seq_cols(rows):
        return [pl.BlockSpec((rows, L), lambda c, sq=sq: (0, sq * nc + c)) for sq in range(n_seq)]

    return pl.pallas_call(
        functools.partial(_mlstm_prompt_kernel, n_seq=n_seq),
        grid=(nc,),
        in_specs=[
            pl.BlockSpec((n_seq, L, 256), tok),
            pl.BlockSpec((n_seq * SUBLANES, L), lambda c: (0, c)),
            pl.BlockSpec((n_seq * SUBLANES, L), lambda c: (0, c)),
            pl.BlockSpec((n_seq, L, LANES), tok),
            _const_spec(gain_t.shape),
        ] + seq_cols(256) + seq_cols(512) + seq_cols(512),
        out_specs=(
            pl.BlockSpec((n_seq, L, 512), tok),
            pl.BlockSpec((n_seq, M_HEADS, STATE_ROWS, M_DK), whole4),
            pl.BlockSpec((n_seq, SUBLANES, LANES), whole3),
        ),
        out_shape=(
            jax.ShapeDtypeStruct((n_seq, t, 512), BF16),
            jax.ShapeDtypeStruct((n_seq, M_HEADS, STATE_ROWS, M_DK), F32),
            jax.ShapeDtypeStruct((n_seq, SUBLANES, LANES), F32),
        ),
        compiler_params=_params(1),
        name="mlstm_prompt",
    )(km, gscan, gdiff, dcol, gain_t, *([qt] * n_seq), *([vt] * n_seq), *([ot] * n_seq))


def _exact_onehot_dot(onehot, x):
    hi = x.astype(BF16)
    r1 = x - hi.astype(F32)
    mid = r1.astype(BF16)
    lo = (r1 - mid.astype(F32)).astype(BF16)
    return (_dot(onehot, hi) + _dot(onehot, mid)) + _dot(onehot, lo)


def _mlstm_sample_kernel(qk_ref, v_ref, o_ref, g_ref, mp_ref, gain_ref, c0_ref, n0_ref,
                         hm_ref, c_ref, n_ref, m_ref, cols_ref, *, t_seq):
    L = qk_ref.shape[0]
    G = L // t_seq
    mp_rows = jnp.concatenate([mp_ref[...], jnp.zeros((SUBLANES - M_HEADS, L), F32)], axis=0)
    b, d, ig, m = _gate_rows(g_ref[...], mp_rows, t_seq)
    m_ref[...] = m[0:M_HEADS, :]

    lane = lax.broadcasted_iota(jnp.int32, b.shape, 1)
    pos = lane % t_seq

    def seg_last(x):
        s = 1
        while s < t_seq:
            x = jnp.where(pos >= t_seq - s, x, pltpu.roll(x, L - s, 1))
            s *= 2
        return x

    b_last = seg_last(b)
    m_new = seg_last(m)
    w_rows = jnp.exp(b_last - b + ig - m_new)
    decay_rows = jnp.exp(b_last + mp_rows - m_new)
    cols_ref[...] = _to_cols((b, m, mp_rows, w_rows, decay_rows), L)
    cols = cols_ref[...]
    seq_cols = cols_ref[pl.ds(0, G, stride=t_seq), :]

    qi = lax.broadcasted_iota(jnp.int32, (L, L), 0)
    ki = lax.broadcasted_iota(jnp.int32, (L, L), 1)
    mask = (ki <= qi) & (ki // t_seq == qi // t_seq)
    er = lax.broadcasted_iota(jnp.int32, (L, G * M_DK), 0)
    ec = lax.broadcasted_iota(jnp.int32, (L, G * M_DK), 1)
    own = (ec // M_DK) == (er // t_seq)
    tok_of_seq = (lax.broadcasted_iota(jnp.int32, (L, G), 0) // t_seq
                  == lax.broadcasted_iota(jnp.int32, (L, G), 1)).astype(BF16)
    seq_of_tok = (lax.broadcasted_iota(jnp.int32, (G, L), 1) // t_seq
                  == lax.broadcasted_iota(jnp.int32, (G, L), 0)).astype(BF16)

    def expand(x):
        x2 = jnp.concatenate([x, x], axis=1)
        return jnp.where(own, jnp.tile(x2, (1, G // 2)), 0.0)

    for hd in range(M_HEADS):
        q = qk_ref[:, hd * M_DK:(hd + 1) * M_DK]
        k = qk_ref[:, 256 + hd * M_DK:256 + (hd + 1) * M_DK]
        v = v_ref[:, hd * M_DV:(hd + 1) * M_DV]
        b_c = cols[:, hd:hd + 1]
        m_c = cols[:, 8 + hd:9 + hd]
        mp_c = cols[:, 16 + hd:17 + hd]
        w_c = cols[:, 24 + hd:25 + hd]
        decay_seq = seq_cols[:, 32 + hd:33 + hd]
        n_old = n0_ref[hd]
        c_old = c0_ref[:, hd].reshape(G * M_DK, M_DV)

        p = jnp.exp(jnp.where(mask, b_c + d[hd:hd + 1, :] - m_c, -jnp.inf))
        s = _dot_nt(q, k) * p
        a_c = jnp.exp(b_c + mp_c - m_c)
        qf = q.astype(F32)
        num = _dot(s.astype(BF16), v) + a_c * _dot(expand(qf).astype(BF16), c_old.astype(BF16))
        n_tok = _exact_onehot_dot(tok_of_seq, n_old)
        nq = jnp.sum(s, axis=1, keepdims=True) + a_c * jnp.sum(qf * n_tok, axis=1, keepdims=True)
        den = jnp.maximum(jnp.abs(nq), jnp.exp(-m_c))
        sl = slice(hd * M_DV, (hd + 1) * M_DV)
        hm_ref[:, sl] = _head_out(num / den, o_ref[:, sl], gain_ref[hd:hd + 1, :])

        kw = k.astype(F32) * w_c
        upd = _dot(expand(kw).T.astype(BF16), v)
        for g in range(G):
            c_ref[g, hd] = decay_seq[g:g + 1, :] * c0_ref[g, hd] + upd[g * M_DK:(g + 1) * M_DK, :]
        n_ref[hd] = decay_seq * n_old + _dot(seq_of_tok, kw.astype(BF16))


def _mlstm_sample(qk, v, o, gates, m_prev_rows, gain, c0, n0_t, layer, t_seq):
    n = qk.shape[0]
    G = SAMPLE_GROUP
    L = G * t_seq
    n_seq = n // t_seq
    tok = lambda i: (i, 0)
    return pl.pallas_call(
        functools.partial(_mlstm_sample_kernel, t_seq=t_seq),
        grid=(n // L,),
        in_specs=[
            pl.BlockSpec((L, 512), tok),
            pl.BlockSpec((L, 512), tok),
            pl.BlockSpec((L, 512), tok),
            pl.BlockSpec((SUBLANES, L), lambda i: (0, i)),
            pl.BlockSpec((M_HEADS, L), lambda i: (0, i)),
            _const_spec(gain.shape),
            pl.BlockSpec((None, G, M_HEADS, M_DK, M_DV), lambda i: (layer, i, 0, 0, 0)),
            pl.BlockSpec((M_HEADS, G, M_DK), lambda i: (0, i, 0)),
        ],
        out_specs=(
            pl.BlockSpec((L, 512), tok),
            pl.BlockSpec((G, M_HEADS, M_DK, M_DV), lambda i: (i, 0, 0, 0)),
            pl.BlockSpec((M_HEADS, G, M_DK), lambda i: (0, i, 0)),
            pl.BlockSpec((M_HEADS, L), lambda i: (0, i)),
        ),
        out_shape=(
            jax.ShapeDtypeStruct((n, 512), BF16),
            jax.ShapeDtypeStruct((n_seq, M_HEADS, M_DK, M_DV), F32),
            jax.ShapeDtypeStruct((M_HEADS, n_seq, M_DK), F32),
            jax.ShapeDtypeStruct((M_HEADS, n), F32),
        ),
        scratch_shapes=[pltpu.VMEM((L, LANES), F32)],
        compiler_params=_params(1),
        name="mlstm_sample",
    )(qk, v, o, gates, m_prev_rows, gain, c0, n0_t)


def _attn_prompt_kernel(qt_ref, k_ref, kt_ref, o_ref, m_scr, l_scr, acc_scr, s0_scr):
    tq = qt_ref.shape[2]
    qb = pl.program_id(1)
    m_scr[...] = jnp.full_like(m_scr, -jnp.inf)
    l_scr[...] = jnp.zeros_like(l_scr)
    acc_scr[...] = jnp.zeros_like(acc_scr)

    def keys(kb):
        return k_ref[pl.ds(pl.multiple_of(kb * tq, tq), tq), :]

    s0_scr[...] = _dot(keys(0), qt_ref[0])

    def block(kb, last):
        kblk = keys(kb)
        vt = kt_ref[0:KV_RANK, pl.ds(pl.multiple_of(kb * tq, tq), tq)]
        scores = {0: s0_scr[...]}
        s0_next = None
        for hd in range(A_HEADS):
            if hd + 1 < A_HEADS:
                scores[hd + 1] = _dot(kblk, qt_ref[hd + 1])
            elif not last:
                s0_next = _dot(keys(kb + 1), qt_ref[0])
            st = scores.pop(hd)
            if last:
                kpos = lax.broadcasted_iota(jnp.int32, st.shape, 0)
                qpos = lax.broadcasted_iota(jnp.int32, st.shape, 1)
                st = jnp.where(kpos <= qpos, st, -jnp.inf)
            m_old = m_scr[hd:hd + 1, :]
            m_new = jnp.maximum(m_old, jnp.max(st, axis=0, keepdims=True))
            alpha = jnp.exp2(m_old - m_new)
            p = jnp.exp2(st - m_new)
            l_scr[hd:hd + 1, :] = alpha * l_scr[hd:hd + 1, :] + jnp.sum(p, axis=0, keepdims=True)
            acc_scr[hd] = alpha * acc_scr[hd] + _dot(vt, p.astype(BF16))
            m_scr[hd:hd + 1, :] = m_new
        if not last:
            s0_scr[...] = s0_next

    def body(kb, carry):
        block(kb, False)
        return carry

    lax.fori_loop(0, qb, body, 0)
    block(qb, True)
    for hd in range(A_HEADS):
        o_ref[hd] = (acc_scr[hd] / l_scr[hd:hd + 1, :]).T.astype(BF16)


def _attn_prompt(qatt_t, katt, katt_t, n_seq):
    n = katt.shape[0]
    t = n // n_seq
    tq = ATT_BLOCK
    nq = t // tq
    return pl.pallas_call(
        _attn_prompt_kernel,
        grid=(n_seq, nq),
        in_specs=[
            pl.BlockSpec((A_HEADS, LAT_PAD, tq), lambda b, i: (0, 0, b * nq + i)),
            pl.BlockSpec((t, LAT_PAD), lambda b, i: (b, 0)),
            pl.BlockSpec((LAT_PAD, t), lambda b, i: (0, b)),
        ],
        out_specs=pl.BlockSpec((A_HEADS, tq, KV_RANK), lambda b, i: (0, b * nq + i, 0)),
        out_shape=jax.ShapeDtypeStruct((A_HEADS, n, KV_RANK), BF16),
        scratch_shapes=[pltpu.VMEM((SUBLANES, tq), F32), pltpu.VMEM((SUBLANES, tq), F32),
                        pltpu.VMEM((A_HEADS, KV_RANK, tq), F32), pltpu.VMEM((tq, tq), F32)],
        compiler_params=_params(2),
        name="attn_prompt",
    )(qatt_t, katt, katt_t)


def _attn_sample_kernel(pt_ref, q_ref, knew_ref, cache_ref, o_ref, pages, kbuf, sems, *, n_pages, t_seq, layer):
    s = pl.program_id(0)
    n_seq = pl.num_programs(0)
    page = pages.shape[3]

    def page_copy(seq, j):
        slot = seq % PAGE_RING
        return pltpu.make_async_copy(cache_ref.at[layer, pt_ref[seq, j]], pages.at[slot, j], sems.at[slot])

    def request(seq):
        for j in range(n_pages):
            page_copy(seq, j).start(priority=j % 2)

    @pl.when(s == 0)
    def _():
        for ahead in range(PAGE_RING - 1):
            request(ahead)

    for j in range(n_pages):
        page_copy(s, j).wait()

    @pl.when(s + PAGE_RING - 1 < n_seq)
    def _():
        request(s + PAGE_RING - 1)

    slot = s % PAGE_RING
    for j in range(n_pages):
        kbuf[:, j * page:(j + 1) * page] = pages[slot, j].astype(BF16)

    q = q_ref[...]
    rows = q.shape[0]
    s_past = _dot(q[:, 0:LAT_DIM], kbuf[...])
    qf = q.astype(F32)
    knew = knew_ref[...].astype(F32)
    tq = lax.broadcasted_iota(jnp.int32, (rows, 1), 0) % t_seq
    s_new = []
    for j in range(t_seq):
        sj = jnp.sum(qf * knew[j:j + 1, :], axis=1, keepdims=True)
        s_new.append(jnp.where(tq >= j, sj, -jnp.inf))
    m = jnp.max(s_past, axis=1, keepdims=True)
    for sj in s_new:
        m = jnp.maximum(m, sj)
    p_past = jnp.exp2(s_past - m)
    p_new = [jnp.exp2(sj - m) for sj in s_new]
    denom = jnp.sum(p_past, axis=1, keepdims=True)
    for pj in p_new:
        denom = denom + pj
    inv = 1.0 / denom
    out = _dot_nt((p_past * inv).astype(BF16), kbuf[0:KV_RANK, :])
    for j in range(t_seq):
        pj = (p_new[j] * inv).astype(BF16).astype(F32)
        out = out + pj * knew[j:j + 1, 0:KV_RANK]
    o_ref[...] = out.astype(BF16)


def _attn_sample(q_seq, knew_seq, cache_t, page_table, layer):
    n_seq, rows, _ = q_seq.shape
    t_seq = knew_seq.shape[1]
    n_pages = page_table.shape[1]
    page = cache_t.shape[3]
    assert n_seq >= PAGE_RING - 1
    grid_spec = pltpu.PrefetchScalarGridSpec(
        num_scalar_prefetch=1,
        grid=(n_seq,),
        in_specs=[pl.BlockSpec((None, rows, LAT_PAD), lambda s, pt: (s, 0, 0)),
                  pl.BlockSpec((None, t_seq, LAT_PAD), lambda s, pt: (s, 0, 0)),
                  pl.BlockSpec(memory_space=pl.ANY)],
        out_specs=pl.BlockSpec((None, rows, KV_RANK), lambda s, pt: (s, 0, 0)),
        scratch_shapes=[pltpu.VMEM((PAGE_RING, n_pages, LAT_DIM, page), F32),
                        pltpu.VMEM((LAT_DIM, n_pages * page), BF16),
                        pltpu.SemaphoreType.DMA((PAGE_RING,))],
    )
    return pl.pallas_call(
        functools.partial(_attn_sample_kernel, n_pages=n_pages, t_seq=t_seq, layer=layer),
        grid_spec=grid_spec,
        out_shape=jax.ShapeDtypeStruct((n_seq, rows, KV_RANK), BF16),
        compiler_params=_params(1),
        name="attn_sample",
    )(page_table, q_seq, knew_seq, cache_t)


def _ffn_kernel(*refs, is_sample, is_last, t_seq, tiles_per_seq):
    (x_ref, hm_ref, ol_ref, p_ref, wuv_ref, wout_ref, gffn_ref, wup_ref, wc_ref, cb_ref,
     wdown_ref, gple_ref, wpg_ref, wpp_ref) = refs[:14]
    refs = refs[14:]
    if is_last:
        gfin_ref, refs = refs[0], refs[1:]
    if is_sample:
        st_ref, x_out, st_out = refs
    else:
        x_out, tail_ref = refs
    tm = x_ref.shape[0]
    d_ff = wdown_ref.shape[0]

    ha = [_dot(ol_ref[hd], wuv_ref[hd]).astype(BF16) for hd in range(A_HEADS)]
    mix = jnp.concatenate([hm_ref[...]] + ha, axis=1)
    x = x_ref[...] + _dot(mix, wout_ref[...])

    hf = _rms(x, gffn_ref[...]).astype(BF16)
    row = lax.broadcasted_iota(jnp.int32, (tm, d_ff), 0)
    if is_sample:
        n_st = (CONV_W - 1) * (tm // t_seq)
        tok = lax.broadcasted_iota(jnp.int32, (tm, n_st), 0)
        srow = lax.broadcasted_iota(jnp.int32, (tm, n_st), 1)
        seq0 = (CONV_W - 1) * (tok // t_seq)
        tpos = tok % t_seq
        prev1 = ((tpos == 0) & (srow == seq0 + 1)).astype(BF16)
        prev2 = ((tpos <= 1) & (srow == seq0 + tpos)).astype(BF16)
        srow_t = lax.broadcasted_iota(jnp.int32, (n_st, tm), 0)
        tok_t = lax.broadcasted_iota(jnp.int32, (n_st, tm), 1)
        keep = (tok_t == t_seq * (srow_t // (CONV_W - 1)) + t_seq - (CONV_W - 1)
                + srow_t % (CONV_W - 1)).astype(BF16)
    else:
        @pl.when(pl.program_id(0) % tiles_per_seq == 0)
        def _():
            tail_ref[...] = jnp.zeros_like(tail_ref)

    def conv(cols):
        u = _dot(hf, wup_ref[:, cols])
        r1 = pltpu.roll(u, 1, 0)
        r2 = pltpu.roll(u, 2, 0)
        if is_sample:
            state = st_ref[:, cols]
            pos = row % t_seq
            u1 = jnp.where(pos >= 1, r1, _exact_onehot_dot(prev1, state))
            u2 = jnp.where(pos >= 2, r2, _exact_onehot_dot(prev2, state))
            st_out[:, cols] = _exact_onehot_dot(keep, u)
        else:
            prev = tail_ref[:, cols]
            last, last2 = prev[SUBLANES - 1:SUBLANES, :], prev[SUBLANES - 2:SUBLANES - 1, :]
            u1 = jnp.where(row >= 1, r1, last)
            u2 = jnp.where(row >= 2, r2, jnp.where(row == 1, last, last2))
            tail_ref[:, cols] = u[tm - SUBLANES:tm, :]
        return (u2 * wc_ref[0:1, cols] + u1 * wc_ref[1:2, cols]) + u * wc_ref[2:3, cols] + cb_ref[:, cols]

    gate = conv(slice(0, d_ff))
    up = conv(slice(d_ff, 2 * d_ff))
    x = x + _dot((gate * _sigmoid(gate) * up).astype(BF16), wdown_ref[...])

    hp = _rms(x, gple_ref[...]).astype(BF16)
    x = x + _sigmoid(_dot(hp, wpg_ref[...])) * _dot(p_ref[...].astype(BF16), wpp_ref[...])
    x_out[...] = _rms(x, gfin_ref[...]) if is_last else x


def _ffn(x, hm, olat, p_all, layer, lw, *, final_gain, conv_state, n_seq, t_seq):
    n, d = x.shape
    d_up = lw['w_up'][0].shape[2]
    is_sample = conv_state is not None
    tm = SAMPLE_FFN_TILE if is_sample else FFN_TILE
    is_last = final_gain is not None
    row = lambda i: (i, 0)
    weights = [lw['w_uv'], lw['w_out'], lw['norm_ffn'], lw['w_up'], lw['ffn_conv'], lw['ffn_conv_b'],
               lw['w_down'], lw['ple_norm'], lw['w_pg'], lw['w_pp']]
    if is_last:
        weights.append(final_gain)
    weight_specs, weight_args = zip(*(_weight(w) for w in weights))
    args = [x, hm, olat, p_all] + list(weight_args)
    in_specs = [pl.BlockSpec((tm, d), row), pl.BlockSpec((tm, 512), row),
                pl.BlockSpec((A_HEADS, tm, KV_RANK), lambda i: (0, i, 0)),
                pl.BlockSpec((None, tm, p_all.shape[2]), lambda i: (layer, i, 0))] + list(weight_specs)
    out_shape = [jax.ShapeDtypeStruct((n, d), F32)]
    out_specs = [pl.BlockSpec((tm, d), row)]
    if is_sample:
        n_st = (CONV_W - 1) * (tm // t_seq)
        args.append(conv_state)
        in_specs.append(pl.BlockSpec((n_st, d_up), row))
        out_shape.append(jax.ShapeDtypeStruct(conv_state.shape, F32))
        out_specs.append(pl.BlockSpec((n_st, d_up), row))
        tiles_per_seq = 1
    else:
        tiles_per_seq = n // n_seq // tm
        out_shape.append(jax.ShapeDtypeStruct((n_seq, SUBLANES, d_up), F32))
        out_specs.append(pl.BlockSpec((None, SUBLANES, d_up), lambda i: (i // tiles_per_seq, 0, 0)))
    return pl.pallas_call(
        functools.partial(_ffn_kernel, is_sample=is_sample, is_last=is_last, t_seq=t_seq,
                          tiles_per_seq=tiles_per_seq),
        grid=(n // tm,),
        in_specs=in_specs,
        out_specs=tuple(out_specs),
        out_shape=tuple(out_shape),
        compiler_params=_params(1),
        name="ffn_sample" if is_sample else "ffn_prompt",
    )(*args)


def _swap_halves(w):
    half = w.shape[-1] // 2
    return jnp.concatenate([w[..., half:], w[..., :half]], axis=-1)


def _stacked_weights(w_out, ffn_w_up, ffn_w_down, ple_w_gate):
    return {'w_out': w_out.astype(BF16), 'w_up': ffn_w_up.astype(BF16),
            'w_down': ffn_w_down.astype(BF16), 'w_pg': ple_w_gate.astype(BF16)}


def _layer_weights(l, stacked, norm_mix, w_in, m_gate_bias, m_norm, mla_q_norm, mla_w_uq, mla_kv_norm, mla_w_uk,
                   mla_w_uv, w_out, norm_ffn, ffn_w_up, ffn_conv, ffn_conv_b, ffn_w_down, ple_norm,
                   ple_w_gate, ple_w_proj):
    d = w_in.shape[1]
    wi = w_in[l]
    sizes = (256, 256, 512, 512, M_HEADS, M_HEADS, Q_RANK, KV_RANK, A_ROPE)
    offs = [0]
    for s in sizes:
        offs.append(offs[-1] + s)
    qm, km, vm, om, im, fm, cq, ckv, kr = (wi[:, offs[i]:offs[i + 1]] for i in range(len(sizes)))
    gate_pad = jnp.zeros((d, LANES - 2 * M_HEADS), wi.dtype)
    w_in_r = jnp.concatenate([fm, im, gate_pad, cq, ckv, kr, _swap_halves(kr), vm, om, qm, km], axis=1)
    uq = mla_w_uq[l].reshape(Q_RANK, A_HEADS, A_NOPE + A_ROPE)
    uq_rope = uq[:, :, A_NOPE:]
    w_uq_r = jnp.concatenate([uq[:, :, :A_NOPE].reshape(Q_RANK, A_HEADS * A_NOPE),
                              jnp.concatenate([uq_rope, _swap_halves(uq_rope)], axis=-1)
                              .reshape(Q_RANK, A_HEADS * LANES)], axis=1)
    row = lambda v: v.reshape(1, -1)
    return {
        'norm_mix': row(norm_mix[l]),
        'w_in': w_in_r.astype(BF16),
        'w_voq_t': w_in_r[:, C_V:C_K].T.astype(BF16),
        'gate_bias': jnp.concatenate([m_gate_bias[l, 1], m_gate_bias[l, 0]]).reshape(SUBLANES, 1),
        'm_norm': m_norm[l],
        'mla_q_norm': row(mla_q_norm[l]),
        'w_uq': w_uq_r.astype(BF16),
        'mla_kv_norm': row(mla_kv_norm[l]),
        'w_uk': jnp.transpose(mla_w_uk[l], (1, 2, 0)).astype(BF16),
        'w_uv': jnp.transpose(mla_w_uv[l], (1, 0, 2)).astype(BF16),
        'w_out': (stacked['w_out'], l),
        'norm_ffn': row(norm_ffn[l]),
        'w_up': (stacked['w_up'], l),
        'ffn_conv': ffn_conv[l],
        'ffn_conv_b': row(ffn_conv_b[l]),
        'w_down': (stacked['w_down'], l),
        'ple_norm': row(ple_norm[l]),
        'w_pg': (stacked['w_pg'], l),
        'w_pp': ple_w_proj[l].astype(BF16),
    }


def _rope_table(pos):
    half = A_ROPE // 2
    freqs = ROPE_THETA ** (-jnp.arange(half, dtype=F32) * 2.0 / A_ROPE)
    ang = pos.astype(F32)[:, None] * freqs[None, :]
    cos, sin = jnp.cos(ang), jnp.sin(ang)
    return jnp.concatenate([cos, cos, -sin, sin], axis=1)


def kernel(x_prompt, x_sample, p_prompt, p_sample, cache_mla, state_mlstm_C, state_mlstm_n, state_mlstm_m,
           state_ffn_conv, page_table, norm_mix, w_in, m_gate_bias, m_norm, mla_q_norm, mla_w_uq, mla_kv_norm,
           mla_w_uk, mla_w_uv, w_out, norm_ffn, ffn_w_up, ffn_conv, ffn_conv_b, ffn_w_down, ple_norm,
           ple_w_gate, ple_w_proj, final_norm):
    depth = w_in.shape[0]
    bp, tp, d = x_prompt.shape
    bs, ts, _ = x_sample.shape
    n_p, n_s = bp * tp, bs * ts
    past_len = page_table.shape[1] * cache_mla.shape[2]
    d_up = ffn_w_up.shape[2]

    tab_p = jnp.tile(_rope_table(jnp.arange(tp)), (bp, 1))
    tab_s = jnp.tile(_rope_table(past_len + jnp.arange(ts)), (bs, 1))
    xp = x_prompt.reshape(n_p, d)
    xs = x_sample.reshape(n_s, d)
    final_gain = final_norm.reshape(1, d)
    cache_t = jnp.swapaxes(cache_mla, 2, 3)

    stacked = _stacked_weights(w_out, ffn_w_up, ffn_w_down, ple_w_gate)
    lat_p, lat_s, c_p, n_pl, m_p, c_s, n_sl, m_s, conv_p, conv_s = ([] for _ in range(10))
    for l in range(depth):
        lw = _layer_weights(l, stacked, norm_mix, w_in, m_gate_bias, m_norm, mla_q_norm, mla_w_uq, mla_kv_norm,
                            mla_w_uk, mla_w_uv, w_out, norm_ffn, ffn_w_up, ffn_conv, ffn_conv_b,
                            ffn_w_down, ple_norm, ple_w_gate, ple_w_proj)
        fin = final_gain if l == depth - 1 else None

        km, qt, vt, ot, gscan, gdiff, dcol, lat, katt, katt_t, qatt_t = _proj(xp, tab_p, lw, True, bp)
        by_seq = lambda a: a.reshape(bp, tp, a.shape[-1])
        hm, state, m_new = _mlstm_prompt(by_seq(km), qt, vt, ot, gscan, gdiff, by_seq(dcol), lw['m_norm'].T, bp)
        c_new = jnp.swapaxes(state[:, :, :M_DV, :], 2, 3)
        n_new = state[:, :, M_DV, :]
        olat = _attn_prompt(qatt_t, katt, katt_t, bp)
        xp, tail = _ffn(xp, hm.reshape(n_p, -1), olat, p_prompt.reshape(depth, n_p, -1), l, lw, final_gain=fin,
                        conv_state=None, n_seq=bp, t_seq=tp)
        lat_p.append(lat.reshape(bp, tp, LAT_DIM))
        c_p.append(c_new)
        n_pl.append(n_new)
        m_p.append(m_new[:, :M_HEADS, 0])
        conv_p.append(tail[:, SUBLANES - (CONV_W - 1):])

        qk, v, o, gates, lat, katt, qatt = _proj(xs, tab_s, lw, False, 1)
        m_prev_rows = jnp.repeat(state_mlstm_m[l].T, ts, axis=1)
        hm, c_new, n_new_t, m_rows = _mlstm_sample(qk, v, o, gates, m_prev_rows, lw['m_norm'],
                                                   state_mlstm_C, jnp.transpose(state_mlstm_n[l], (1, 0, 2)),
                                                   l, ts)
        q_seq = jnp.transpose(qatt.reshape(A_HEADS, bs, ts, LAT_PAD), (1, 0, 2, 3)).reshape(bs, A_HEADS * ts, LAT_PAD)
        o_seq = _attn_sample(q_seq, katt.reshape(bs, ts, LAT_PAD), cache_t, page_table, l)
        olat = jnp.transpose(o_seq.reshape(bs, A_HEADS, ts, KV_RANK), (1, 0, 2, 3)).reshape(A_HEADS, n_s, KV_RANK)
        xs, st_new = _ffn(xs, hm, olat, p_sample.reshape(depth, n_s, -1), l, lw, final_gain=fin,
                          conv_state=state_ffn_conv[l].reshape(bs * (CONV_W - 1), d_up), n_seq=bs, t_seq=ts)
        lat_s.append(lat.reshape(bs, ts, LAT_DIM))
        c_s.append(c_new)
        n_sl.append(jnp.transpose(n_new_t, (1, 0, 2)))
        m_s.append(m_rows[:, ts - 1::ts].T)
        conv_s.append(st_new.reshape(bs, CONV_W - 1, d_up))

    return (xp.reshape(bp, tp, d), xs.reshape(bs, ts, d), jnp.stack(lat_p), jnp.stack(lat_s),
            jnp.stack(c_p), jnp.stack(n_pl), jnp.stack(m_p), jnp.stack(c_s), jnp.stack(n_sl), jnp.stack(m_s),
            jnp.stack(conv_p), jnp.stack(conv_s))
```

```python
import functools

import jax
import jax.numpy as jnp
from jax import lax
from jax.experimental import pallas as pl
from jax.experimental.pallas import tpu as pltpu

F32 = jnp.float32
BF16 = jnp.bfloat16

M_HEADS = 4
M_DK = 64
M_DV = 128
A_HEADS = 4
A_NOPE = 128
A_ROPE = 64
KV_RANK = 256
Q_RANK = 384
LAT_DIM = KV_RANK + A_ROPE
GATE_CAP = 15.0
ROPE_THETA = 10000.0
EPS = 1e-6
CONV_W = 3
ATT_SCALE = (A_NOPE + A_ROPE) ** -0.5
LOG2_E = 1.4426950408889634
Q_SCALE = ATT_SCALE * LOG2_E

LANES = 128
SUBLANES = 8
VMEM_LIMIT_BYTES = 56 * 1024 * 1024

LAT_PAD = 3 * LANES
TOK_TILE = 256
MLSTM_CHUNK = 256
ATT_BLOCK = 1024
SAMPLE_GROUP = 32
FFN_TILE = 512
PROJ_TILE = 1024
PAGE_RING = 3
SAMPLE_FFN_TILE = 128
STATE_ROWS = M_DV + SUBLANES

C_G, C_CQ, C_CKV, C_KR = 0, 128, 512, 768
C_V, C_O, C_Q, C_K = 896, 1408, 1920, 2176
IN_COLS = 2432


def _const_spec(shape):
    nd = len(shape)
    return pl.BlockSpec(shape, lambda *_: (0,) * nd, pipeline_mode=pl.Buffered(1))


def _weight(w):
    if not isinstance(w, tuple):
        return _const_spec(w.shape), w
    arr, layer = w
    tail = (0,) * (arr.ndim - 1)
    return pl.BlockSpec((None,) + arr.shape[1:], lambda *_: (layer,) + tail, pipeline_mode=pl.Buffered(1)), arr


def _params(n_axes):
    return pltpu.CompilerParams(dimension_semantics=("arbitrary",) * n_axes,
                                vmem_limit_bytes=VMEM_LIMIT_BYTES)


def _rms(x, g):
    return x * lax.rsqrt(jnp.mean(x * x, axis=-1, keepdims=True) + EPS) * g


def _sigmoid(x):
    return 1.0 / (1.0 + jnp.exp(-x))


def _dot(a, b):
    return jnp.dot(a, b, preferred_element_type=F32)


def _dot_nt(a, b):
    return lax.dot_general(a, b, (((1,), (1,)), ((), ())), preferred_element_type=F32)


def _fold_halves(p):
    return p + pltpu.roll(p, 64, 1)


def _proj_kernel(x_ref, tab_ref, gmix_ref, win_ref, gb_ref, gq_ref, wuq_ref, gkv_ref, wuk_ref,
                 *refs, feature_major):
    if feature_major:
        (wvoq_ref, km_ref, qt_ref, vt_ref, ot_ref, gscan_ref, gdiff_ref, dcol_ref,
         lat_ref, katt_ref, katt_t_ref, qatt_ref) = refs
    else:
        qk_ref, v_ref, o_ref, gates_ref, lat_ref, katt_ref, qatt_ref = refs
    h = _rms(x_ref[...], gmix_ref[...]).astype(BF16)
    z = _dot(h, win_ref[:, 0:C_V])

    gt = z[:, C_G:C_G + LANES].T[0:SUBLANES, :] + gb_ref[...]
    capped = GATE_CAP * jnp.tanh(gt / GATE_CAP)
    log_sig = jnp.minimum(capped, 0.0) - jnp.log1p(jnp.exp(-jnp.abs(capped)))
    row = lax.broadcasted_iota(jnp.int32, gt.shape, 0)
    gates = jnp.where(row < M_HEADS, log_sig, capped)
    if feature_major:
        b, d, e = _gate_scans(gates, MLSTM_CHUNK)
        gscan_ref[...] = jnp.where(row < M_HEADS, b, pltpu.roll(e, M_HEADS, 0))
        d = jnp.where(row < M_HEADS, d, 0.0)
        gdiff_ref[...] = d
        dcol_ref[...] = _to_cols((d,), d.shape[1])
    else:
        gates_ref[...] = gates

    tab = tab_ref[...]
    lane = lax.broadcasted_iota(jnp.int32, tab.shape, 1)
    ckv_n = _rms(z[:, C_CKV:C_CKV + KV_RANK], gkv_ref[...])
    rk = _fold_halves(z[:, C_KR:C_KR + LANES] * tab)
    lat_ref[:, 0:KV_RANK] = ckv_n
    lat_ref[:, KV_RANK:LAT_DIM] = rk[:, 0:A_ROPE]
    k_cat = jnp.concatenate([ckv_n, jnp.where(lane < A_ROPE, rk, 0.0)], axis=1)
    katt_ref[...] = k_cat.astype(BF16)
    if feature_major:
        katt_t_ref[...] = k_cat.T.astype(BF16)

    qn = _rms(z[:, C_CQ:C_CQ + Q_RANK], gq_ref[...]).astype(BF16)
    qa = _dot(qn, wuq_ref[...])
    for hd in range(A_HEADS):
        q_lat = _dot(qa[:, hd * A_NOPE:(hd + 1) * A_NOPE].astype(BF16), wuk_ref[hd])
        rq = _fold_halves(qa[:, 512 + hd * LANES:512 + (hd + 1) * LANES] * tab)
        q_cat = jnp.concatenate([q_lat, jnp.where(lane < A_ROPE, rq, 0.0)], axis=1) * Q_SCALE
        qatt_ref[hd] = (q_cat.T if feature_major else q_cat).astype(BF16)

    k_m = (_dot(h, win_ref[:, C_K:C_K + 256]) * (M_DK ** -0.5)).astype(BF16)
    if feature_major:
        zt = _dot_nt(wvoq_ref[...], h)
        vt_ref[...] = zt[0:512, :].astype(BF16)
        ot_ref[...] = zt[512:1024, :]
        qt_ref[...] = zt[1024:1280, :].astype(BF16)
        km_ref[...] = k_m
    else:
        z2 = _dot(h, win_ref[:, C_V:C_K])
        v_ref[...] = z2[:, 0:512].astype(BF16)
        o_ref[...] = z2[:, 512:1024]
        qk_ref[:, 0:256] = z2[:, 1024:1280].astype(BF16)
        qk_ref[:, 256:512] = k_m


def _proj(x, tab, lw, feature_major, gate_seqs):
    n = x.shape[0]
    tm = PROJ_TILE if feature_major else TOK_TILE
    tiles_per_seq = n // gate_seqs // tm
    row = lambda i: (i, 0)
    col = lambda i: (0, i)
    band = lambda i: (i // tiles_per_seq, i % tiles_per_seq)
    gate_rows = jax.ShapeDtypeStruct((gate_seqs * SUBLANES, n // gate_seqs), F32)
    if feature_major:
        assert tm % MLSTM_CHUNK == 0
        out_shape = [
            jax.ShapeDtypeStruct((n, 256), BF16),
            jax.ShapeDtypeStruct((256, n), BF16),
            jax.ShapeDtypeStruct((512, n), BF16),
            jax.ShapeDtypeStruct((512, n), F32),
            gate_rows,
            gate_rows,
            jax.ShapeDtypeStruct((n, LANES), F32),
        ]
        out_specs = [
            pl.BlockSpec((tm, 256), row),
            pl.BlockSpec((256, tm), col),
            pl.BlockSpec((512, tm), col),
            pl.BlockSpec((512, tm), col),
            pl.BlockSpec((SUBLANES, tm), band),
            pl.BlockSpec((SUBLANES, tm), band),
            pl.BlockSpec((tm, LANES), row),
        ]
    else:
        out_shape = [
            jax.ShapeDtypeStruct((n, 512), BF16),
            jax.ShapeDtypeStruct((n, 512), BF16),
            jax.ShapeDtypeStruct((n, 512), F32),
            gate_rows,
        ]
        out_specs = [
            pl.BlockSpec((tm, 512), row),
            pl.BlockSpec((tm, 512), row),
            pl.BlockSpec((tm, 512), row),
            pl.BlockSpec((SUBLANES, tm), band),
        ]
    out_shape += [jax.ShapeDtypeStruct((n, LAT_DIM), F32),
                  jax.ShapeDtypeStruct((n, LAT_PAD), BF16)]
    out_specs += [pl.BlockSpec((tm, LAT_DIM), row), pl.BlockSpec((tm, LAT_PAD), row)]
    if feature_major:
        out_shape += [jax.ShapeDtypeStruct((LAT_PAD, n), BF16),
                      jax.ShapeDtypeStruct((A_HEADS, LAT_PAD, n), BF16)]
        out_specs += [pl.BlockSpec((LAT_PAD, tm), col),
                      pl.BlockSpec((A_HEADS, LAT_PAD, tm), lambda i: (0, 0, i))]
    else:
        out_shape.append(jax.ShapeDtypeStruct((A_HEADS, n, LAT_PAD), BF16))
        out_specs.append(pl.BlockSpec((A_HEADS, tm, LAT_PAD), lambda i: (0, i, 0)))
    return pl.pallas_call(
        functools.partial(_proj_kernel, feature_major=feature_major),
        grid=(n // tm,),
        in_specs=[
            pl.BlockSpec((tm, x.shape[1]), row),
            pl.BlockSpec((tm, LANES), lambda i: (i % (tab.shape[0] // tm), 0)),
            _const_spec(lw['norm_mix'].shape),
            _const_spec(lw['w_in'].shape),
            _const_spec(lw['gate_bias'].shape),
            _const_spec(lw['mla_q_norm'].shape),
            _const_spec(lw['w_uq'].shape),
            _const_spec(lw['mla_kv_norm'].shape),
            _const_spec(lw['w_uk'].shape),
        ] + ([_const_spec(lw['w_voq_t'].shape)] if feature_major else []),
        out_specs=tuple(out_specs),
        out_shape=tuple(out_shape),
        compiler_params=_params(1),
        name="proj_prompt" if feature_major else "proj_sample",
    )(x, tab, lw['norm_mix'], lw['w_in'], lw['gate_bias'], lw['mla_q_norm'], lw['w_uq'],
      lw['mla_kv_norm'], lw['w_uk'], *([lw['w_voq_t']] if feature_major else []))


def _seg_scan(x, op, fill, seg):
    n = x.shape[1]
    lane = lax.broadcasted_iota(jnp.int32, x.shape, 1)
    pos = lane if seg is None else lane % seg
    span = n if seg is None else seg
    s = 1
    while s < span:
        x = op(x, jnp.where(pos >= s, pltpu.roll(x, s, 1), fill))
        s *= 2
    return x


def _gate_scans(gates, seg):
    ig = pltpu.roll(gates, M_HEADS, 0)
    b = _seg_scan(gates, jnp.add, 0.0, seg)
    d = ig - b
    e = _seg_scan(d, jnp.maximum, -jnp.inf, seg)
    return b, d, e


def _gate_rows(gates, m_prev, seg):
    b, d, e = _gate_scans(gates, seg)
    return b, d, pltpu.roll(gates, M_HEADS, 0), b + jnp.maximum(m_prev, e)


def _to_cols(row_blocks, n_tok):
    pad = jnp.zeros((LANES - SUBLANES * len(row_blocks), n_tok), F32)
    return jnp.concatenate(list(row_blocks) + [pad], axis=0).T


def _head_out(hh, og, gain):
    hn = hh * lax.rsqrt(jnp.mean(hh * hh, axis=-1, keepdims=True) + EPS) * gain
    return (hn * _sigmoid(og)).astype(BF16)


def _mlstm_prompt_kernel(*refs, n_seq):
    km_ref, gscan_ref, gdiff_ref, dcol_ref, gain_ref = refs[:5]
    qt_refs = refs[5:5 + n_seq]
    vt_refs = refs[5 + n_seq:5 + 2 * n_seq]
    ot_refs = refs[5 + 2 * n_seq:5 + 3 * n_seq]
    hm_ref, s_ref, m_ref = refs[5 + 3 * n_seq:]
    L = km_ref.shape[1]

    @pl.when(pl.program_id(0) == 0)
    def _():
        s_ref[...] = jnp.zeros_like(s_ref)
        m_ref[...] = jnp.zeros_like(m_ref)

    ki = lax.broadcasted_iota(jnp.int32, (L, L), 0)
    qi = lax.broadcasted_iota(jnp.int32, (L, L), 1)
    causal = ki <= qi
    pad_rows = jnp.zeros((STATE_ROWS - M_DV - 1, L), F32)

    for sq in range(n_seq):
        band = slice(sq * SUBLANES, (sq + 1) * SUBLANES)
        scan = gscan_ref[band, :]
        diff = gdiff_ref[band, :]
        ks = [km_ref[sq, :, hd * M_DK:(hd + 1) * M_DK] for hd in range(M_HEADS)]
        qts = [qt_refs[sq][hd * M_DK:(hd + 1) * M_DK, :] for hd in range(M_HEADS)]
        states = [s_ref[sq, hd] for hd in range(M_HEADS)]
        kq = [_dot(ks[hd], qts[hd]) for hd in range(M_HEADS)]
        q_state = [_dot(states[hd].astype(BF16), qts[hd]) for hd in range(M_HEADS)]

        for hd in range(M_HEADS):
            vt = vt_refs[sq][hd * M_DV:(hd + 1) * M_DV, :]
            b_r = scan[hd:hd + 1, :]
            e_r = scan[M_HEADS + hd:M_HEADS + hd + 1, :]
            d_r = diff[hd:hd + 1, :]
            d_c = dcol_ref[sq, :, hd:hd + 1]
            mp = m_ref[sq, hd:hd + 1, 0:1]
            mx_r = jnp.maximum(mp, e_r)
            mx_last = mx_r[:, L - 1:L]

            st = kq[hd] * jnp.exp(jnp.where(causal, d_c - mx_r, -jnp.inf))
            a_r = jnp.exp(mp - mx_r)
            num = _dot(vt, st.astype(BF16)) + a_r * q_state[hd][0:M_DV, :]
            nq = jnp.sum(st, axis=0, keepdims=True) + a_r * q_state[hd][M_DV:M_DV + 1, :]
            inv = 1.0 / jnp.maximum(jnp.abs(nq), jnp.exp(-(b_r + mx_r)))
            rs = lax.rsqrt(inv * inv * jnp.mean(num * num, axis=0, keepdims=True) + EPS)
            ht = num * (inv * rs) * gain_ref[:, hd:hd + 1] * _sigmoid(ot_refs[sq][hd * M_DV:(hd + 1) * M_DV, :])
            hm_ref[sq, :, hd * M_DV:(hd + 1) * M_DV] = ht.T.astype(BF16)

            w_r = jnp.exp(d_r - mx_last)
            lhs = jnp.concatenate([vt.astype(F32) * w_r, w_r, pad_rows], axis=0).astype(BF16)
            s_ref[sq, hd] = jnp.exp(mp - mx_last) * states[hd] + _dot(lhs, ks[hd])
            m_ref[sq, hd:hd + 1, :] = jnp.broadcast_to(b_r[:, L - 1:L] + mx_last, (1, LANES))


def _mlstm_prompt(km, qt, vt, ot, gscan, gdiff, dcol, gain_t, n_seq):
    t = km.shape[1]
    L = MLSTM_CHUNK
    nc = t // L
    tok = lambda c: (0, c, 0)
    whole4 = lambda c: (0, 0, 0, 0)
    whole3 = lambda c: (0, 0, 0)

    def seq_cols(rows):
        return [pl.BlockSpec((rows, L), lambda c, sq=sq: (0, sq * nc + c)) for sq in range(n_seq)]

    return pl.pallas_call(
        functools.partial(_mlstm_prompt_kernel, n_seq=n_seq),
        grid=(nc,),
        in_specs=[
            pl.BlockSpec((n_seq, L, 256), tok),
            pl.BlockSpec((n_seq * SUBLANES, L), lambda c: (0, c)),
            pl.BlockSpec((n_seq * SUBLANES, L), lambda c: (0, c)),
            pl.BlockSpec((n_seq, L, LANES), tok),
            _const_spec(gain_t.shape),
        ] + seq_cols(256) + seq_cols(512) + seq_cols(512),
        out_specs=(
            pl.BlockSpec((n_seq, L, 512), tok),
            pl.BlockSpec((n_seq, M_HEADS, STATE_ROWS, M_DK), whole4),
            pl.BlockSpec((n_seq, SUBLANES, LANES), whole3),
        ),
        out_shape=(
            jax.ShapeDtypeStruct((n_seq, t, 512), BF16),
            jax.ShapeDtypeStruct((n_seq, M_HEADS, STATE_ROWS, M_DK), F32),
            jax.ShapeDtypeStruct((n_seq, SUBLANES, LANES), F32),
        ),
        compiler_params=_params(1),
        name="mlstm_prompt",
    )(km, gscan, gdiff, dcol, gain_t, *([qt] * n_seq), *([vt] * n_seq), *([ot] * n_seq))


def _exact_onehot_dot(onehot, x):
    hi = x.astype(BF16)
    r1 = x - hi.astype(F32)
    mid = r1.astype(BF16)
    lo = (r1 - mid.astype(F32)).astype(BF16)
    return (_dot(onehot, hi) + _dot(onehot, mid)) + _dot(onehot, lo)


def _mlstm_sample_kernel(qk_ref, v_ref, o_ref, g_ref, mp_ref, gain_ref, c0_ref, n0_ref,
                         hm_ref, c_ref, n_ref, m_ref, cols_ref, *, t_seq):
    L = qk_ref.shape[0]
    G = L // t_seq
    mp_rows = jnp.concatenate([mp_ref[...], jnp.zeros((SUBLANES - M_HEADS, L), F32)], axis=0)
    b, d, ig, m = _gate_rows(g_ref[...], mp_rows, t_seq)
    m_ref[...] = m[0:M_HEADS, :]

    lane = lax.broadcasted_iota(jnp.int32, b.shape, 1)
    pos = lane % t_seq

    def seg_last(x):
        s = 1
        while s < t_seq:
            x = jnp.where(pos >= t_seq - s, x, pltpu.roll(x, L - s, 1))
            s *= 2
        return x

    b_last = seg_last(b)
    m_new = seg_last(m)
    w_rows = jnp.exp(b_last - b + ig - m_new)
    decay_rows = jnp.exp(b_last + mp_rows - m_new)
    cols_ref[...] = _to_cols((b, m, mp_rows, w_rows, decay_rows), L)
    cols = cols_ref[...]
    seq_cols = cols_ref[pl.ds(0, G, stride=t_seq), :]

    qi = lax.broadcasted_iota(jnp.int32, (L, L), 0)
    ki = lax.broadcasted_iota(jnp.int32, (L, L), 1)
    mask = (ki <= qi) & (ki // t_seq == qi // t_seq)
    er = lax.broadcasted_iota(jnp.int32, (L, G * M_DK), 0)
    ec = lax.broadcasted_iota(jnp.int32, (L, G * M_DK), 1)
    own = (ec // M_DK) == (er // t_seq)
    tok_of_seq = (lax.broadcasted_iota(jnp.int32, (L, G), 0) // t_seq
                  == lax.broadcasted_iota(jnp.int32, (L, G), 1)).astype(BF16)
    seq_of_tok = (lax.broadcasted_iota(jnp.int32, (G, L), 1) // t_seq
                  == lax.broadcasted_iota(jnp.int32, (G, L), 0)).astype(BF16)

    def expand(x):
        x2 = jnp.concatenate([x, x], axis=1)
        return jnp.where(own, jnp.tile(x2, (1, G // 2)), 0.0)

    for hd in range(M_HEADS):
        q = qk_ref[:, hd * M_DK:(hd + 1) * M_DK]
        k = qk_ref[:, 256 + hd * M_DK:256 + (hd + 1) * M_DK]
        v = v_ref[:, hd * M_DV:(hd + 1) * M_DV]
        b_c = cols[:, hd:hd + 1]
        m_c = cols[:, 8 + hd:9 + hd]
        mp_c = cols[:, 16 + hd:17 + hd]
        w_c = cols[:, 24 + hd:25 + hd]
        decay_seq = seq_cols[:, 32 + hd:33 + hd]
        n_old = n0_ref[hd]
        c_old = c0_ref[:, hd].reshape(G * M_DK, M_DV)

        p = jnp.exp(jnp.where(mask, b_c + d[hd:hd + 1, :] - m_c, -jnp.inf))
        s = _dot_nt(q, k) * p
        a_c = jnp.exp(b_c + mp_c - m_c)
        qf = q.astype(F32)
        num = _dot(s.astype(BF16), v) + a_c * _dot(expand(qf).astype(BF16), c_old.astype(BF16))
        n_tok = _exact_onehot_dot(tok_of_seq, n_old)
        nq = jnp.sum(s, axis=1, keepdims=True) + a_c * jnp.sum(qf * n_tok, axis=1, keepdims=True)
        den = jnp.maximum(jnp.abs(nq), jnp.exp(-m_c))
        sl = slice(hd * M_DV, (hd + 1) * M_DV)
        hm_ref[:, sl] = _head_out(num / den, o_ref[:, sl], gain_ref[hd:hd + 1, :])

        kw = k.astype(F32) * w_c
        upd = _dot(expand(kw).T.astype(BF16), v)
        for g in range(G):
            c_ref[g, hd] = decay_seq[g:g + 1, :] * c0_ref[g, hd] + upd[g * M_DK:(g + 1) * M_DK, :]
        n_ref[hd] = decay_seq * n_old + _dot(seq_of_tok, kw.astype(BF16))


def _mlstm_sample(qk, v, o, gates, m_prev_rows, gain, c0, n0_t, layer, t_seq):
    n = qk.shape[0]
    G = SAMPLE_GROUP
    L = G * t_seq
    n_seq = n // t_seq
    tok = lambda i: (i, 0)
    return pl.pallas_call(
        functools.partial(_mlstm_sample_kernel, t_seq=t_seq),
        grid=(n // L,),
        in_specs=[
            pl.BlockSpec((L, 512), tok),
            pl.BlockSpec((L, 512), tok),
            pl.BlockSpec((L, 512), tok),
            pl.BlockSpec((SUBLANES, L), lambda i: (0, i)),
            pl.BlockSpec((M_HEADS, L), lambda i: (0, i)),
            _const_spec(gain.shape),
            pl.BlockSpec((None, G, M_HEADS, M_DK, M_DV), lambda i: (layer, i, 0, 0, 0)),
            pl.BlockSpec((M_HEADS, G, M_DK), lambda i: (0, i, 0)),
        ],
        out_specs=(
            pl.BlockSpec((L, 512), tok),
            pl.BlockSpec((G, M_HEADS, M_DK, M_DV), lambda i: (i, 0, 0, 0)),
            pl.BlockSpec((M_HEADS, G, M_DK), lambda i: (0, i, 0)),
            pl.BlockSpec((M_HEADS, L), lambda i: (0, i)),
        ),
        out_shape=(
            jax.ShapeDtypeStruct((n, 512), BF16),
            jax.ShapeDtypeStruct((n_seq, M_HEADS, M_DK, M_DV), F32),
            jax.ShapeDtypeStruct((M_HEADS, n_seq, M_DK), F32),
            jax.ShapeDtypeStruct((M_HEADS, n), F32),
        ),
        scratch_shapes=[pltpu.VMEM((L, LANES), F32)],
        compiler_params=_params(1),
        name="mlstm_sample",
    )(qk, v, o, gates, m_prev_rows, gain, c0, n0_t)


def _attn_prompt_kernel(qt_ref, k_ref, kt_ref, o_ref, m_scr, l_scr, acc_scr, s0_scr):
    tq = qt_ref.shape[2]
    qb = pl.program_id(1)
    m_scr[...] = jnp.full_like(m_scr, -jnp.inf)
    l_scr[...] = jnp.zeros_like(l_scr)
    acc_scr[...] = jnp.zeros_like(acc_scr)

    def keys(kb):
        return k_ref[pl.ds(pl.multiple_of(kb * tq, tq), tq), :]

    s0_scr[...] = _dot(keys(0), qt_ref[0])

    def block(kb, last):
        kblk = keys(kb)
        vt = kt_ref[0:KV_RANK, pl.ds(pl.multiple_of(kb * tq, tq), tq)]
        scores = {0: s0_scr[...]}
        s0_next = None
        for hd in range(A_HEADS):
            if hd + 1 < A_HEADS:
                scores[hd + 1] = _dot(kblk, qt_ref[hd + 1])
            elif not last:
                s0_next = _dot(keys(kb + 1), qt_ref[0])
            st = scores.pop(hd)
            if last:
                kpos = lax.broadcasted_iota(jnp.int32, st.shape, 0)
                qpos = lax.broadcasted_iota(jnp.int32, st.shape, 1)
                st = jnp.where(kpos <= qpos, st, -jnp.inf)
            m_old = m_scr[hd:hd + 1, :]
            m_new = jnp.maximum(m_old, jnp.max(st, axis=0, keepdims=True))
            alpha = jnp.exp2(m_old - m_new)
            p = jnp.exp2(st - m_new)
            l_scr[hd:hd + 1, :] = alpha * l_scr[hd:hd + 1, :] + jnp.sum(p, axis=0, keepdims=True)
            acc_scr[hd] = alpha * acc_scr[hd] + _dot(vt, p.astype(BF16))
            m_scr[hd:hd + 1, :] = m_new
        if not last:
            s0_scr[...] = s0_next

    def body(kb, carry):
        block(kb, False)
        return carry

    lax.fori_loop(0, qb, body, 0)
    block(qb, True)
    for hd in range(A_HEADS):
        o_ref[hd] = (acc_scr[hd] / l_scr[hd:hd + 1, :]).T.astype(BF16)


def _attn_prompt(qatt_t, katt, katt_t, n_seq):
    n = katt.shape[0]
    t = n // n_seq
    tq = ATT_BLOCK
    nq = t // tq
    return pl.pallas_call(
        _attn_prompt_kernel,
        grid=(n_seq, nq),
        in_specs=[
            pl.BlockSpec((A_HEADS, LAT_PAD, tq), lambda b, i: (0, 0, b * nq + i)),
            pl.BlockSpec((t, LAT_PAD), lambda b, i: (b, 0)),
            pl.BlockSpec((LAT_PAD, t), lambda b, i: (0, b)),
        ],
        out_specs=pl.BlockSpec((A_HEADS, tq, KV_RANK), lambda b, i: (0, b * nq + i, 0)),
        out_shape=jax.ShapeDtypeStruct((A_HEADS, n, KV_RANK), BF16),
        scratch_shapes=[pltpu.VMEM((SUBLANES, tq), F32), pltpu.VMEM((SUBLANES, tq), F32),
                        pltpu.VMEM((A_HEADS, KV_RANK, tq), F32), pltpu.VMEM((tq, tq), F32)],
        compiler_params=_params(2),
        name="attn_prompt",
    )(qatt_t, katt, katt_t)


def _attn_sample_kernel(pt_ref, q_ref, knew_ref, cache_ref, o_ref, pages, kbuf, sems, *, n_pages, t_seq, layer):
    s = pl.program_id(0)
    n_seq = pl.num_programs(0)
    page = pages.shape[3]

    def page_copy(seq, j):
        slot = seq % PAGE_RING
        return pltpu.make_async_copy(cache_ref.at[layer, pt_ref[seq, j]], pages.at[slot, j], sems.at[slot])

    def request(seq):
        for j in range(n_pages):
            page_copy(seq, j).start(priority=j % 2)

    @pl.when(s == 0)
    def _():
        for ahead in range(PAGE_RING - 1):
            request(ahead)

    for j in range(n_pages):
        page_copy(s, j).wait()

    @pl.when(s + PAGE_RING - 1 < n_seq)
    def _():
        request(s + PAGE_RING - 1)

    slot = s % PAGE_RING
    for j in range(n_pages):
        kbuf[:, j * page:(j + 1) * page] = pages[slot, j].astype(BF16)

    q = q_ref[...]
    rows = q.shape[0]
    s_past = _dot(q[:, 0:LAT_DIM], kbuf[...])
    qf = q.astype(F32)
    knew = knew_ref[...].astype(F32)
    tq = lax.broadcasted_iota(jnp.int32, (rows, 1), 0) % t_seq
    s_new = []
    for j in range(t_seq):
        sj = jnp.sum(qf * knew[j:j + 1, :], axis=1, keepdims=True)
        s_new.append(jnp.where(tq >= j, sj, -jnp.inf))
    m = jnp.max(s_past, axis=1, keepdims=True)
    for sj in s_new:
        m = jnp.maximum(m, sj)
    p_past = jnp.exp2(s_past - m)
    p_new = [jnp.exp2(sj - m) for sj in s_new]
    denom = jnp.sum(p_past, axis=1, keepdims=True)
    for pj in p_new:
        denom = denom + pj
    inv = 1.0 / denom
    out = _dot_nt((p_past * inv).astype(BF16), kbuf[0:KV_RANK, :])
    for j in range(t_seq):
        pj = (p_new[j] * inv).astype(BF16).astype(F32)
        out = out + pj * knew[j:j + 1, 0:KV_RANK]
    o_ref[...] = out.astype(BF16)


def _attn_sample(q_seq, knew_seq, cache_t, page_table, layer):
    n_seq, rows, _ = q_seq.shape
    t_seq = knew_seq.shape[1]
    n_pages = page_table.shape[1]
    page = cache_t.shape[3]
    assert n_seq >= PAGE_RING - 1
    grid_spec = pltpu.PrefetchScalarGridSpec(
        num_scalar_prefetch=1,
        grid=(n_seq,),
        in_specs=[pl.BlockSpec((None, rows, LAT_PAD), lambda s, pt: (s, 0, 0)),
                  pl.BlockSpec((None, t_seq, LAT_PAD), lambda s, pt: (s, 0, 0)),
                  pl.BlockSpec(memory_space=pl.ANY)],
        out_specs=pl.BlockSpec((None, rows, KV_RANK), lambda s, pt: (s, 0, 0)),
        scratch_shapes=[pltpu.VMEM((PAGE_RING, n_pages, LAT_DIM, page), F32),
                        pltpu.VMEM((LAT_DIM, n_pages * page), BF16),
                        pltpu.SemaphoreType.DMA((PAGE_RING,))],
    )
    return pl.pallas_call(
        functools.partial(_attn_sample_kernel, n_pages=n_pages, t_seq=t_seq, layer=layer),
        grid_spec=grid_spec,
        out_shape=jax.ShapeDtypeStruct((n_seq, rows, KV_RANK), BF16),
        compiler_params=_params(1),
        name="attn_sample",
    )(page_table, q_seq, knew_seq, cache_t)


def _ffn_kernel(*refs, is_sample, is_last, t_seq, tiles_per_seq):
    (x_ref, hm_ref, ol_ref, p_ref, wuv_ref, wout_ref, gffn_ref, wup_ref, wc_ref, cb_ref,
     wdown_ref, gple_ref, wpg_ref, wpp_ref) = refs[:14]
    refs = refs[14:]
    if is_last:
        gfin_ref, refs = refs[0], refs[1:]
    if is_sample:
        st_ref, x_out, st_out = refs
    else:
        x_out, tail_ref = refs
    tm = x_ref.shape[0]
    d_ff = wdown_ref.shape[0]

    ha = [_dot(ol_ref[hd], wuv_ref[hd]).astype(BF16) for hd in range(A_HEADS)]
    mix = jnp.concatenate([hm_ref[...]] + ha, axis=1)
    x = x_ref[...] + _dot(mix, wout_ref[...])

    hf = _rms(x, gffn_ref[...]).astype(BF16)
    row = lax.broadcasted_iota(jnp.int32, (tm, d_ff), 0)
    if is_sample:
        n_st = (CONV_W - 1) * (tm // t_seq)
        tok = lax.broadcasted_iota(jnp.int32, (tm, n_st), 0)
        srow = lax.broadcasted_iota(jnp.int32, (tm, n_st), 1)
        seq0 = (CONV_W - 1) * (tok // t_seq)
        tpos = tok % t_seq
        prev1 = ((tpos == 0) & (srow == seq0 + 1)).astype(BF16)
        prev2 = ((tpos <= 1) & (srow == seq0 + tpos)).astype(BF16)
        srow_t = lax.broadcasted_iota(jnp.int32, (n_st, tm), 0)
        tok_t = lax.broadcasted_iota(jnp.int32, (n_st, tm), 1)
        keep = (tok_t == t_seq * (srow_t // (CONV_W - 1)) + t_seq - (CONV_W - 1)
                + srow_t % (CONV_W - 1)).astype(BF16)
    else:
        @pl.when(pl.program_id(0) % tiles_per_seq == 0)
        def _():
            tail_ref[...] = jnp.zeros_like(tail_ref)

    def conv(cols):
        u = _dot(hf, wup_ref[:, cols])
        r1 = pltpu.roll(u, 1, 0)
        r2 = pltpu.roll(u, 2, 0)
        if is_sample:
            state = st_ref[:, cols]
            pos = row % t_seq
            u1 = jnp.where(pos >= 1, r1, _exact_onehot_dot(prev1, state))
            u2 = jnp.where(pos >= 2, r2, _exact_onehot_dot(prev2, state))
            st_out[:, cols] = _exact_onehot_dot(keep, u)
        else:
            prev = tail_ref[:, cols]
            last, last2 = prev[SUBLANES - 1:SUBLANES, :], prev[SUBLANES - 2:SUBLANES - 1, :]
            u1 = jnp.where(row >= 1, r1, last)
            u2 = jnp.where(row >= 2, r2, jnp.where(row == 1, last, last2))
            tail_ref[:, cols] = u[tm - SUBLANES:tm, :]
        return (u2 * wc_ref[0:1, cols] + u1 * wc_ref[1:2, cols]) + u * wc_ref[2:3, cols] + cb_ref[:, cols]

    gate = conv(slice(0, d_ff))
    up = conv(slice(d_ff, 2 * d_ff))
    x = x + _dot((gate * _sigmoid(gate) * up).astype(BF16), wdown_ref[...])

    hp = _rms(x, gple_ref[...]).astype(BF16)
    x = x + _sigmoid(_dot(hp, wpg_ref[...])) * _dot(p_ref[...].astype(BF16), wpp_ref[...])
    x_out[...] = _rms(x, gfin_ref[...]) if is_last else x


def _ffn(x, hm, olat, p_all, layer, lw, *, final_gain, conv_state, n_seq, t_seq):
    n, d = x.shape
    d_up = lw['w_up'][0].shape[2]
    is_sample = conv_state is not None
    tm = SAMPLE_FFN_TILE if is_sample else FFN_TILE
    is_last = final_gain is not None
    row = lambda i: (i, 0)
    weights = [lw['w_uv'], lw['w_out'], lw['norm_ffn'], lw['w_up'], lw['ffn_conv'], lw['ffn_conv_b'],
               lw['w_down'], lw['ple_norm'], lw['w_pg'], lw['w_pp']]
    if is_last:
        weights.append(final_gain)
    weight_specs, weight_args = zip(*(_weight(w) for w in weights))
    args = [x, hm, olat, p_all] + list(weight_args)
    in_specs = [pl.BlockSpec((tm, d), row), pl.BlockSpec((tm, 512), row),
                pl.BlockSpec((A_HEADS, tm, KV_RANK), lambda i: (0, i, 0)),
                pl.BlockSpec((None, tm, p_all.shape[2]), lambda i: (layer, i, 0))] + list(weight_specs)
    out_shape = [jax.ShapeDtypeStruct((n, d), F32)]
    out_specs = [pl.BlockSpec((tm, d), row)]
    if is_sample:
        n_st = (CONV_W - 1) * (tm // t_seq)
        args.append(conv_state)
        in_specs.append(pl.BlockSpec((None, n_st, d_up), lambda i: (layer, i, 0)))
        out_shape.append(jax.ShapeDtypeStruct(conv_state.shape[1:], F32))
        out_specs.append(pl.BlockSpec((n_st, d_up), row))
        tiles_per_seq = 1
    else:
        tiles_per_seq = n // n_seq // tm
        out_shape.append(jax.ShapeDtypeStruct((n_seq, SUBLANES, d_up), F32))
        out_specs.append(pl.BlockSpec((None, SUBLANES, d_up), lambda i: (i // tiles_per_seq, 0, 0)))
    return pl.pallas_call(
        functools.partial(_ffn_kernel, is_sample=is_sample, is_last=is_last, t_seq=t_seq,
                          tiles_per_seq=tiles_per_seq),
        grid=(n // tm,),
        in_specs=in_specs,
        out_specs=tuple(out_specs),
        out_shape=tuple(out_shape),
        compiler_params=_params(1),
        name="ffn_sample" if is_sample else "ffn_prompt",
    )(*args)


def _swap_halves(w):
    half = w.shape[-1] // 2
    return jnp.concatenate([w[..., half:], w[..., :half]], axis=-1)


def _stacked_weights(w_out, ffn_w_up, ffn_w_down, ple_w_gate):
    return {'w_out': w_out.astype(BF16), 'w_up': ffn_w_up.astype(BF16),
            'w_down': ffn_w_down.astype(BF16), 'w_pg': ple_w_gate.astype(BF16)}


def _layer_weights(l, stacked, norm_mix, w_in, m_gate_bias, m_norm, mla_q_norm, mla_w_uq, mla_kv_norm, mla_w_uk,
                   mla_w_uv, w_out, norm_ffn, ffn_w_up, ffn_conv, ffn_conv_b, ffn_w_down, ple_norm,
                   ple_w_gate, ple_w_proj):
    d = w_in.shape[1]
    wi = w_in[l]
    sizes = (256, 256, 512, 512, M_HEADS, M_HEADS, Q_RANK, KV_RANK, A_ROPE)
    offs = [0]
    for s in sizes:
        offs.append(offs[-1] + s)
    qm, km, vm, om, im, fm, cq, ckv, kr = (wi[:, offs[i]:offs[i + 1]] for i in range(len(sizes)))
    gate_pad = jnp.zeros((d, LANES - 2 * M_HEADS), wi.dtype)
    w_in_r = jnp.concatenate([fm, im, gate_pad, cq, ckv, kr, _swap_halves(kr), vm, om, qm, km], axis=1)
    uq = mla_w_uq[l].reshape(Q_RANK, A_HEADS, A_NOPE + A_ROPE)
    uq_rope = uq[:, :, A_NOPE:]
    w_uq_r = jnp.concatenate([uq[:, :, :A_NOPE].reshape(Q_RANK, A_HEADS * A_NOPE),
                              jnp.concatenate([uq_rope, _swap_halves(uq_rope)], axis=-1)
                              .reshape(Q_RANK, A_HEADS * LANES)], axis=1)
    row = lambda v: v.reshape(1, -1)
    return {
        'norm_mix': row(norm_mix[l]),
        'w_in': w_in_r.astype(BF16),
        'w_voq_t': w_in_r[:, C_V:C_K].T.astype(BF16),
        'gate_bias': jnp.concatenate([m_gate_bias[l, 1], m_gate_bias[l, 0]]).reshape(SUBLANES, 1),
        'm_norm': m_norm[l],
        'mla_q_norm': row(mla_q_norm[l]),
        'w_uq': w_uq_r.astype(BF16),
        'mla_kv_norm': row(mla_kv_norm[l]),
        'w_uk': jnp.transpose(mla_w_uk[l], (1, 2, 0)).astype(BF16),
        'w_uv': jnp.transpose(mla_w_uv[l], (1, 0, 2)).astype(BF16),
        'w_out': (stacked['w_out'], l),
        'norm_ffn': row(norm_ffn[l]),
        'w_up': (stacked['w_up'], l),
        'ffn_conv': ffn_conv[l],
        'ffn_conv_b': row(ffn_conv_b[l]),
        'w_down': (stacked['w_down'], l),
        'ple_norm': row(ple_norm[l]),
        'w_pg': (stacked['w_pg'], l),
        'w_pp': ple_w_proj[l].astype(BF16),
    }


def _rope_table(pos):
    half = A_ROPE // 2
    freqs = ROPE_THETA ** (-jnp.arange(half, dtype=F32) * 2.0 / A_ROPE)
    ang = pos.astype(F32)[:, None] * freqs[None, :]
    cos, sin = jnp.cos(ang), jnp.sin(ang)
    return jnp.concatenate([cos, cos, -sin, sin], axis=1)


def kernel(x_prompt, x_sample, p_prompt, p_sample, cache_mla, state_mlstm_C, state_mlstm_n, state_mlstm_m,
           state_ffn_conv, page_table, norm_mix, w_in, m_gate_bias, m_norm, mla_q_norm, mla_w_uq, mla_kv_norm,
           mla_w_uk, mla_w_uv, w_out, norm_ffn, ffn_w_up, ffn_conv, ffn_conv_b, ffn_w_down, ple_norm,
           ple_w_gate, ple_w_proj, final_norm):
    depth = w_in.shape[0]
    bp, tp, d = x_prompt.shape
    bs, ts, _ = x_sample.shape
    n_p, n_s = bp * tp, bs * ts
    past_len = page_table.shape[1] * cache_mla.shape[2]
    d_up = ffn_w_up.shape[2]

    tab_p = _rope_table(jnp.arange(tp))
    tab_s = jnp.tile(_rope_table(past_len + jnp.arange(ts)), (TOK_TILE // ts, 1))
    xp = x_prompt.reshape(n_p, d)
    xs = x_sample.reshape(n_s, d)
    final_gain = final_norm.reshape(1, d)
    cache_t = jnp.swapaxes(cache_mla, 2, 3)

    stacked = _stacked_weights(w_out, ffn_w_up, ffn_w_down, ple_w_gate)
    conv_state = state_ffn_conv.reshape(depth, bs * (CONV_W - 1), d_up)
    lat_p, lat_s, c_p, n_pl, m_p, c_s, n_sl, m_s, conv_p, conv_s = ([] for _ in range(10))
    for l in range(depth):
        lw = _layer_weights(l, stacked, norm_mix, w_in, m_gate_bias, m_norm, mla_q_norm, mla_w_uq, mla_kv_norm,
                            mla_w_uk, mla_w_uv, w_out, norm_ffn, ffn_w_up, ffn_conv, ffn_conv_b,
                            ffn_w_down, ple_norm, ple_w_gate, ple_w_proj)
        fin = final_gain if l == depth - 1 else None

        km, qt, vt, ot, gscan, gdiff, dcol, lat, katt, katt_t, qatt_t = _proj(xp, tab_p, lw, True, bp)
        by_seq = lambda a: a.reshape(bp, tp, a.shape[-1])
        hm, state, m_new = _mlstm_prompt(by_seq(km), qt, vt, ot, gscan, gdiff, by_seq(dcol), lw['m_norm'].T, bp)
        c_new = jnp.swapaxes(state[:, :, :M_DV, :], 2, 3)
        n_new = state[:, :, M_DV, :]
        olat = _attn_prompt(qatt_t, katt, katt_t, bp)
        xp, tail = _ffn(xp, hm.reshape(n_p, -1), olat, p_prompt.reshape(depth, n_p, -1), l, lw, final_gain=fin,
                        conv_state=None, n_seq=bp, t_seq=tp)
        lat_p.append(lat.reshape(bp, tp, LAT_DIM))
        c_p.append(c_new)
        n_pl.append(n_new)
        m_p.append(m_new[:, :M_HEADS, 0])
        conv_p.append(tail[:, SUBLANES - (CONV_W - 1):])

        qk, v, o, gates, lat, katt, qatt = _proj(xs, tab_s, lw, False, 1)
        m_prev_rows = jnp.repeat(state_mlstm_m[l].T, ts, axis=1)
        hm, c_new, n_new_t, m_rows = _mlstm_sample(qk, v, o, gates, m_prev_rows, lw['m_norm'],
                                                   state_mlstm_C, jnp.transpose(state_mlstm_n[l], (1, 0, 2)),
                                                   l, ts)
        q_seq = jnp.transpose(qatt.reshape(A_HEADS, bs, ts, LAT_PAD), (1, 0, 2, 3)).reshape(bs, A_HEADS * ts, LAT_PAD)
        o_seq = _attn_sample(q_seq, katt.reshape(bs, ts, LAT_PAD), cache_t, page_table, l)
        olat = jnp.transpose(o_seq.reshape(bs, A_HEADS, ts, KV_RANK), (1, 0, 2, 3)).reshape(A_HEADS, n_s, KV_RANK)
        xs, st_new = _ffn(xs, hm, olat, p_sample.reshape(depth, n_s, -1), l, lw, final_gain=fin,
                          conv_state=conv_state, n_seq=bs, t_seq=ts)
        lat_s.append(lat.reshape(bs, ts, LAT_DIM))
        c_s.append(c_new)
        n_sl.append(jnp.transpose(n_new_t, (1, 0, 2)))
        m_s.append(m_rows[:, ts - 1::ts].T)
        conv_s.append(st_new)

    return (xp.reshape(bp, tp, d), xs.reshape(bs, ts, d), jnp.stack(lat_p), jnp.stack(lat_s),
            jnp.stack(c_p), jnp.stack(n_pl), jnp.stack(m_p), jnp.stack(c_s), jnp.stack(n_sl), jnp.stack(m_s),
            jnp.stack(conv_p), jnp.stack(conv_s).reshape(depth, bs, CONV_W - 1, d_up))
```

```python
import functools

import jax
import jax.numpy as jnp
from jax import lax
from jax.experimental import pallas as pl
from jax.experimental.pallas import tpu as pltpu

F32 = jnp.float32
BF16 = jnp.bfloat16

M_HEADS = 4
M_DK = 64
M_DV = 128
A_HEADS = 4
A_NOPE = 128
A_ROPE = 64
KV_RANK = 256
Q_RANK = 384
LAT_DIM = KV_RANK + A_ROPE
GATE_CAP = 15.0
ROPE_THETA = 10000.0
EPS = 1e-6
CONV_W = 3
ATT_SCALE = (A_NOPE + A_ROPE) ** -0.5
LOG2_E = 1.4426950408889634
Q_SCALE = ATT_SCALE * LOG2_E

LANES = 128
SUBLANES = 8
VMEM_LIMIT_BYTES = 56 * 1024 * 1024

LAT_PAD = 3 * LANES
TOK_TILE = 256
MLSTM_CHUNK = 256
ATT_BLOCK = 1024
SAMPLE_GROUP = 32
FFN_TILE = 512
PROJ_TILE = 1024
PAGE_RING = 4
SAMPLE_FFN_TILE = 128
STATE_ROWS = M_DV + SUBLANES

C_G, C_CQ, C_CKV, C_KR = 0, 128, 512, 768
C_V, C_O, C_Q, C_K = 896, 1408, 1920, 2176
IN_COLS = 2432


def _const_spec(shape):
    nd = len(shape)
    return pl.BlockSpec(shape, lambda *_: (0,) * nd, pipeline_mode=pl.Buffered(1))


def _weight(w):
    if not isinstance(w, tuple):
        return _const_spec(w.shape), w
    arr, layer = w
    tail = (0,) * (arr.ndim - 1)
    return pl.BlockSpec((None,) + arr.shape[1:], lambda *_: (layer,) + tail, pipeline_mode=pl.Buffered(1)), arr


def _params(n_axes):
    return pltpu.CompilerParams(dimension_semantics=("arbitrary",) * n_axes,
                                vmem_limit_bytes=VMEM_LIMIT_BYTES)


def _rms(x, g):
    return x * lax.rsqrt(jnp.mean(x * x, axis=-1, keepdims=True) + EPS) * g


def _sigmoid(x):
    return 1.0 / (1.0 + jnp.exp(-x))


def _dot(a, b):
    return jnp.dot(a, b, preferred_element_type=F32)


def _dot_nt(a, b):
    return lax.dot_general(a, b, (((1,), (1,)), ((), ())), preferred_element_type=F32)


def _fold_halves(p):
    return p + pltpu.roll(p, 64, 1)


def _proj_kernel(x_ref, tab_ref, gmix_ref, win_ref, gb_ref, gq_ref, wuq_ref, gkv_ref, wuk_ref,
                 *refs, feature_major):
    if feature_major:
        (wvoq_ref, km_ref, qt_ref, vt_ref, ot_ref, gscan_ref, gdiff_ref, dcol_ref,
         lat_ref, katt_ref, katt_t_ref, qatt_ref) = refs
    else:
        qk_ref, v_ref, o_ref, gates_ref, lat_ref, katt_ref, qatt_ref = refs
    h = _rms(x_ref[...], gmix_ref[...]).astype(BF16)
    z = _dot(h, win_ref[:, 0:C_V])

    gt = z[:, C_G:C_G + LANES].T[0:SUBLANES, :] + gb_ref[...]
    capped = GATE_CAP * jnp.tanh(gt / GATE_CAP)
    log_sig = jnp.minimum(capped, 0.0) - jnp.log1p(jnp.exp(-jnp.abs(capped)))
    row = lax.broadcasted_iota(jnp.int32, gt.shape, 0)
    gates = jnp.where(row < M_HEADS, log_sig, capped)
    if feature_major:
        b, d, e = _gate_scans(gates, MLSTM_CHUNK)
        gscan_ref[...] = jnp.where(row < M_HEADS, b, pltpu.roll(e, M_HEADS, 0))
        d = jnp.where(row < M_HEADS, d, 0.0)
        gdiff_ref[...] = d
        dcol_ref[...] = _to_cols((d,), d.shape[1])
    else:
        gates_ref[...] = gates

    tab = tab_ref[...]
    lane = lax.broadcasted_iota(jnp.int32, tab.shape, 1)
    ckv_n = _rms(z[:, C_CKV:C_CKV + KV_RANK], gkv_ref[...])
    rk = _fold_halves(z[:, C_KR:C_KR + LANES] * tab)
    lat_ref[:, 0:KV_RANK] = ckv_n
    lat_ref[:, KV_RANK:LAT_DIM] = rk[:, 0:A_ROPE]
    k_cat = jnp.concatenate([ckv_n, jnp.where(lane < A_ROPE, rk, 0.0)], axis=1)
    katt_ref[...] = k_cat.astype(BF16)
    if feature_major:
        katt_t_ref[...] = k_cat.T.astype(BF16)

    qn = _rms(z[:, C_CQ:C_CQ + Q_RANK], gq_ref[...]).astype(BF16)
    qa = _dot(qn, wuq_ref[...])
    for hd in range(A_HEADS):
        q_lat = _dot(qa[:, hd * A_NOPE:(hd + 1) * A_NOPE].astype(BF16), wuk_ref[hd])
        rq = _fold_halves(qa[:, 512 + hd * LANES:512 + (hd + 1) * LANES] * tab)
        q_cat = jnp.concatenate([q_lat, jnp.where(lane < A_ROPE, rq, 0.0)], axis=1) * Q_SCALE
        qatt_ref[hd] = (q_cat.T if feature_major else q_cat).astype(BF16)

    k_m = (_dot(h, win_ref[:, C_K:C_K + 256]) * (M_DK ** -0.5)).astype(BF16)
    if feature_major:
        zt = _dot_nt(wvoq_ref[...], h)
        vt_ref[...] = zt[0:512, :].astype(BF16)
        ot_ref[...] = zt[512:1024, :]
        qt_ref[...] = zt[1024:1280, :].astype(BF16)
        km_ref[...] = k_m
    else:
        z2 = _dot(h, win_ref[:, C_V:C_K])
        v_ref[...] = z2[:, 0:512].astype(BF16)
        o_ref[...] = z2[:, 512:1024]
        qk_ref[:, 0:256] = z2[:, 1024:1280].astype(BF16)
        qk_ref[:, 256:512] = k_m


def _proj(x, tab, lw, feature_major, gate_seqs):
    n = x.shape[0]
    tm = PROJ_TILE if feature_major else TOK_TILE
    tiles_per_seq = n // gate_seqs // tm
    row = lambda i: (i, 0)
    col = lambda i: (0, i)
    band = lambda i: (i // tiles_per_seq, i % tiles_per_seq)
    gate_rows = jax.ShapeDtypeStruct((gate_seqs * SUBLANES, n // gate_seqs), F32)
    if feature_major:
        assert tm % MLSTM_CHUNK == 0
        out_shape = [
            jax.ShapeDtypeStruct((n, 256), BF16),
            jax.ShapeDtypeStruct((256, n), BF16),
            jax.ShapeDtypeStruct((512, n), BF16),
            jax.ShapeDtypeStruct((512, n), F32),
            gate_rows,
            gate_rows,
            jax.ShapeDtypeStruct((n, LANES), F32),
        ]
        out_specs = [
            pl.BlockSpec((tm, 256), row),
            pl.BlockSpec((256, tm), col),
            pl.BlockSpec((512, tm), col),
            pl.BlockSpec((512, tm), col),
            pl.BlockSpec((SUBLANES, tm), band),
            pl.BlockSpec((SUBLANES, tm), band),
            pl.BlockSpec((tm, LANES), row),
        ]
    else:
        out_shape = [
            jax.ShapeDtypeStruct((n, 512), BF16),
            jax.ShapeDtypeStruct((n, 512), BF16),
            jax.ShapeDtypeStruct((n, 512), F32),
            gate_rows,
        ]
        out_specs = [
            pl.BlockSpec((tm, 512), row),
            pl.BlockSpec((tm, 512), row),
            pl.BlockSpec((tm, 512), row),
            pl.BlockSpec((SUBLANES, tm), band),
        ]
    out_shape += [jax.ShapeDtypeStruct((n, LAT_DIM), F32),
                  jax.ShapeDtypeStruct((n, LAT_PAD), BF16)]
    out_specs += [pl.BlockSpec((tm, LAT_DIM), row), pl.BlockSpec((tm, LAT_PAD), row)]
    if feature_major:
        out_shape += [jax.ShapeDtypeStruct((LAT_PAD, n), BF16),
                      jax.ShapeDtypeStruct((A_HEADS, LAT_PAD, n), BF16)]
        out_specs += [pl.BlockSpec((LAT_PAD, tm), col),
                      pl.BlockSpec((A_HEADS, LAT_PAD, tm), lambda i: (0, 0, i))]
    else:
        out_shape.append(jax.ShapeDtypeStruct((A_HEADS, n, LAT_PAD), BF16))
        out_specs.append(pl.BlockSpec((A_HEADS, tm, LAT_PAD), lambda i: (0, i, 0)))
    return pl.pallas_call(
        functools.partial(_proj_kernel, feature_major=feature_major),
        grid=(n // tm,),
        in_specs=[
            pl.BlockSpec((tm, x.shape[1]), row),
            pl.BlockSpec((tm, LANES), lambda i: (i % (tab.shape[0] // tm), 0)),
            _const_spec(lw['norm_mix'].shape),
            _const_spec(lw['w_in'].shape),
            _const_spec(lw['gate_bias'].shape),
            _const_spec(lw['mla_q_norm'].shape),
            _const_spec(lw['w_uq'].shape),
            _const_spec(lw['mla_kv_norm'].shape),
            _const_spec(lw['w_uk'].shape),
        ] + ([_const_spec(lw['w_voq_t'].shape)] if feature_major else []),
        out_specs=tuple(out_specs),
        out_shape=tuple(out_shape),
        compiler_params=_params(1),
        name="proj_prompt" if feature_major else "proj_sample",
    )(x, tab, lw['norm_mix'], lw['w_in'], lw['gate_bias'], lw['mla_q_norm'], lw['w_uq'],
      lw['mla_kv_norm'], lw['w_uk'], *([lw['w_voq_t']] if feature_major else []))


def _seg_scan(x, op, fill, seg):
    n = x.shape[1]
    lane = lax.broadcasted_iota(jnp.int32, x.shape, 1)
    pos = lane if seg is None else lane % seg
    span = n if seg is None else seg
    s = 1
    while s < span:
        x = op(x, jnp.where(pos >= s, pltpu.roll(x, s, 1), fill))
        s *= 2
    return x


def _gate_scans(gates, seg):
    ig = pltpu.roll(gates, M_HEADS, 0)
    b = _seg_scan(gates, jnp.add, 0.0, seg)
    d = ig - b
    e = _seg_scan(d, jnp.maximum, -jnp.inf, seg)
    return b, d, e


def _gate_rows(gates, m_prev, seg):
    b, d, e = _gate_scans(gates, seg)
    return b, d, pltpu.roll(gates, M_HEADS, 0), b + jnp.maximum(m_prev, e)


def _to_cols(row_blocks, n_tok):
    pad = jnp.zeros((LANES - SUBLANES * len(row_blocks), n_tok), F32)
    return jnp.concatenate(list(row_blocks) + [pad], axis=0).T


def _head_out(hh, og, gain):
    hn = hh * lax.rsqrt(jnp.mean(hh * hh, axis=-1, keepdims=True) + EPS) * gain
    return (hn * _sigmoid(og)).astype(BF16)


def _mlstm_prompt_kernel(*refs, n_seq):
    km_ref, gscan_ref, gdiff_ref, dcol_ref, gain_ref = refs[:5]
    qt_refs = refs[5:5 + n_seq]
    vt_refs = refs[5 + n_seq:5 + 2 * n_seq]
    ot_refs = refs[5 + 2 * n_seq:5 + 3 * n_seq]
    hm_ref, s_ref, m_ref = refs[5 + 3 * n_seq:]
    L = km_ref.shape[1]

    @pl.when(pl.program_id(0) == 0)
    def _():
        s_ref[...] = jnp.zeros_like(s_ref)
        m_ref[...] = jnp.zeros_like(m_ref)

    ki = lax.broadcasted_iota(jnp.int32, (L, L), 0)
    qi = lax.broadcasted_iota(jnp.int32, (L, L), 1)
    causal = ki <= qi
    pad_rows = jnp.zeros((STATE_ROWS - M_DV - 1, L), F32)

    for sq in range(n_seq):
        band = slice(sq * SUBLANES, (sq + 1) * SUBLANES)
        scan = gscan_ref[band, :]
        diff = gdiff_ref[band, :]
        ks = [km_ref[sq, :, hd * M_DK:(hd + 1) * M_DK] for hd in range(M_HEADS)]
        qts = [qt_refs[sq][hd * M_DK:(hd + 1) * M_DK, :] for hd in range(M_HEADS)]
        states = [s_ref[sq, hd] for hd in range(M_HEADS)]
        kq = [_dot(ks[hd], qts[hd]) for hd in range(M_HEADS)]
        q_state = [_dot(states[hd].astype(BF16), qts[hd]) for hd in range(M_HEADS)]

        for hd in range(M_HEADS):
            vt = vt_refs[sq][hd * M_DV:(hd + 1) * M_DV, :]
            b_r = scan[hd:hd + 1, :]
            e_r = scan[M_HEADS + hd:M_HEADS + hd + 1, :]
            d_r = diff[hd:hd + 1, :]
            d_c = dcol_ref[sq, :, hd:hd + 1]
            mp = m_ref[sq, hd:hd + 1, 0:1]
            mx_r = jnp.maximum(mp, e_r)
            mx_last = mx_r[:, L - 1:L]

            st = kq[hd] * jnp.exp(jnp.where(causal, d_c - mx_r, -jnp.inf))
            a_r = jnp.exp(mp - mx_r)
            num = _dot(vt, st.astype(BF16)) + a_r * q_state[hd][0:M_DV, :]
            nq = jnp.sum(st, axis=0, keepdims=True) + a_r * q_state[hd][M_DV:M_DV + 1, :]
            inv = 1.0 / jnp.maximum(jnp.abs(nq), jnp.exp(-(b_r + mx_r)))
            rs = lax.rsqrt(inv * inv * jnp.mean(num * num, axis=0, keepdims=True) + EPS)
            ht = num * (inv * rs) * gain_ref[:, hd:hd + 1] * _sigmoid(ot_refs[sq][hd * M_DV:(hd + 1) * M_DV, :])
            hm_ref[sq, :, hd * M_DV:(hd + 1) * M_DV] = ht.T.astype(BF16)

            w_r = jnp.exp(d_r - mx_last)
            lhs = jnp.concatenate([vt.astype(F32) * w_r, w_r, pad_rows], axis=0).astype(BF16)
            s_ref[sq, hd] = jnp.exp(mp - mx_last) * states[hd] + _dot(lhs, ks[hd])
            m_ref[sq, hd:hd + 1, :] = jnp.broadcast_to(b_r[:, L - 1:L] + mx_last, (1, LANES))


def _mlstm_prompt(km, qt, vt, ot, gscan, gdiff, dcol, gain_t, n_seq):
    t = km.shape[1]
    L = MLSTM_CHUNK
    nc = t // L
    tok = lambda c: (0, c, 0)
    whole4 = lambda c: (0, 0, 0, 0)
    whole3 = lambda c: (0, 0, 0)

    def seq_cols(rows):
        return [pl.BlockSpec((rows, L), lambda c, sq=sq: (0, sq * nc + c)) for sq in range(n_seq)]

    return pl.pallas_call(
        functools.partial(_mlstm_prompt_kernel, n_seq=n_seq),
        grid=(nc,),
        in_specs=[
            pl.BlockSpec((n_seq, L, 256), tok),
            pl.BlockSpec((n_seq * SUBLANES, L), lambda c: (0, c)),
            pl.BlockSpec((n_seq * SUBLANES, L), lambda c: (0, c)),
            pl.BlockSpec((n_seq, L, LANES), tok),
            _const_spec(gain_t.shape),
        ] + seq_cols(256) + seq_cols(512) + seq_cols(512),
        out_specs=(
            pl.BlockSpec((n_seq, L, 512), tok),
            pl.BlockSpec((n_seq, M_HEADS, STATE_ROWS, M_DK), whole4),
            pl.BlockSpec((n_seq, SUBLANES, LANES), whole3),
        ),
        out_shape=(
            jax.ShapeDtypeStruct((n_seq, t, 512), BF16),
            jax.ShapeDtypeStruct((n_seq, M_HEADS, STATE_ROWS, M_DK), F32),
            jax.ShapeDtypeStruct((n_seq, SUBLANES, LANES), F32),
        ),
        compiler_params=_params(1),
        name="mlstm_prompt",
    )(km, gscan, gdiff, dcol, gain_t, *([qt] * n_seq), *([vt] * n_seq), *([ot] * n_seq))


def _exact_onehot_dot(onehot, x):
    hi = x.astype(BF16)
    r1 = x - hi.astype(F32)
    mid = r1.astype(BF16)
    lo = (r1 - mid.astype(F32)).astype(BF16)
    return (_dot(onehot, hi) + _dot(onehot, mid)) + _dot(onehot, lo)


def _mlstm_sample_kernel(qk_ref, v_ref, o_ref, g_ref, mp_ref, gain_ref, c0_ref, n0_ref,
                         hm_ref, c_ref, n_ref, m_ref, cols_ref, *, t_seq):
    L = qk_ref.shape[0]
    G = L // t_seq
    mp_rows = jnp.concatenate([mp_ref[...], jnp.zeros((SUBLANES - M_HEADS, L), F32)], axis=0)
    b, d, ig, m = _gate_rows(g_ref[...], mp_rows, t_seq)
    m_ref[...] = m[0:M_HEADS, :]

    lane = lax.broadcasted_iota(jnp.int32, b.shape, 1)
    pos = lane % t_seq

    def seg_last(x):
        s = 1
        while s < t_seq:
            x = jnp.where(pos >= t_seq - s, x, pltpu.roll(x, L - s, 1))
            s *= 2
        return x

    b_last = seg_last(b)
    m_new = seg_last(m)
    w_rows = jnp.exp(b_last - b + ig - m_new)
    decay_rows = jnp.exp(b_last + mp_rows - m_new)
    cols_ref[...] = _to_cols((b, m, mp_rows, w_rows, decay_rows), L)
    cols = cols_ref[...]
    seq_cols = cols_ref[pl.ds(0, G, stride=t_seq), :]

    qi = lax.broadcasted_iota(jnp.int32, (L, L), 0)
    ki = lax.broadcasted_iota(jnp.int32, (L, L), 1)
    mask = (ki <= qi) & (ki // t_seq == qi // t_seq)
    er = lax.broadcasted_iota(jnp.int32, (L, G * M_DK), 0)
    ec = lax.broadcasted_iota(jnp.int32, (L, G * M_DK), 1)
    own = (ec // M_DK) == (er // t_seq)
    tok_of_seq = (lax.broadcasted_iota(jnp.int32, (L, G), 0) // t_seq
                  == lax.broadcasted_iota(jnp.int32, (L, G), 1)).astype(BF16)
    seq_of_tok = (lax.broadcasted_iota(jnp.int32, (G, L), 1) // t_seq
                  == lax.broadcasted_iota(jnp.int32, (G, L), 0)).astype(BF16)

    def expand(x):
        x2 = jnp.concatenate([x, x], axis=1)
        return jnp.where(own, jnp.tile(x2, (1, G // 2)), 0.0)

    for hd in range(M_HEADS):
        q = qk_ref[:, hd * M_DK:(hd + 1) * M_DK]
        k = qk_ref[:, 256 + hd * M_DK:256 + (hd + 1) * M_DK]
        v = v_ref[:, hd * M_DV:(hd + 1) * M_DV]
        b_c = cols[:, hd:hd + 1]
        m_c = cols[:, 8 + hd:9 + hd]
        mp_c = cols[:, 16 + hd:17 + hd]
        w_c = cols[:, 24 + hd:25 + hd]
        decay_seq = seq_cols[:, 32 + hd:33 + hd]
        n_old = n0_ref[hd]
        c_old = c0_ref[:, hd].reshape(G * M_DK, M_DV)

        p = jnp.exp(jnp.where(mask, b_c + d[hd:hd + 1, :] - m_c, -jnp.inf))
        s = _dot_nt(q, k) * p
        a_c = jnp.exp(b_c + mp_c - m_c)
        qf = q.astype(F32)
        num = _dot(s.astype(BF16), v) + a_c * _dot(expand(qf).astype(BF16), c_old.astype(BF16))
        n_tok = _exact_onehot_dot(tok_of_seq, n_old)
        nq = jnp.sum(s, axis=1, keepdims=True) + a_c * jnp.sum(qf * n_tok, axis=1, keepdims=True)
        den = jnp.maximum(jnp.abs(nq), jnp.exp(-m_c))
        sl = slice(hd * M_DV, (hd + 1) * M_DV)
        hm_ref[:, sl] = _head_out(num / den, o_ref[:, sl], gain_ref[hd:hd + 1, :])

        kw = k.astype(F32) * w_c
        upd = _dot(expand(kw).T.astype(BF16), v)
        for g in range(G):
            c_ref[g, hd] = decay_seq[g:g + 1, :] * c0_ref[g, hd] + upd[g * M_DK:(g + 1) * M_DK, :]
        n_ref[hd] = decay_seq * n_old + _dot(seq_of_tok, kw.astype(BF16))


def _mlstm_sample(qk, v, o, gates, m_prev_rows, gain, c0, n0_t, layer, t_seq):
    n = qk.shape[0]
    G = SAMPLE_GROUP
    L = G * t_seq
    n_seq = n // t_seq
    tok = lambda i: (i, 0)
    return pl.pallas_call(
        functools.partial(_mlstm_sample_kernel, t_seq=t_seq),
        grid=(n // L,),
        in_specs=[
            pl.BlockSpec((L, 512), tok),
            pl.BlockSpec((L, 512), tok),
            pl.BlockSpec((L, 512), tok),
            pl.BlockSpec((SUBLANES, L), lambda i: (0, i)),
            pl.BlockSpec((M_HEADS, L), lambda i: (0, i)),
            _const_spec(gain.shape),
            pl.BlockSpec((None, G, M_HEADS, M_DK, M_DV), lambda i: (layer, i, 0, 0, 0)),
            pl.BlockSpec((M_HEADS, G, M_DK), lambda i: (0, i, 0)),
        ],
        out_specs=(
            pl.BlockSpec((L, 512), tok),
            pl.BlockSpec((G, M_HEADS, M_DK, M_DV), lambda i: (i, 0, 0, 0)),
            pl.BlockSpec((M_HEADS, G, M_DK), lambda i: (0, i, 0)),
            pl.BlockSpec((M_HEADS, L), lambda i: (0, i)),
        ),
        out_shape=(
            jax.ShapeDtypeStruct((n, 512), BF16),
            jax.ShapeDtypeStruct((n_seq, M_HEADS, M_DK, M_DV), F32),
            jax.ShapeDtypeStruct((M_HEADS, n_seq, M_DK), F32),
            jax.ShapeDtypeStruct((M_HEADS, n), F32),
        ),
        scratch_shapes=[pltpu.VMEM((L, LANES), F32)],
        compiler_params=_params(1),
        name="mlstm_sample",
    )(qk, v, o, gates, m_prev_rows, gain, c0, n0_t)


def _attn_prompt_kernel(qt_ref, k_ref, kt_ref, o_ref, m_scr, l_scr, acc_scr, s0_scr):
    tq = qt_ref.shape[2]
    qb = pl.program_id(1)
    m_scr[...] = jnp.full_like(m_scr, -jnp.inf)
    l_scr[...] = jnp.zeros_like(l_scr)
    acc_scr[...] = jnp.zeros_like(acc_scr)

    def keys(kb):
        return k_ref[pl.ds(pl.multiple_of(kb * tq, tq), tq), :]

    s0_scr[...] = _dot(keys(0), qt_ref[0])

    def block(kb, last):
        kblk = keys(kb)
        vt = kt_ref[0:KV_RANK, pl.ds(pl.multiple_of(kb * tq, tq), tq)]
        scores = {0: s0_scr[...]}
        s0_next = None
        for hd in range(A_HEADS):
            if hd + 1 < A_HEADS:
                scores[hd + 1] = _dot(kblk, qt_ref[hd + 1])
            elif not last:
                s0_next = _dot(keys(kb + 1), qt_ref[0])
            st = scores.pop(hd)
            if last:
                kpos = lax.broadcasted_iota(jnp.int32, st.shape, 0)
                qpos = lax.broadcasted_iota(jnp.int32, st.shape, 1)
                st = jnp.where(kpos <= qpos, st, -jnp.inf)
            m_old = m_scr[hd:hd + 1, :]
            m_new = jnp.maximum(m_old, jnp.max(st, axis=0, keepdims=True))
            alpha = jnp.exp2(m_old - m_new)
            p = jnp.exp2(st - m_new)
            l_scr[hd:hd + 1, :] = alpha * l_scr[hd:hd + 1, :] + jnp.sum(p, axis=0, keepdims=True)
            acc_scr[hd] = alpha * acc_scr[hd] + _dot(vt, p.astype(BF16))
            m_scr[hd:hd + 1, :] = m_new
        if not last:
            s0_scr[...] = s0_next

    def body(kb, carry):
        block(kb, False)
        return carry

    lax.fori_loop(0, qb, body, 0)
    block(qb, True)
    for hd in range(A_HEADS):
        o_ref[hd] = (acc_scr[hd] / l_scr[hd:hd + 1, :]).T.astype(BF16)


def _attn_prompt(qatt_t, katt, katt_t, n_seq):
    n = katt.shape[0]
    t = n // n_seq
    tq = ATT_BLOCK
    nq = t // tq
    return pl.pallas_call(
        _attn_prompt_kernel,
        grid=(n_seq, nq),
        in_specs=[
            pl.BlockSpec((A_HEADS, LAT_PAD, tq), lambda b, i: (0, 0, b * nq + i)),
            pl.BlockSpec((t, LAT_PAD), lambda b, i: (b, 0)),
            pl.BlockSpec((LAT_PAD, t), lambda b, i: (0, b)),
        ],
        out_specs=pl.BlockSpec((A_HEADS, tq, KV_RANK), lambda b, i: (0, b * nq + i, 0)),
        out_shape=jax.ShapeDtypeStruct((A_HEADS, n, KV_RANK), BF16),
        scratch_shapes=[pltpu.VMEM((SUBLANES, tq), F32), pltpu.VMEM((SUBLANES, tq), F32),
                        pltpu.VMEM((A_HEADS, KV_RANK, tq), F32), pltpu.VMEM((tq, tq), F32)],
        compiler_params=_params(2),
        name="attn_prompt",
    )(qatt_t, katt, katt_t)


def _attn_sample_kernel(pt_ref, q_ref, knew_ref, cache_ref, o_ref, pages, kbuf, sems, *, n_pages, t_seq, layer):
    s = pl.program_id(0)
    n_seq = pl.num_programs(0)
    page = pages.shape[3]

    def page_copy(seq, j):
        slot = seq % PAGE_RING
        return pltpu.make_async_copy(cache_ref.at[layer, pt_ref[seq, j]], pages.at[slot, j], sems.at[slot])

    def request(seq):
        for j in range(n_pages):
            page_copy(seq, j).start(priority=j % 2)

    @pl.when(s == 0)
    def _():
        for ahead in range(PAGE_RING - 1):
            request(ahead)

    for j in range(n_pages):
        page_copy(s, j).wait()

    @pl.when(s + PAGE_RING - 1 < n_seq)
    def _():
        request(s + PAGE_RING - 1)

    slot = s % PAGE_RING
    for j in range(n_pages):
        kbuf[:, j * page:(j + 1) * page] = pages[slot, j].astype(BF16)

    q = q_ref[...]
    rows = q.shape[0]
    s_past = _dot(q[:, 0:LAT_DIM], kbuf[...])
    qf = q.astype(F32)
    knew = knew_ref[...].astype(F32)
    tq = lax.broadcasted_iota(jnp.int32, (rows, 1), 0) % t_seq
    s_new = []
    for j in range(t_seq):
        sj = jnp.sum(qf * knew[j:j + 1, :], axis=1, keepdims=True)
        s_new.append(jnp.where(tq >= j, sj, -jnp.inf))
    m = jnp.max(s_past, axis=1, keepdims=True)
    for sj in s_new:
        m = jnp.maximum(m, sj)
    p_past = jnp.exp2(s_past - m)
    p_new = [jnp.exp2(sj - m) for sj in s_new]
    denom = jnp.sum(p_past, axis=1, keepdims=True)
    for pj in p_new:
        denom = denom + pj
    inv = 1.0 / denom
    out = _dot_nt((p_past * inv).astype(BF16), kbuf[0:KV_RANK, :])
    for j in range(t_seq):
        pj = (p_new[j] * inv).astype(BF16).astype(F32)
        out = out + pj * knew[j:j + 1, 0:KV_RANK]
    o_ref[...] = out.astype(BF16)


def _attn_sample(q_seq, knew_seq, cache_t, page_table, layer):
    n_seq, rows, _ = q_seq.shape
    t_seq = knew_seq.shape[1]
    n_pages = page_table.shape[1]
    page = cache_t.shape[3]
    assert n_seq >= PAGE_RING - 1
    grid_spec = pltpu.PrefetchScalarGridSpec(
        num_scalar_prefetch=1,
        grid=(n_seq,),
        in_specs=[pl.BlockSpec((None, rows, LAT_PAD), lambda s, pt: (s, 0, 0)),
                  pl.BlockSpec((None, t_seq, LAT_PAD), lambda s, pt: (s, 0, 0)),
                  pl.BlockSpec(memory_space=pl.ANY)],
        out_specs=pl.BlockSpec((None, rows, KV_RANK), lambda s, pt: (s, 0, 0)),
        scratch_shapes=[pltpu.VMEM((PAGE_RING, n_pages, LAT_DIM, page), F32),
                        pltpu.VMEM((LAT_DIM, n_pages * page), BF16),
                        pltpu.SemaphoreType.DMA((PAGE_RING,))],
    )
    return pl.pallas_call(
        functools.partial(_attn_sample_kernel, n_pages=n_pages, t_seq=t_seq, layer=layer),
        grid_spec=grid_spec,
        out_shape=jax.ShapeDtypeStruct((n_seq, rows, KV_RANK), BF16),
        compiler_params=_params(1),
        name="attn_sample",
    )(page_table, q_seq, knew_seq, cache_t)


def _ffn_kernel(*refs, is_sample, is_last, t_seq, tiles_per_seq):
    (x_ref, hm_ref, ol_ref, p_ref, wuv_ref, wout_ref, gffn_ref, wup_ref, wc_ref, cb_ref,
     wdown_ref, gple_ref, wpg_ref, wpp_ref) = refs[:14]
    refs = refs[14:]
    if is_last:
        gfin_ref, refs = refs[0], refs[1:]
    if is_sample:
        st_ref, x_out, st_out = refs
    else:
        x_out, tail_ref = refs
    tm = x_ref.shape[0]
    d_ff = wdown_ref.shape[0]

    ha = [_dot(ol_ref[hd], wuv_ref[hd]).astype(BF16) for hd in range(A_HEADS)]
    mix = jnp.concatenate([hm_ref[...]] + ha, axis=1)
    x = x_ref[...] + _dot(mix, wout_ref[...])

    hf = _rms(x, gffn_ref[...]).astype(BF16)
    row = lax.broadcasted_iota(jnp.int32, (tm, d_ff), 0)
    if is_sample:
        n_st = (CONV_W - 1) * (tm // t_seq)
        tok = lax.broadcasted_iota(jnp.int32, (tm, n_st), 0)
        srow = lax.broadcasted_iota(jnp.int32, (tm, n_st), 1)
        seq0 = (CONV_W - 1) * (tok // t_seq)
        tpos = tok % t_seq
        prev1 = ((tpos == 0) & (srow == seq0 + 1)).astype(BF16)
        prev2 = ((tpos <= 1) & (srow == seq0 + tpos)).astype(BF16)
        srow_t = lax.broadcasted_iota(jnp.int32, (n_st, tm), 0)
        tok_t = lax.broadcasted_iota(jnp.int32, (n_st, tm), 1)
        keep = (tok_t == t_seq * (srow_t // (CONV_W - 1)) + t_seq - (CONV_W - 1)
                + srow_t % (CONV_W - 1)).astype(BF16)
    else:
        @pl.when(pl.program_id(0) % tiles_per_seq == 0)
        def _():
            tail_ref[...] = jnp.zeros_like(tail_ref)

    def conv(cols):
        u = _dot(hf, wup_ref[:, cols])
        r1 = pltpu.roll(u, 1, 0)
        r2 = pltpu.roll(u, 2, 0)
        if is_sample:
            state = st_ref[:, cols]
            pos = row % t_seq
            u1 = jnp.where(pos >= 1, r1, _exact_onehot_dot(prev1, state))
            u2 = jnp.where(pos >= 2, r2, _exact_onehot_dot(prev2, state))
            st_out[:, cols] = _exact_onehot_dot(keep, u)
        else:
            prev = tail_ref[:, cols]
            last, last2 = prev[SUBLANES - 1:SUBLANES, :], prev[SUBLANES - 2:SUBLANES - 1, :]
            u1 = jnp.where(row >= 1, r1, last)
            u2 = jnp.where(row >= 2, r2, jnp.where(row == 1, last, last2))
            tail_ref[:, cols] = u[tm - SUBLANES:tm, :]
        return (u2 * wc_ref[0:1, cols] + u1 * wc_ref[1:2, cols]) + u * wc_ref[2:3, cols] + cb_ref[:, cols]

    gate = conv(slice(0, d_ff))
    up = conv(slice(d_ff, 2 * d_ff))
    x = x + _dot((gate * _sigmoid(gate) * up).astype(BF16), wdown_ref[...])

    hp = _rms(x, gple_ref[...]).astype(BF16)
    x = x + _sigmoid(_dot(hp, wpg_ref[...])) * _dot(p_ref[...].astype(BF16), wpp_ref[...])
    x_out[...] = _rms(x, gfin_ref[...]) if is_last else x


def _ffn(x, hm, olat, p_all, layer, lw, *, final_gain, conv_state, n_seq, t_seq):
    n, d = x.shape
    d_up = lw['w_up'][0].shape[2]
    is_sample = conv_state is not None
    tm = SAMPLE_FFN_TILE if is_sample else FFN_TILE
    is_last = final_gain is not None
    row = lambda i: (i, 0)
    weights = [lw['w_uv'], lw['w_out'], lw['norm_ffn'], lw['w_up'], lw['ffn_conv'], lw['ffn_conv_b'],
               lw['w_down'], lw['ple_norm'], lw['w_pg'], lw['w_pp']]
    if is_last:
        weights.append(final_gain)
    weight_specs, weight_args = zip(*(_weight(w) for w in weights))
    args = [x, hm, olat, p_all] + list(weight_args)
    in_specs = [pl.BlockSpec((tm, d), row), pl.BlockSpec((tm, 512), row),
                pl.BlockSpec((A_HEADS, tm, KV_RANK), lambda i: (0, i, 0)),
                pl.BlockSpec((None, tm, p_all.shape[2]), lambda i: (layer, i, 0))] + list(weight_specs)
    out_shape = [jax.ShapeDtypeStruct((n, d), F32)]
    out_specs = [pl.BlockSpec((tm, d), row)]
    if is_sample:
        n_st = (CONV_W - 1) * (tm // t_seq)
        args.append(conv_state)
        in_specs.append(pl.BlockSpec((None, n_st, d_up), lambda i: (layer, i, 0)))
        out_shape.append(jax.ShapeDtypeStruct(conv_state.shape[1:], F32))
        out_specs.append(pl.BlockSpec((n_st, d_up), row))
        tiles_per_seq = 1
    else:
        tiles_per_seq = n // n_seq // tm
        out_shape.append(jax.ShapeDtypeStruct((n_seq, SUBLANES, d_up), F32))
        out_specs.append(pl.BlockSpec((None, SUBLANES, d_up), lambda i: (i // tiles_per_seq, 0, 0)))
    return pl.pallas_call(
        functools.partial(_ffn_kernel, is_sample=is_sample, is_last=is_last, t_seq=t_seq,
                          tiles_per_seq=tiles_per_seq),
        grid=(n // tm,),
        in_specs=in_specs,
        out_specs=tuple(out_specs),
        out_shape=tuple(out_shape),
        compiler_params=_params(1),
        name="ffn_sample" if is_sample else "ffn_prompt",
    )(*args)


def _swap_halves(w):
    half = w.shape[-1] // 2
    return jnp.concatenate([w[..., half:], w[..., :half]], axis=-1)


def _stacked_weights(w_out, ffn_w_up, ffn_w_down, ple_w_gate):
    return {'w_out': w_out.astype(BF16), 'w_up': ffn_w_up.astype(BF16),
            'w_down': ffn_w_down.astype(BF16), 'w_pg': ple_w_gate.astype(BF16)}


def _layer_weights(l, stacked, norm_mix, w_in, m_gate_bias, m_norm, mla_q_norm, mla_w_uq, mla_kv_norm, mla_w_uk,
                   mla_w_uv, w_out, norm_ffn, ffn_w_up, ffn_conv, ffn_conv_b, ffn_w_down, ple_norm,
                   ple_w_gate, ple_w_proj):
    d = w_in.shape[1]
    wi = w_in[l]
    sizes = (256, 256, 512, 512, M_HEADS, M_HEADS, Q_RANK, KV_RANK, A_ROPE)
    offs = [0]
    for s in sizes:
        offs.append(offs[-1] + s)
    qm, km, vm, om, im, fm, cq, ckv, kr = (wi[:, offs[i]:offs[i + 1]] for i in range(len(sizes)))
    gate_pad = jnp.zeros((d, LANES - 2 * M_HEADS), wi.dtype)
    w_in_r = jnp.concatenate([fm, im, gate_pad, cq, ckv, kr, _swap_halves(kr), vm, om, qm, km], axis=1)
    uq = mla_w_uq[l].reshape(Q_RANK, A_HEADS, A_NOPE + A_ROPE)
    uq_rope = uq[:, :, A_NOPE:]
    w_uq_r = jnp.concatenate([uq[:, :, :A_NOPE].reshape(Q_RANK, A_HEADS * A_NOPE),
                              jnp.concatenate([uq_rope, _swap_halves(uq_rope)], axis=-1)
                              .reshape(Q_RANK, A_HEADS * LANES)], axis=1)
    row = lambda v: v.reshape(1, -1)
    return {
        'norm_mix': row(norm_mix[l]),
        'w_in': w_in_r.astype(BF16),
        'w_voq_t': w_in_r[:, C_V:C_K].T.astype(BF16),
        'gate_bias': jnp.concatenate([m_gate_bias[l, 1], m_gate_bias[l, 0]]).reshape(SUBLANES, 1),
        'm_norm': m_norm[l],
        'mla_q_norm': row(mla_q_norm[l]),
        'w_uq': w_uq_r.astype(BF16),
        'mla_kv_norm': row(mla_kv_norm[l]),
        'w_uk': jnp.transpose(mla_w_uk[l], (1, 2, 0)).astype(BF16),
        'w_uv': jnp.transpose(mla_w_uv[l], (1, 0, 2)).astype(BF16),
        'w_out': (stacked['w_out'], l),
        'norm_ffn': row(norm_ffn[l]),
        'w_up': (stacked['w_up'], l),
        'ffn_conv': ffn_conv[l],
        'ffn_conv_b': row(ffn_conv_b[l]),
        'w_down': (stacked['w_down'], l),
        'ple_norm': row(ple_norm[l]),
        'w_pg': (stacked['w_pg'], l),
        'w_pp': ple_w_proj[l].astype(BF16),
    }


def _rope_table(pos):
    half = A_ROPE // 2
    freqs = ROPE_THETA ** (-jnp.arange(half, dtype=F32) * 2.0 / A_ROPE)
    ang = pos.astype(F32)[:, None] * freqs[None, :]
    cos, sin = jnp.cos(ang), jnp.sin(ang)
    return jnp.concatenate([cos, cos, -sin, sin], axis=1)


def kernel(x_prompt, x_sample, p_prompt, p_sample, cache_mla, state_mlstm_C, state_mlstm_n, state_mlstm_m,
           state_ffn_conv, page_table, norm_mix, w_in, m_gate_bias, m_norm, mla_q_norm, mla_w_uq, mla_kv_norm,
           mla_w_uk, mla_w_uv, w_out, norm_ffn, ffn_w_up, ffn_conv, ffn_conv_b, ffn_w_down, ple_norm,
           ple_w_gate, ple_w_proj, final_norm):
    depth = w_in.shape[0]
    bp, tp, d = x_prompt.shape
    bs, ts, _ = x_sample.shape
    n_p, n_s = bp * tp, bs * ts
    past_len = page_table.shape[1] * cache_mla.shape[2]
    d_up = ffn_w_up.shape[2]

    tab_p = _rope_table(jnp.arange(tp))
    tab_s = jnp.tile(_rope_table(past_len + jnp.arange(ts)), (TOK_TILE // ts, 1))
    xp = x_prompt.reshape(n_p, d)
    xs = x_sample.reshape(n_s, d)
    final_gain = final_norm.reshape(1, d)
    cache_t = jnp.swapaxes(cache_mla, 2, 3)

    stacked = _stacked_weights(w_out, ffn_w_up, ffn_w_down, ple_w_gate)
    conv_state = state_ffn_conv.reshape(depth, bs * (CONV_W - 1), d_up)
    lat_p, lat_s, c_p, n_pl, m_p, c_s, n_sl, m_s, conv_p, conv_s = ([] for _ in range(10))
    for l in range(depth):
        lw = _layer_weights(l, stacked, norm_mix, w_in, m_gate_bias, m_norm, mla_q_norm, mla_w_uq, mla_kv_norm,
                            mla_w_uk, mla_w_uv, w_out, norm_ffn, ffn_w_up, ffn_conv, ffn_conv_b,
                            ffn_w_down, ple_norm, ple_w_gate, ple_w_proj)
        fin = final_gain if l == depth - 1 else None

        km, qt, vt, ot, gscan, gdiff, dcol, lat, katt, katt_t, qatt_t = _proj(xp, tab_p, lw, True, bp)
        by_seq = lambda a: a.reshape(bp, tp, a.shape[-1])
        hm, state, m_new = _mlstm_prompt(by_seq(km), qt, vt, ot, gscan, gdiff, by_seq(dcol), lw['m_norm'].T, bp)
        c_new = jnp.swapaxes(state[:, :, :M_DV, :], 2, 3)
        n_new = state[:, :, M_DV, :]
        olat = _attn_prompt(qatt_t, katt, katt_t, bp)
        xp, tail = _ffn(xp, hm.reshape(n_p, -1), olat, p_prompt.reshape(depth, n_p, -1), l, lw, final_gain=fin,
                        conv_state=None, n_seq=bp, t_seq=tp)
        lat_p.append(lat.reshape(bp, tp, LAT_DIM))
        c_p.append(c_new)
        n_pl.append(n_new)
        m_p.append(m_new[:, :M_HEADS, 0])
        conv_p.append(tail[:, SUBLANES - (CONV_W - 1):])

        qk, v, o, gates, lat, katt, qatt = _proj(xs, tab_s, lw, False, 1)
        m_prev_rows = jnp.repeat(state_mlstm_m[l].T, ts, axis=1)
        hm, c_new, n_new_t, m_rows = _mlstm_sample(qk, v, o, gates, m_prev_rows, lw['m_norm'],
                                                   state_mlstm_C, jnp.transpose(state_mlstm_n[l], (1, 0, 2)),
                                                   l, ts)
        q_seq = jnp.transpose(qatt.reshape(A_HEADS, bs, ts, LAT_PAD), (1, 0, 2, 3)).reshape(bs, A_HEADS * ts, LAT_PAD)
        o_seq = _attn_sample(q_seq, katt.reshape(bs, ts, LAT_PAD), cache_t, page_table, l)
        olat = jnp.transpose(o_seq.reshape(bs, A_HEADS, ts, KV_RANK), (1, 0, 2, 3)).reshape(A_HEADS, n_s, KV_RANK)
        xs, st_new = _ffn(xs, hm, olat, p_sample.reshape(depth, n_s, -1), l, lw, final_gain=fin,
                          conv_state=conv_state, n_seq=bs, t_seq=ts)
        lat_s.append(lat.reshape(bs, ts, LAT_DIM))
        c_s.append(c_new)
        n_sl.append(jnp.transpose(n_new_t, (1, 0, 2)))
        m_s.append(m_rows[:, ts - 1::ts].T)
        conv_s.append(st_new)

    return (xp.reshape(bp, tp, d), xs.reshape(bs, ts, d), jnp.stack(lat_p), jnp.stack(lat_s),
            jnp.stack(c_p), jnp.stack(n_pl), jnp.stack(m_p), jnp.stack(c_s), jnp.stack(n_sl), jnp.stack(m_s),
            jnp.stack(conv_p), jnp.stack(conv_s).reshape(depth, bs, CONV_W - 1, d_up))
```
